```python
import math
import jax
import jax.numpy as jnp
from jax import lax
import numpy as np

D_MODEL = 2048
BATCH = 4
SEQ = 4096
DEPTH = 2

D_MIX = D_MODEL
GROUP_W = D_MIX // 4
LRU_W = GROUP_W
LRU_BLOCKS = 8
LRU_BLOCK_W = LRU_W // LRU_BLOCKS
LRU_CONV = 4
LRU_C = 8.0
ATT_HEADS = 8
ATT_KV_HEADS = 2
ATT_GROUP = ATT_HEADS // ATT_KV_HEADS
ATT_HEAD_DIM = GROUP_W // ATT_HEADS
ROPE_AXIS = ATT_HEAD_DIM // 2
ROPE_THETA = 10000.0
Q_BLOCK = 128
GRID_W = 64
HY_W = GROUP_W
HY_CONV = 3
HY_ORDER = 2
HY_BANDS = 8
HY_EMB = 2 * HY_BANDS + 1
HY_FFN = 64
HY_TARGET = 1e-2
HY_FAST_DECAY = 0.3
HY_SLOW_DECAY = 1.5
ML_HEADS = 4
ML_HEAD_DIM = GROUP_W // ML_HEADS
ML_CHUNK = 128
D_FF = -(-8 * D_MODEL // 768) * 256
EPS = 1e-6
IN_SIZES = (LRU_W, LRU_W, ATT_HEADS * ATT_HEAD_DIM, ATT_KV_HEADS * ATT_HEAD_DIM, ATT_KV_HEADS * ATT_HEAD_DIM, 3 * HY_W, GROUP_W, GROUP_W, GROUP_W, GROUP_W, 4 * ML_HEADS)
D_IN = sum(IN_SIZES)

kernel_name = 'hybrid_parallel_group_encoder'


def rmsnorm(x, g):
    xf = x.astype(jnp.float32)
    y = xf * lax.rsqrt(jnp.mean(xf * xf, axis=-1, keepdims=True) + EPS)
    return (y * g.astype(jnp.float32)).astype(x.dtype)


def dwconv_centred(x, w, b):
    K = w.shape[0]
    S = x.shape[1]
    left = K // 2
    xp = jnp.pad(x, ((0, 0), (left, K - 1 - left), (0, 0)))
    out = b
    for j in range(K):
        out = out + xp[:, j:j + S, :] * w[j]
    return out


def _linrec_combine(left, right):
    a1, u1 = left
    a2, u2 = right
    return a1 * a2, a2 * u1 + u2


def rglru_direction(xc, wa, ba, wx, bx, lam, reverse):
    Bn, S, W = xc.shape
    xb = xc.reshape(Bn, S, LRU_BLOCKS, LRU_BLOCK_W)
    r = jax.nn.sigmoid((jnp.einsum('bsnk,nkj->bsnj', xb, wa).reshape(Bn, S, W) + ba).astype(jnp.float32))
    i = jax.nn.sigmoid((jnp.einsum('bsnk,nkj->bsnj', xb, wx).reshape(Bn, S, W) + bx).astype(jnp.float32))
    log_a = -LRU_C * jax.nn.softplus(-lam.astype(jnp.float32)) * r
    a = jnp.exp(log_a)
    u = jnp.sqrt(-jnp.expm1(2.0 * log_a)) * (i * xc.astype(jnp.float32))
    _, h = lax.associative_scan(_linrec_combine, (a, u), reverse=reverse, axis=1)
    return h


def mixer_rglru(xa, ga, conv_w, conv_b, wa, ba, wx, bx, lam):
    xc = dwconv_centred(xa, conv_w, conv_b)
    h = (rglru_direction(xc, wa[0], ba[0], wx[0], bx[0], lam[0], False)
         + rglru_direction(xc, wa[1], ba[1], wx[1], bx[1], lam[1], True))
    return jax.nn.gelu(ga.astype(jnp.float32)) * h


def axial_angles(S):
    rows = S // GRID_W
    row = jnp.repeat(jnp.arange(rows, dtype=jnp.float32), GRID_W)
    col = jnp.tile(jnp.arange(GRID_W, dtype=jnp.float32), rows)
    inv = ROPE_THETA ** (-jnp.arange(0, ROPE_AXIS, 2, dtype=jnp.float32) / ROPE_AXIS)
    return row[:, None] * inv, col[:, None] * inv


def rotate_pairs(x, ang):
    m = x.shape[-1] // 2
    xf = x.astype(jnp.float32)
    cos = jnp.cos(ang)[None, :, None, :]
    sin = jnp.sin(ang)[None, :, None, :]
    x1, x2 = xf[..., :m], xf[..., m:]
    return jnp.concatenate([x1 * cos - x2 * sin, x2 * cos + x1 * sin], axis=-1).astype(x.dtype)


def axial_rope(x, ang_row, ang_col):
    return jnp.concatenate([rotate_pairs(x[..., :ROPE_AXIS], ang_row), rotate_pairs(x[..., ROPE_AXIS:], ang_col)], axis=-1)


def mixer_attention(q, k, v, q_g, k_g):
    Bn, S, _ = q.shape
    q = rmsnorm(q.reshape(Bn, S, ATT_HEADS, ATT_HEAD_DIM), q_g)
    k = rmsnorm(k.reshape(Bn, S, ATT_KV_HEADS, ATT_HEAD_DIM), k_g)
    v = v.reshape(Bn, S, ATT_KV_HEADS, ATT_HEAD_DIM)
    ang_row, ang_col = axial_angles(S)
    q = axial_rope(q, ang_row, ang_col) * (ATT_HEAD_DIM ** -0.5)
    k = axial_rope(k, ang_row, ang_col)
    qh = q.reshape(Bn, S, ATT_KV_HEADS, ATT_GROUP, ATT_HEAD_DIM).transpose(0, 2, 3, 1, 4)
    kh = k.transpose(0, 2, 1, 3)
    vh = v.transpose(0, 2, 1, 3)
    nb = S // Q_BLOCK
    qb = jnp.moveaxis(qh.reshape(Bn, ATT_KV_HEADS, ATT_GROUP, nb, Q_BLOCK, ATT_HEAD_DIM), 3, 0)

    def attend(qi):
        s = jnp.einsum('bhgqd,bhkd->bhgqk', qi, kh).astype(jnp.float32)
        p = jax.nn.softmax(s, axis=-1).astype(vh.dtype)
        return jnp.einsum('bhgqk,bhkd->bhgqd', p, vh)

    o = lax.map(attend, qb)
    return o.transpose(1, 0, 4, 2, 3, 5).reshape(Bn, S, ATT_HEADS * ATT_HEAD_DIM)


def hyena_filter_spectra(L, w1, b1, w2, b2, w3, sin_freq, decay):
    f32 = jnp.float32
    pos = jnp.arange(L, dtype=f32)
    t = pos / max(L - 1, 1)
    bands = jnp.linspace(1e-4, HY_BANDS - 1, HY_BANDS, dtype=f32)
    ang = (2.0 * math.pi * pos / L)[:, None] * bands
    feat = jnp.concatenate([t[:, None], jnp.cos(ang), -jnp.sin(ang)], axis=-1)
    sf = sin_freq.astype(f32)
    h = jnp.sin(sf * (feat @ w1.astype(f32) + b1.astype(f32)))
    h = jnp.sin(sf * (h @ w2.astype(f32) + b2.astype(f32)))
    h = h @ w3.astype(f32)
    h = h * jnp.exp(-t[:, None] * jnp.abs(decay.astype(f32)))
    h = h.reshape(L, HY_ORDER, 2, HY_W)
    h = h * lax.rsqrt(jnp.sum(h * h, axis=(0, 2), keepdims=True) + EPS)
    fwd, bwd = h[:, :, 0], h[:, :, 1]
    two_sided = jnp.concatenate([fwd, jnp.zeros((1, HY_ORDER, HY_W), f32), bwd[:0:-1]], axis=0)
    return jnp.moveaxis(jnp.fft.rfft(two_sided, axis=0), 1, 0)


def long_conv(u, h_spec, skip):
    S = u.shape[1]
    y = jnp.fft.irfft(jnp.fft.rfft(u, n=2 * S, axis=1) * h_spec[None], n=2 * S, axis=1)[:, :S]
    return y + u * skip


def mixer_hyena(u3, conv_w, conv_b, w1, b1, w2, b2, w3, sin_freq, decay, skip):
    z = dwconv_centred(u3, conv_w, conv_b).astype(jnp.float32)
    v, x1, x2 = jnp.split(z, 3, axis=-1)
    h_spec = hyena_filter_spectra(u3.shape[1], w1, b1, w2, b2, w3, sin_freq, decay)
    sk = skip.astype(jnp.float32)
    z = x1 * long_conv(v, h_spec[0], sk[0])
    z = x2 * long_conv(z, h_spec[1], sk[1])
    return z


def mlstm_chunkwise(q, k, v, log_i, log_f):
    Bn, H, S, Dh = q.shape
    nc = S // ML_CHUNK
    q = q.reshape(Bn, H, nc, ML_CHUNK, Dh)
    k = k.reshape(Bn, H, nc, ML_CHUNK, Dh)
    v = v.reshape(Bn, H, nc, ML_CHUNK, Dh)
    li = log_i.reshape(Bn, H, nc, ML_CHUNK)
    b = jnp.cumsum(log_f.reshape(Bn, H, nc, ML_CHUNK), axis=-1)
    b_tot = b[..., -1]
    causal = jnp.tril(jnp.ones((ML_CHUNK, ML_CHUNK), dtype=bool))
    d_intra = jnp.where(causal, b[..., :, None] - b[..., None, :] + li[..., None, :], -jnp.inf)
    w_end = b_tot[..., None] - b + li
    m_loc = jnp.max(w_end, axis=-1)
    e_end = jnp.exp(w_end - m_loc[..., None])
    dC = jnp.einsum('bhnl,bhnlk,bhnlv->bhnkv', e_end, k, v)
    dn = jnp.einsum('bhnl,bhnlk->bhnk', e_end, k)

    def step(carry, inp):
        C, n, m = carry
        dC_c, dn_c, mloc_c, bt_c = inp
        m_new = jnp.maximum(bt_c + m, mloc_c)
        decay = jnp.exp(bt_c + m - m_new)
        gain = jnp.exp(mloc_c - m_new)
        C_new = decay[..., None, None] * C + gain[..., None, None] * dC_c
        n_new = decay[..., None] * n + gain[..., None] * dn_c
        return (C_new, n_new, m_new), (C, n, m)

    init = (jnp.zeros((Bn, H, Dh, Dh), jnp.float32), jnp.zeros((Bn, H, Dh), jnp.float32), jnp.zeros((Bn, H), jnp.float32))
    xs = (jnp.moveaxis(dC, 2, 0), jnp.moveaxis(dn, 2, 0), jnp.moveaxis(m_loc, 2, 0), jnp.moveaxis(b_tot, 2, 0))
    _, (C_prev, n_prev, m_prev) = lax.scan(step, init, xs)
    C_prev = jnp.moveaxis(C_prev, 0, 2)
    n_prev = jnp.moveaxis(n_prev, 0, 2)
    m_prev = jnp.moveaxis(m_prev, 0, 2)
    m_inter = b + m_prev[..., None]
    m_t = jnp.maximum(m_inter, jnp.max(d_intra, axis=-1))
    e_inter = jnp.exp(m_inter - m_t)
    s = jnp.einsum('bhnlk,bhnsk->bhnls', q, k) * jnp.exp(d_intra - m_t[..., None])
    num = jnp.einsum('bhnls,bhnsv->bhnlv', s, v) + e_inter[..., None] * jnp.einsum('bhnlk,bhnkv->bhnlv', q, C_prev)
    den = jnp.sum(s, axis=-1) + e_inter * jnp.einsum('bhnlk,bhnk->bhnl', q, n_prev)
    h = num / jnp.maximum(jnp.abs(den), jnp.exp(-m_t))[..., None]
    return h.reshape(Bn, H, S, Dh)


def mixer_mlstm(q, k, v, o, gates, norm_g):
    Bn, S, _ = q.shape

    def heads(a):
        return a.astype(jnp.float32).reshape(Bn, S, ML_HEADS, ML_HEAD_DIM).transpose(0, 2, 1, 3)

    qh = heads(q) * (ML_HEAD_DIM ** -0.5)
    kh = heads(k)
    vh = heads(v)
    g = gates.astype(jnp.float32).transpose(0, 2, 1)
    i_f, f_f, i_b, f_b = jnp.split(g, 4, axis=1)
    h_f = mlstm_chunkwise(qh, kh, vh, i_f, jax.nn.log_sigmoid(f_f))
    fl = lambda a: jnp.flip(a, axis=2)
    h_b = fl(mlstm_chunkwise(fl(qh), fl(kh), fl(vh), fl(i_b), fl(jax.nn.log_sigmoid(f_b))))
    h = (h_f + h_b).transpose(0, 2, 1, 3)
    h = rmsnorm(h, norm_g.reshape(ML_HEADS, ML_HEAD_DIM))
    return jax.nn.sigmoid(o.astype(jnp.float32)) * h.reshape(Bn, S, GROUP_W)


def swiglu(h, w1, w3, w2):
    return (jax.nn.silu(h @ w1) * (h @ w3)) @ w2


def setup_inputs(seed: int = 0) -> dict:
    key = jax.random.key(seed)
    ks = jax.random.split(key, 40)
    f32 = jnp.float32

    def nrm(i, shape, s):
        return s * jax.random.normal(ks[i], shape, f32)

    x = nrm(0, (BATCH, SEQ, D_MODEL), 1.0)
    c = nrm(1, (BATCH, D_MODEL), 1.0)
    w_in = nrm(2, (DEPTH, D_MODEL, D_IN), D_MODEL ** -0.5)
    g0 = D_IN - 4 * ML_HEADS
    f_bias = jnp.linspace(3.0, 6.0, ML_HEADS, dtype=f32)
    b_in = nrm(3, (DEPTH, D_IN), 0.01)
    b_in = b_in.at[:, g0 + ML_HEADS:g0 + 2 * ML_HEADS].add(f_bias).at[:, g0 + 3 * ML_HEADS:].add(f_bias)
    w_out = nrm(4, (DEPTH, D_MIX, D_MODEL), D_MIX ** -0.5)
    norm_mix_g = 1.0 + nrm(5, (DEPTH, D_MODEL), 0.05)
    norm_ffn_g = 1.0 + nrm(6, (DEPTH, D_MODEL), 0.05)
    ada_w = nrm(7, (DEPTH, D_MODEL, 6 * D_MODEL), D_MODEL ** -0.5)
    ada_b = nrm(8, (DEPTH, 6 * D_MODEL), 0.01)
    lru_conv_w = nrm(9, (DEPTH, LRU_CONV, LRU_W), LRU_CONV ** -0.5)
    lru_conv_b = nrm(10, (DEPTH, LRU_W), 0.01)
    lru_wa = nrm(11, (DEPTH, 2, LRU_BLOCKS, LRU_BLOCK_W, LRU_BLOCK_W), LRU_BLOCK_W ** -0.5)
    lru_ba = nrm(12, (DEPTH, 2, LRU_W), 0.01)
    lru_wx = nrm(13, (DEPTH, 2, LRU_BLOCKS, LRU_BLOCK_W, LRU_BLOCK_W), LRU_BLOCK_W ** -0.5)
    lru_bx = nrm(14, (DEPTH, 2, LRU_W), 0.01)
    a_c = jax.random.uniform(ks[15], (DEPTH, 2, LRU_W), f32, minval=0.9, maxval=0.999)
    a0 = a_c ** (1.0 / LRU_C)
    lru_lambda = jnp.log(a0) - jnp.log1p(-a0)
    att_q_norm_g = 1.0 + nrm(16, (DEPTH, ATT_HEAD_DIM), 0.05)
    att_k_norm_g = 1.0 + nrm(17, (DEPTH, ATT_HEAD_DIM), 0.05)
    hy_conv_w = nrm(18, (DEPTH, HY_CONV, 3 * HY_W), HY_CONV ** -0.5)
    hy_conv_b = nrm(19, (DEPTH, 3 * HY_W), 0.01)
    hy_w1 = nrm(20, (DEPTH, HY_EMB, HY_FFN), HY_EMB ** -0.5)
    hy_b1 = nrm(21, (DEPTH, HY_FFN), 0.1)
    hy_w2 = nrm(22, (DEPTH, HY_FFN, HY_FFN), HY_FFN ** -0.5)
    hy_b2 = nrm(23, (DEPTH, HY_FFN), 0.1)
    hy_w3 = nrm(24, (DEPTH, HY_FFN, HY_ORDER * 2 * HY_W), HY_FFN ** -0.5)
    hy_sin_freq = 1.0 + nrm(25, (DEPTH, HY_FFN), 0.05)
    base_decay = jnp.abs(jnp.linspace(math.log(HY_TARGET) / HY_SLOW_DECAY, math.log(HY_TARGET) / HY_FAST_DECAY, HY_W, dtype=f32))
    hy_decay = jnp.tile(base_decay, HY_ORDER * 2)[None, :] * (1.0 + nrm(26, (DEPTH, HY_ORDER * 2 * HY_W), 0.05))
    hy_skip = nrm(27, (DEPTH, HY_ORDER, HY_W), 1.0)
    ml_norm_g = 1.0 + nrm(28, (DEPTH, GROUP_W), 0.05)
    ffn_w1 = nrm(29, (DEPTH, D_MODEL, D_FF), D_MODEL ** -0.5)
    ffn_w3 = nrm(30, (DEPTH, D_MODEL, D_FF), D_MODEL ** -0.5)
    ffn_w2 = nrm(31, (DEPTH, D_FF, D_MODEL), D_FF ** -0.5)
    final_g = 1.0 + nrm(32, (D_MODEL,), 0.05)
    return {'x': x, 'c': c, 'w_in': w_in, 'b_in': b_in, 'w_out': w_out,
            'norm_mix_g': norm_mix_g, 'norm_ffn_g': norm_ffn_g, 'ada_w': ada_w, 'ada_b': ada_b,
            'lru_conv_w': lru_conv_w, 'lru_conv_b': lru_conv_b, 'lru_wa': lru_wa, 'lru_ba': lru_ba,
            'lru_wx': lru_wx, 'lru_bx': lru_bx, 'lru_lambda': lru_lambda,
            'att_q_norm_g': att_q_norm_g, 'att_k_norm_g': att_k_norm_g,
            'hy_conv_w': hy_conv_w, 'hy_conv_b': hy_conv_b, 'hy_w1': hy_w1, 'hy_b1': hy_b1,
            'hy_w2': hy_w2, 'hy_b2': hy_b2, 'hy_w3': hy_w3, 'hy_sin_freq': hy_sin_freq,
            'hy_decay': hy_decay, 'hy_skip': hy_skip, 'ml_norm_g': ml_norm_g,
            'ffn_w1': ffn_w1, 'ffn_w3': ffn_w3, 'ffn_w2': ffn_w2, 'final_g': final_g}


def reference(x, c, w_in, b_in, w_out, norm_mix_g, norm_ffn_g, ada_w, ada_b,
              lru_conv_w, lru_conv_b, lru_wa, lru_ba, lru_wx, lru_bx, lru_lambda,
              att_q_norm_g, att_k_norm_g, hy_conv_w, hy_conv_b, hy_w1, hy_b1, hy_w2, hy_b2,
              hy_w3, hy_sin_freq, hy_decay, hy_skip, ml_norm_g, ffn_w1, ffn_w3, ffn_w2, final_g):
    split_points = np.cumsum(IN_SIZES)[:-1].tolist()
    c_act = jax.nn.silu(c)
    for l in range(DEPTH):
        mod = c_act @ ada_w[l] + ada_b[l]
        sh1, sc1, g1, sh2, sc2, g2 = [m[:, None, :] for m in jnp.split(mod, 6, axis=-1)]
        h = rmsnorm(x, norm_mix_g[l]) * (1.0 + sc1) + sh1
        proj = h @ w_in[l] + b_in[l]
        a_x, a_g, b_q, b_k, b_v, c_u, d_q, d_k, d_v, d_o, d_gates = jnp.split(proj, split_points, axis=-1)
        y_a = mixer_rglru(a_x, a_g, lru_conv_w[l], lru_conv_b[l], lru_wa[l], lru_ba[l], lru_wx[l], lru_bx[l], lru_lambda[l])
        y_b = mixer_attention(b_q, b_k, b_v, att_q_norm_g[l], att_k_norm_g[l])
        y_c = mixer_hyena(c_u, hy_conv_w[l], hy_conv_b[l], hy_w1[l], hy_b1[l], hy_w2[l], hy_b2[l], hy_w3[l], hy_sin_freq[l], hy_decay[l], hy_skip[l])
        y_d = mixer_mlstm(d_q, d_k, d_v, d_o, d_gates, ml_norm_g[l])
        y = jnp.concatenate([y_a.astype(x.dtype), y_b.astype(x.dtype), y_c.astype(x.dtype), y_d.astype(x.dtype)], axis=-1)
        x = x + g1 * (y @ w_out[l])
        h = rmsnorm(x, norm_ffn_g[l]) * (1.0 + sc2) + sh2
        x = x + g2 * swiglu(h, ffn_w1[l], ffn_w3[l], ffn_w2[l])
    return rmsnorm(x, final_g)
```

```python
import functools
import math

import numpy as np
import jax
import jax.numpy as jnp
from jax import lax
from jax.experimental import pallas as pl
from jax.experimental.pallas import tpu as pltpu

F32 = jnp.float32
BF16 = jnp.bfloat16
HIGHEST = lax.Precision.HIGHEST

D_MODEL = 2048
SEQ = 4096
DEPTH = 2
GROUP_W = 512
LRU_BLOCKS = 8
LRU_BLOCK_W = GROUP_W // LRU_BLOCKS
LRU_C = 8.0
ATT_HEADS = 8
ATT_KV_HEADS = 2
ATT_GROUP = ATT_HEADS // ATT_KV_HEADS
ATT_HEAD_DIM = 64
ROPE_AXIS = ATT_HEAD_DIM // 2
ROPE_THETA = 10000.0
GRID_W = 64
HY_W = GROUP_W
HY_ORDER = 2
HY_BANDS = 8
HY_EMB = 2 * HY_BANDS + 1
HY_EMB_PAD = 32
HY_FFN = 64
ML_HEADS = 4
ML_HEAD_DIM = 128
ML_CHUNK = 128
D_FF = 5632
EPS = 1e-6
IN_SIZES = (512, 512, 512, 128, 128, 1536, 512, 512, 512, 512, 16)
D_IN = sum(IN_SIZES)

COL_AX, COL_AG, COL_BQ, COL_CU, COL_DQ, COL_DK, COL_DV, COL_DO = 0, 512, 1024, 1536, 3072, 3584, 4096, 4608
COL_BK, COL_BV, COL_GATES = 5120, 5248, 5376
D_IN_PAD = 5632

FFT_N = 2 * SEQ
FFT_N1 = 64
FFT_N2 = 128

VMEM_LIMIT = 56 * 1024 * 1024


def _cparams(sem, vmem=VMEM_LIMIT):
    return pltpu.CompilerParams(dimension_semantics=sem, vmem_limit_bytes=vmem)


def _bdot(a, b):
    return jnp.dot(a.astype(BF16), b.astype(BF16), preferred_element_type=F32)


def _sigmoid(x):
    return jax.nn.sigmoid(x)


def _log_sigmoid(x):
    return jnp.minimum(x, 0.0) - jnp.log1p(jnp.exp(-jnp.abs(x)))


def _softplus(x):
    return jnp.maximum(x, 0.0) + jnp.log1p(jnp.exp(-jnp.abs(x)))


def _ada_kernel(c_ref, w_ref, b_ref, o_ref):
    c = c_ref[...]
    o_ref[...] = _bdot(c * _sigmoid(c), w_ref[...]) + b_ref[...]


def _ada_all(c, ada_w, ada_b):
    bsz = c.shape[0]
    rows = 8
    cp = jnp.zeros((rows, D_MODEL), F32).at[:bsz].set(c)
    tn = 1024
    out = pl.pallas_call(
        _ada_kernel,
        grid=(DEPTH, 6 * D_MODEL // tn),
        in_specs=[pl.BlockSpec((rows, D_MODEL), lambda l, j: (0, 0)),
                  pl.BlockSpec((None, D_MODEL, tn), lambda l, j: (l, 0, j)),
                  pl.BlockSpec((None, 1, tn), lambda l, j: (l, 0, j))],
        out_specs=pl.BlockSpec((None, rows, tn), lambda l, j: (l, 0, j)),
        out_shape=jax.ShapeDtypeStruct((DEPTH, rows, 6 * D_MODEL), F32),
        compiler_params=_cparams(("parallel", "parallel")),
        name="ada_mod",
    )(cp, ada_w, ada_b.reshape(DEPTH, 1, 6 * D_MODEL))
    return out[:, :bsz].reshape(DEPTH, bsz, 6, D_MODEL)


def _rms_mod(x, g, scale, shift):
    ms = jnp.mean(x * x, axis=-1, keepdims=True)
    return (x * lax.rsqrt(ms + EPS) * g) * (1.0 + scale) + shift


def _inproj_kernel(x_ref, m_ref, g_ref, w_ref, b_ref, o_ref, h_scr):
    @pl.when(pl.program_id(1) == 0)
    def _():
        h_scr[...] = _rms_mod(x_ref[...], g_ref[...], m_ref[1:2, :], m_ref[0:1, :]).astype(BF16)

    o_ref[...] = jnp.dot(h_scr[...], w_ref[...], preferred_element_type=F32) + b_ref[...]


def _inproj(x2, mod_l, g, w, b):
    n = x2.shape[0]
    tm, tn = 1024, 512
    per_b = SEQ // tm
    return pl.pallas_call(
        _inproj_kernel,
        grid=(n // tm, D_IN_PAD // tn),
        in_specs=[pl.BlockSpec((tm, D_MODEL), lambda i, j: (i, 0)),
                  pl.BlockSpec((None, 6, D_MODEL), lambda i, j: (i // per_b, 0, 0)),
                  pl.BlockSpec((1, D_MODEL), lambda i, j: (0, 0)),
                  pl.BlockSpec((D_MODEL, tn), lambda i, j: (0, j)),
                  pl.BlockSpec((1, tn), lambda i, j: (0, j))],
        out_specs=pl.BlockSpec((tm, tn), lambda i, j: (i, j)),
        out_shape=jax.ShapeDtypeStruct((n, D_IN_PAD), F32),
        scratch_shapes=[pltpu.VMEM((tm, D_MODEL), BF16)],
        compiler_params=_cparams(("parallel", "arbitrary")),
        name="in_proj",
    )(x2, mod_l, g.reshape(1, D_MODEL), w, b)


def _shifted(ext, off, rows):
    total = ext.shape[0]
    if off == 0:
        return ext[8:8 + rows]
    return pltpu.roll(ext, (-off) % total, axis=0)[8:8 + rows]


def _dwconv_tile(x_ref, p_ref, n_ref, w_ref, b_ref, tile, n_tiles, left):
    rows = x_ref.shape[0]
    prev = jnp.where(tile > 0, p_ref[...], 0.0)
    nxt = jnp.where(tile < n_tiles - 1, n_ref[...], 0.0)
    ext = jnp.concatenate([prev, x_ref[...], nxt], axis=0)
    out = b_ref[...]
    for j in range(w_ref.shape[0]):
        out = out + _shifted(ext, j - left, rows) * w_ref[j:j + 1, :]
    return out


def _halo_specs(tile_rows, width, col_block, tile_of, n_tiles):
    r8 = tile_rows // 8
    last8 = SEQ // 8 - 1

    def main(b, t):
        return (b, tile_of(t), col_block)

    def prev(b, t):
        return (b, jnp.maximum(tile_of(t) * r8 - 1, 0), col_block)

    def nxt(b, t):
        return (b, jnp.minimum((tile_of(t) + 1) * r8, last8), col_block)

    return [pl.BlockSpec((None, tile_rows, width), main),
            pl.BlockSpec((None, 8, width), prev),
            pl.BlockSpec((None, 8, width), nxt)]


def _lru_gates(xc, wg_ref, bg_ref, lam_ref):
    gates = _bdot(xc, wg_ref[...]) + bg_ref[...]
    r = _sigmoid(gates[:, :GROUP_W])
    i = _sigmoid(gates[:, GROUP_W:])
    log_a = (-LRU_C * _softplus(-lam_ref[...])) * r
    a = jnp.exp(log_a)
    th = jnp.tanh(log_a)
    u = jnp.sqrt(-2.0 * th / (1.0 - th)) * (i * xc)
    return a, u


def _lru_scan(a_scr, u_scr, h_ref, carry0, reverse):
    rows = a_scr.shape[0]
    n_chunks = rows // 8
    ridx = lax.broadcasted_iota(jnp.int32, (8, GROUP_W), 0)

    def body(c, carry):
        cc = (n_chunks - 1 - c) if reverse else c
        r0 = pl.multiple_of(cc * 8, 8)
        a = a_scr[pl.ds(r0, 8), :]
        u = u_scr[pl.ds(r0, 8), :]
        for k in (1, 2, 4):
            if reverse:
                keep = ridx < 8 - k
                sh = 8 - k
            else:
                keep = ridx >= k
                sh = k
            a_sh = jnp.where(keep, pltpu.roll(a, sh, axis=0), 1.0)
            u_sh = jnp.where(keep, pltpu.roll(u, sh, axis=0), 0.0)
            u = a * u_sh + u
            a = a * a_sh
        h = u + a * carry
        h_ref[pl.ds(r0, 8), :] = h
        return h[0:1, :] if reverse else h[7:8, :]

    return lax.fori_loop(0, n_chunks, body, carry0)


def _lru_fwd_kernel(x_ref, p_ref, n_ref, cw_ref, cb_ref, wg_ref, bg_ref, lam_ref, h_ref,
                    a_scr, u_scr, c_scr, *, n_tiles):
    t = pl.program_id(1)
    xc = _dwconv_tile(x_ref, p_ref, n_ref, cw_ref, cb_ref, t, n_tiles, 2)
    a, u = _lru_gates(xc, wg_ref, bg_ref, lam_ref)
    a_scr[...] = a
    u_scr[...] = u

    @pl.when(t == 0)
    def _():
        c_scr[...] = jnp.zeros_like(c_scr)

    carry = _lru_scan(a_scr, u_scr, h_ref, c_scr[0:1, :], False)
    c_scr[0:1, :] = carry


def _lru_bwd_kernel(x_ref, p_ref, n_ref, ga_ref, hf_ref, cw_ref, cb_ref, wg_ref, bg_ref, lam_ref, y_ref,
                    a_scr, u_scr, hb_scr, c_scr, *, n_tiles):
    t = pl.program_id(1)
    tile = n_tiles - 1 - t
    xc = _dwconv_tile(x_ref, p_ref, n_ref, cw_ref, cb_ref, tile, n_tiles, 2)
    a, u = _lru_gates(xc, wg_ref, bg_ref, lam_ref)
    a_scr[...] = a
    u_scr[...] = u

    @pl.when(t == 0)
    def _():
        c_scr[...] = jnp.zeros_like(c_scr)

    carry = _lru_scan(a_scr, u_scr, hb_scr, c_scr[0:1, :], True)
    c_scr[0:1, :] = carry
    h = hf_ref[...] + hb_scr[...]
    y_ref[...] = (jax.nn.gelu(ga_ref[...]) * h).astype(y_ref.dtype)


def _block_diag(w):
    nb, k, j = w.shape
    eye = jnp.eye(nb, dtype=w.dtype)
    return jnp.einsum('nkj,nm->nkmj', w, eye).reshape(nb * k, nb * j)


def _mixer_rglru(proj3, conv_w, conv_b, wa, ba, wx, bx, lam):
    bsz = proj3.shape[0]
    ts = 512
    n_tiles = SEQ // ts
    wgs = [jnp.concatenate([_block_diag(wa[d]), _block_diag(wx[d])], axis=1).astype(BF16) for d in range(2)]
    bgs = [jnp.concatenate([ba[d], bx[d]]).reshape(1, 2 * GROUP_W) for d in range(2)]
    cb = conv_b.reshape(1, GROUP_W)
    small = lambda shape: pl.BlockSpec(shape, lambda b, t: (0, 0))
    wspecs = [small((4, GROUP_W)), small((1, GROUP_W)), small((GROUP_W, 2 * GROUP_W)),
              small((1, 2 * GROUP_W)), small((1, GROUP_W))]
    tile_spec = lambda col, tile_of: pl.BlockSpec((None, ts, GROUP_W), lambda b, t: (b, tile_of(t), col))

    fwd_of = lambda t: t
    hf = pl.pallas_call(
        functools.partial(_lru_fwd_kernel, n_tiles=n_tiles),
        grid=(bsz, n_tiles),
        in_specs=_halo_specs(ts, GROUP_W, COL_AX // GROUP_W, fwd_of, n_tiles) + wspecs,
        out_specs=pl.BlockSpec((None, ts, GROUP_W), lambda b, t: (b, t, 0)),
        out_shape=jax.ShapeDtypeStruct((bsz, SEQ, GROUP_W), F32),
        scratch_shapes=[pltpu.VMEM((ts, GROUP_W), F32), pltpu.VMEM((ts, GROUP_W), F32),
                        pltpu.VMEM((8, GROUP_W), F32)],
        compiler_params=_cparams(("parallel", "arbitrary")),
        name="rglru_fwd",
    )(proj3, proj3, proj3, conv_w, cb, wgs[0], bgs[0], lam[0].reshape(1, GROUP_W))

    bwd_of = lambda t: n_tiles - 1 - t
    return pl.pallas_call(
        functools.partial(_lru_bwd_kernel, n_tiles=n_tiles),
        grid=(bsz, n_tiles),
        in_specs=_halo_specs(ts, GROUP_W, COL_AX // GROUP_W, bwd_of, n_tiles)
        + [tile_spec(COL_AG // GROUP_W, bwd_of), tile_spec(0, bwd_of)] + wspecs,
        out_specs=pl.BlockSpec((None, ts, GROUP_W), lambda b, t: (b, n_tiles - 1 - t, 0)),
        out_shape=jax.ShapeDtypeStruct((bsz, SEQ, GROUP_W), BF16),
        scratch_shapes=[pltpu.VMEM((ts, GROUP_W), F32), pltpu.VMEM((ts, GROUP_W), F32),
                        pltpu.VMEM((ts, GROUP_W), F32), pltpu.VMEM((8, GROUP_W), F32)],
        compiler_params=_cparams(("parallel", "arbitrary")),
        name="rglru_bwd",
    )(proj3, proj3, proj3, proj3, hf, conv_w, cb, wgs[1], bgs[1], lam[1].reshape(1, GROUP_W))


def _split_dot(x, m_ref):
    hi = x.astype(BF16)
    lo = (x - hi.astype(F32)).astype(BF16)
    m = m_ref[...]
    return (jnp.dot(hi, m, preferred_element_type=F32) + jnp.dot(lo, m, preferred_element_type=F32))


def _norm_rope(x, gain, m_ref, cos, sins):
    width = x.shape[1]
    ms = _split_dot(x * x, m_ref)
    xn = x * lax.rsqrt(ms + EPS) * gain
    lane = lax.broadcasted_iota(jnp.int32, xn.shape, 1)
    first = (lane % ROPE_AXIS) < (ROPE_AXIS // 2)
    half = ROPE_AXIS // 2
    partner = jnp.where(first, pltpu.roll(xn, width - half, axis=1), pltpu.roll(xn, half, axis=1))
    return xn * cos + partner * sins


def _attn_prep_kernel(q_ref, kv_ref, gq_ref, gk_ref, mq_ref, mk_ref, cos_ref, sin_ref,
                      qo_ref, kt_ref, vo_ref):
    cos = cos_ref[...]
    sins = sin_ref[...]
    cos_q = jnp.concatenate([cos] * (GROUP_W // 128), axis=1)
    sin_q = jnp.concatenate([sins] * (GROUP_W // 128), axis=1)
    q = _norm_rope(q_ref[...], gq_ref[...], mq_ref, cos_q, sin_q)
    qo_ref[...] = (q * (ATT_HEAD_DIM ** -0.5)).astype(BF16)
    kv = kv_ref[...]
    k = _norm_rope(kv[:, :128], gk_ref[...], mk_ref, cos, sins)
    kt_ref[...] = k.T.astype(BF16)
    vo_ref[...] = kv[:, 128:].astype(BF16)


def _rope_tables():
    rows = SEQ // GRID_W
    row = jnp.repeat(jnp.arange(rows, dtype=F32), GRID_W)
    col = jnp.tile(jnp.arange(GRID_W, dtype=F32), rows)
    inv = ROPE_THETA ** (-jnp.arange(0, ROPE_AXIS, 2, dtype=F32) / ROPE_AXIS)
    ar = row[:, None] * inv
    ac = col[:, None] * inv
    ang = jnp.concatenate([ar, ar, ac, ac], axis=1)
    sign = jnp.concatenate([-jnp.ones((ROPE_AXIS // 2,), F32), jnp.ones((ROPE_AXIS // 2,), F32)] * 2)
    cos = jnp.tile(jnp.cos(ang), (1, 2))
    sins = jnp.tile(jnp.sin(ang) * sign, (1, 2))
    return cos, sins


def _head_mean_matrix(width):
    idx = np.arange(width) // ATT_HEAD_DIM
    return jnp.asarray((idx[:, None] == idx[None, :]).astype(np.float32) / ATT_HEAD_DIM, dtype=BF16)


def _attn_kernel(q_ref, kt_ref, v_ref, o_ref):
    outs = []
    v = v_ref[...]
    for h in range(ATT_HEADS):
        g = h // ATT_GROUP
        q = q_ref[:, h * ATT_HEAD_DIM:(h + 1) * ATT_HEAD_DIM]
        kt = kt_ref[g * ATT_HEAD_DIM:(g + 1) * ATT_HEAD_DIM, :]
        s = jnp.dot(q, kt, preferred_element_type=F32)
        m = jnp.max(s, axis=-1, keepdims=True)
        p = jnp.exp(s - m)
        l = jnp.sum(p, axis=-1, keepdims=True)
        pv = jnp.dot(p.astype(BF16), v, preferred_element_type=F32)
        outs.append(pv[:, g * ATT_HEAD_DIM:(g + 1) * ATT_HEAD_DIM] / l)
    o_ref[...] = jnp.concatenate(outs, axis=1).astype(o_ref.dtype)


def _mixer_attention(proj3, q_g, k_g):
    bsz = proj3.shape[0]
    ts = 512
    cos, sins = _rope_tables()
    gq = jnp.tile(q_g, ATT_HEADS).reshape(1, GROUP_W)
    gk = jnp.tile(k_g, ATT_KV_HEADS).reshape(1, 128)
    const = lambda shape: pl.BlockSpec(shape, lambda b, t: (0, 0))
    qp, kt, vp = pl.pallas_call(
        _attn_prep_kernel,
        grid=(bsz, SEQ // ts),
        in_specs=[pl.BlockSpec((None, ts, GROUP_W), lambda b, t: (b, t, COL_BQ // GROUP_W)),
                  pl.BlockSpec((None, ts, 256), lambda b, t: (b, t, COL_BK // 256)),
                  const((1, GROUP_W)), const((1, 128)), const((GROUP_W, GROUP_W)), const((128, 128)),
                  pl.BlockSpec((ts, 128), lambda b, t: (t, 0)),
                  pl.BlockSpec((ts, 128), lambda b, t: (t, 0))],
        out_specs=[pl.BlockSpec((None, ts, GROUP_W), lambda b, t: (b, t, 0)),
                   pl.BlockSpec((None, 128, ts), lambda b, t: (b, 0, t)),
                   pl.BlockSpec((None, ts, 128), lambda b, t: (b, t, 0))],
        out_shape=[jax.ShapeDtypeStruct((bsz, SEQ, GROUP_W), BF16),
                   jax.ShapeDtypeStruct((bsz, 128, SEQ), BF16),
                   jax.ShapeDtypeStruct((bsz, SEQ, 128), BF16)],
        compiler_params=_cparams(("parallel", "parallel")),
        name="attn_prep",
    )(proj3, proj3, gq, gk, _head_mean_matrix(GROUP_W), _head_mean_matrix(128), cos, sins)

    tq = 256
    return pl.pallas_call(
        _attn_kernel,
        grid=(bsz, SEQ // tq),
        in_specs=[pl.BlockSpec((None, tq, GROUP_W), lambda b, t: (b, t, 0)),
                  pl.BlockSpec((None, 128, SEQ), lambda b, t: (b, 0, 0)),
                  pl.BlockSpec((None, SEQ, 128), lambda b, t: (b, 0, 0))],
        out_specs=pl.BlockSpec((None, tq, GROUP_W), lambda b, t: (b, t, 0)),
        out_shape=jax.ShapeDtypeStruct((bsz, SEQ, GROUP_W), BF16),
        compiler_params=_cparams(("parallel", "parallel")),
        name="attention",
    )(qp, kt, vp)


def _hy_features():
    L = SEQ
    pos = jnp.arange(L, dtype=F32)
    t = pos / max(L - 1, 1)
    bands = jnp.linspace(1e-4, HY_BANDS - 1, HY_BANDS, dtype=F32)
    ang = (2.0 * math.pi * pos / L)[:, None] * bands
    feat = jnp.concatenate([t[:, None], jnp.cos(ang), -jnp.sin(ang)], axis=-1)
    feat = jnp.pad(feat, ((0, 0), (0, HY_EMB_PAD - HY_EMB)))
    rev_idx = np.concatenate([[0], np.arange(L - 1, 0, -1)])
    return feat, feat[rev_idx]


def _hy_mlp(feat, w1_ref, b1_ref, w2_ref, b2_ref, w3_ref, sf_ref, dec_ref):
    sf = sf_ref[...]
    h = jnp.sin(sf * (jnp.dot(feat, w1_ref[...], precision=HIGHEST, preferred_element_type=F32) + b1_ref[...]))
    h = jnp.sin(sf * (jnp.dot(h, w2_ref[...], precision=HIGHEST, preferred_element_type=F32) + b2_ref[...]))
    h = jnp.dot(h, w3_ref[...], precision=HIGHEST, preferred_element_type=F32)
    return h * jnp.exp(-feat[:, 0:1] * jnp.abs(dec_ref[...]))


def _hy_filter_kernel(ff_ref, fr_ref, w1_ref, b1_ref, w2_ref, b2_ref, w3f_ref, w3b_ref, sf_ref,
                      decf_ref, decb_ref, of_ref, ob_ref, ssq_scr):
    phase = pl.program_id(1)
    i = pl.program_id(2)
    hf = _hy_mlp(ff_ref[...], w1_ref, b1_ref, w2_ref, b2_ref, w3f_ref, sf_ref, decf_ref)
    hb = _hy_mlp(fr_ref[...], w1_ref, b1_ref, w2_ref, b2_ref, w3b_ref, sf_ref, decb_ref)

    @pl.when(jnp.logical_and(phase == 0, i == 0))
    def _():
        ssq_scr[...] = jnp.zeros_like(ssq_scr)

    @pl.when(phase == 0)
    def _():
        ssq_scr[0:1, :] += jnp.sum(hf * hf + hb * hb, axis=0, keepdims=True)

    @pl.when(phase == 1)
    def _():
        r = lax.rsqrt(ssq_scr[0:1, :] + EPS)
        of_ref[...] = hf * r
        row = lax.broadcasted_iota(jnp.int32, hb.shape, 0)
        ob_ref[...] = jnp.where(jnp.logical_and(i == 0, row == 0), 0.0, hb * r)


def _hy_filters(hy_w1, hy_b1, hy_w2, hy_b2, hy_w3, hy_sin_freq, hy_decay):
    feat, feat_rev = _hy_features()
    tr = 512
    cw = HY_ORDER * HY_W
    w1 = jnp.pad(hy_w1, ((0, 0), (0, HY_EMB_PAD - HY_EMB), (0, 0)))
    w3 = hy_w3.reshape(DEPTH, HY_FFN, HY_ORDER, 2, HY_W)
    dec = hy_decay.reshape(DEPTH, HY_ORDER, 2, HY_W)
    w3f = w3[:, :, :, 0].reshape(DEPTH, HY_FFN, cw)
    w3b = w3[:, :, :, 1].reshape(DEPTH, HY_FFN, cw)
    decf = dec[:, :, 0].reshape(DEPTH, 1, cw)
    decb = dec[:, :, 1].reshape(DEPTH, 1, cw)
    row_spec = pl.BlockSpec((tr, HY_EMB_PAD), lambda l, p, i: (i, 0))
    per_layer = lambda a, b: pl.BlockSpec((None, a, b), lambda l, p, i: (l, 0, 0))
    out_spec = pl.BlockSpec((None, tr, cw), lambda l, p, i: (l, i * p, 0))
    return pl.pallas_call(
        _hy_filter_kernel,
        grid=(DEPTH, 2, SEQ // tr),
        in_specs=[row_spec, row_spec, per_layer(HY_EMB_PAD, HY_FFN), per_layer(1, HY_FFN),
                  per_layer(HY_FFN, HY_FFN), per_layer(1, HY_FFN), per_layer(HY_FFN, cw), per_layer(HY_FFN, cw),
                  per_layer(1, HY_FFN), per_layer(1, cw), per_layer(1, cw)],
        out_specs=[out_spec, out_spec],
        out_shape=[jax.ShapeDtypeStruct((DEPTH, SEQ, cw), F32)] * 2,
        scratch_shapes=[pltpu.VMEM((8, cw), F32)],
        compiler_params=_cparams(("arbitrary", "arbitrary", "arbitrary")),
        name="hyena_filter",
    )(feat, feat_rev, w1, hy_b1.reshape(DEPTH, 1, HY_FFN), hy_w2, hy_b2.reshape(DEPTH, 1, HY_FFN),
      w3f, w3b, hy_sin_freq.reshape(DEPTH, 1, HY_FFN), decf, decb)


def _dft_constants():
    n1 = np.arange(FFT_N1)
    n2 = np.arange(FFT_N2)
    f1 = np.exp(-2j * np.pi * np.outer(n1, n1) / FFT_N1)
    stack = lambda m: np.concatenate([m.real, m.imag], axis=0)
    half = FFT_N1 // 2
    sig_l = stack(f1[:, :half])
    sig_r = np.concatenate([-f1[:, :half].imag, f1[:, :half].real], axis=0)
    fil_r = stack(f1[:, half:])
    f2 = np.exp(-2j * np.pi * np.outer(n2, n2) / FFT_N2)
    tw = np.exp(-2j * np.pi * np.outer(n1, n2) / FFT_N)
    fwd = f2[None, :, :] * tw[:, None, :]
    inv = np.conj(np.transpose(fwd, (0, 2, 1))) / FFT_N
    block = lambda m: np.concatenate([np.concatenate([m.real, -m.imag], axis=2),
                                      np.concatenate([m.imag, m.real], axis=2)], axis=1)
    g1 = np.conj(f1[:half, :])
    out_l = stack(g1)
    out_r = np.concatenate([-g1.imag, g1.real], axis=0)
    as32 = lambda a: jnp.asarray(a.astype(np.float32))
    return dict(sig_l=as32(sig_l), sig_r=as32(sig_r), fil_r=as32(fil_r), fwd=as32(block(fwd)),
                inv=as32(block(inv)), out_l=as32(out_l), out_r=as32(out_r))


def _dft_in_kernel(ml_ref, mr_ref, a_ref, b_ref, o_ref):
    o_ref[...] = _bdot(ml_ref[...], a_ref[...]) + _bdot(mr_ref[...], b_ref[...])


def _dft_in(ml, mr, a, a_map, b, b_map, groups):
    cols = a.shape[-1]
    tc = 8192
    half = FFT_N1 // 2
    mspec = pl.BlockSpec((2 * FFT_N1, half), lambda g, j: (0, 0))
    return pl.pallas_call(
        _dft_in_kernel,
        grid=(groups, cols // tc),
        in_specs=[mspec, mspec,
                  pl.BlockSpec((None, half, tc), lambda g, j: (a_map(g), 0, j)),
                  pl.BlockSpec((None, half, tc), lambda g, j: (b_map(g), 0, j))],
        out_specs=pl.BlockSpec((None, 2 * FFT_N1, tc), lambda g, j: (g, 0, j)),
        out_shape=jax.ShapeDtypeStruct((groups, 2 * FFT_N1, cols), F32),
        compiler_params=_cparams(("parallel", "parallel")),
        name="dft_in",
    )(ml, mr, a, b)


def _dft_spec_kernel(f_ref, a_ref, o_ref):
    x = jnp.concatenate([a_ref[0], a_ref[1]], axis=0)
    y = _bdot(f_ref[...], x)
    o_ref[0] = y[:FFT_N2]
    o_ref[1] = y[FFT_N2:]


def _dft_spectrum(fwd, a5):
    depth, _, _, _, cw = a5.shape
    blk = pl.BlockSpec((None, 2, None, FFT_N2, cw), lambda l, k: (l, 0, k, 0, 0))
    return pl.pallas_call(
        _dft_spec_kernel,
        grid=(depth, FFT_N1),
        in_specs=[pl.BlockSpec((None, 2 * FFT_N2, 2 * FFT_N2), lambda l, k: (k, 0, 0)), blk],
        out_specs=blk,
        out_shape=jax.ShapeDtypeStruct(a5.shape, F32),
        compiler_params=_cparams(("parallel", "parallel")),
        name="dft_spectrum",
    )(fwd, a5)


def _dft_mid_kernel(f_ref, g_ref, h_ref, a_ref, o_ref):
    hr = h_ref[0]
    hi = h_ref[1]
    for p in range(a_ref.shape[0]):
        x = jnp.concatenate([a_ref[p, 0], a_ref[p, 1]], axis=0)
        y = _bdot(f_ref[...], x)
        yr = y[:FFT_N2]
        yi = y[FFT_N2:]
        z = jnp.concatenate([yr * hr - yi * hi, yr * hi + yi * hr], axis=0)
        w = _bdot(g_ref[...], z)
        o_ref[p, 0] = w[:FFT_N2]
        o_ref[p, 1] = w[FFT_N2:]


def _dft_mid(fwd, inv, h5, layer, order, a5):
    pairs = a5.shape[0]
    blk = pl.BlockSpec((pairs, 2, None, FFT_N2, HY_W), lambda k: (0, 0, k, 0, 0))
    mat = pl.BlockSpec((None, 2 * FFT_N2, 2 * FFT_N2), lambda k: (k, 0, 0))
    return pl.pallas_call(
        _dft_mid_kernel,
        grid=(FFT_N1,),
        in_specs=[mat, mat,
                  pl.BlockSpec((None, 2, None, FFT_N2, HY_W), lambda k: (layer, 0, k, 0, order)),
                  blk],
        out_specs=blk,
        out_shape=jax.ShapeDtypeStruct(a5.shape, F32),
        compiler_params=_cparams(("parallel",)),
        name="dft_mid",
    )(fwd, inv, h5, a5)


def _dft_out_kernel(ml_ref, mr_ref, br_ref, bi_ref, u_ref, gate_ref, skip_ref, o_ref):
    y = _bdot(ml_ref[...], br_ref[...]) + _bdot(mr_ref[...], bi_ref[...])
    half = FFT_N1 // 2
    skip = skip_ref[...]
    for r in range(2):
        conv = y[r * half:(r + 1) * half]
        o_ref[r] = (gate_ref[r] * (conv + u_ref[r] * skip)).astype(o_ref.dtype)


def _dft_out(ml, mr, b4, u3, gate3, skip_row, out_dtype):
    pairs = b4.shape[0]
    cols = b4.shape[-1]
    tc = 8192
    half = FFT_N1 // 2
    mspec = pl.BlockSpec((FFT_N1, FFT_N1), lambda p, j: (0, 0))
    sig = pl.BlockSpec((2, half, tc), lambda p, j: (p, 0, j))
    return pl.pallas_call(
        _dft_out_kernel,
        grid=(pairs, cols // tc),
        in_specs=[mspec, mspec,
                  pl.BlockSpec((None, None, FFT_N1, tc), lambda p, j: (p, 0, 0, j)),
                  pl.BlockSpec((None, None, FFT_N1, tc), lambda p, j: (p, 1, 0, j)),
                  sig, sig, pl.BlockSpec((1, tc), lambda p, j: (0, 0))],
        out_specs=sig,
        out_shape=jax.ShapeDtypeStruct(u3.shape, out_dtype),
        compiler_params=_cparams(("parallel", "parallel")),
        name="dft_out",
    )(ml, mr, b4, b4, u3, gate3, skip_row)


def _hy_dwconv_kernel(x_ref, p_ref, n_ref, w_ref, b_ref, v_ref, x1_ref, x2_ref, *, n_tiles):
    z = _dwconv_tile(x_ref, p_ref, n_ref, w_ref, b_ref, pl.program_id(1), n_tiles, 1)
    v_ref[...] = z[:, :HY_W]
    x1_ref[...] = z[:, HY_W:2 * HY_W]
    x2_ref[...] = z[:, 2 * HY_W:]


def _mixer_hyena(proj3, conv_w, conv_b, skip, h5, layer, consts):
    bsz = proj3.shape[0]
    ts = 512
    n_tiles = SEQ // ts
    cu = 3 * HY_W
    small = lambda shape: pl.BlockSpec(shape, lambda b, t: (0, 0))
    out_spec = pl.BlockSpec((None, ts, HY_W), lambda b, t: (b, t, 0))
    v, x1, x2 = pl.pallas_call(
        functools.partial(_hy_dwconv_kernel, n_tiles=n_tiles),
        grid=(bsz, n_tiles),
        in_specs=_halo_specs(ts, cu, COL_CU // cu, lambda t: t, n_tiles) + [small((3, cu)), small((1, cu))],
        out_specs=[out_spec] * 3,
        out_shape=[jax.ShapeDtypeStruct((bsz, SEQ, HY_W), F32)] * 3,
        compiler_params=_cparams(("parallel", "parallel")),
        name="hyena_dwconv",
    )(proj3, proj3, proj3, conv_w, conv_b.reshape(1, cu))

    half = FFT_N1 // 2
    cols = FFT_N2 * HY_W
    view = lambda a: a.reshape(bsz, half, cols)
    pairs = bsz // 2

    def long_conv_gated(u, gate, order, out_dtype):
        a = _dft_in(consts['sig_l'], consts['sig_r'], view(u), lambda g: 2 * g, view(u), lambda g: 2 * g + 1, pairs)
        a5 = a.reshape(pairs, 2, FFT_N1, FFT_N2, HY_W)
        b5 = _dft_mid(consts['fwd'], consts['inv'], h5, layer, order, a5)
        skip_row = jnp.tile(skip[order], 8192 // HY_W).reshape(1, 8192)
        out = _dft_out(consts['out_l'], consts['out_r'], b5.reshape(pairs, 2, FFT_N1, cols),
                       view(u), view(gate), skip_row, out_dtype)
        return out.reshape(bsz, SEQ, HY_W)

    z1 = long_conv_gated(v, x1, 0, F32)
    return long_conv_gated(z1, x2, 1, BF16)


def _mlstm_chunk(q_ref, k_ref, v_ref, gc_ref, gr_ref, c_scr, m_scr, reverse, i_off, f_off, first):
    ch = ML_CHUNK
    ri = lax.broadcasted_iota(jnp.int32, (ch, ch), 0)
    ci = lax.broadcasted_iota(jnp.int32, (ch, ch), 1)
    tri = (ci >= ri) if reverse else (ci <= ri)
    tri_f = tri.astype(F32)
    lf_col = _log_sigmoid(gc_ref[...])
    lf_row = _log_sigmoid(gr_ref[...])
    b_col = jnp.dot(tri_f, lf_col, precision=HIGHEST, preferred_element_type=F32)
    b_row = lax.dot_general(lf_row, tri_f, (((1,), (1,)), ((), ())), precision=HIGHEST,
                            preferred_element_type=F32)
    gc = gc_ref[...]
    gr = gr_ref[...]
    last = 0 if reverse else ch - 1
    lane = lax.broadcasted_iota(jnp.int32, (ch, ML_HEAD_DIM), 1)
    ones_col = jnp.where(lane == 0, 1.0, 0.0).astype(BF16)

    @pl.when(first)
    def _():
        c_scr[...] = jnp.zeros_like(c_scr)
        m_scr[...] = jnp.zeros_like(m_scr)

    outs = []
    for h in range(ML_HEADS):
        sl = slice(h * ML_HEAD_DIM, (h + 1) * ML_HEAD_DIM)
        q = (q_ref[:, sl] * (ML_HEAD_DIM ** -0.5)).astype(BF16)
        k = k_ref[:, sl]
        v_aug = jnp.concatenate([v_ref[:, sl].astype(BF16), ones_col], axis=1)
        bc = b_col[:, f_off + h:f_off + h + 1]
        lic = gc[:, i_off + h:i_off + h + 1]
        br = b_row[f_off + h:f_off + h + 1, :]
        lir = gr[i_off + h:i_off + h + 1, :]
        b_tot = bc[last:last + 1, :]
        d = jnp.where(tri, bc - br + lir, -jnp.inf)
        w_end = b_tot - bc + lic
        m_loc = jnp.max(w_end, axis=0, keepdims=True)
        e_end = jnp.exp(w_end - m_loc)
        m_prev = m_scr[h:h + 1, 0:1]
        c_prev = c_scr[h]
        m_inter = bc + m_prev
        m_t = jnp.maximum(m_inter, jnp.max(d, axis=-1, keepdims=True))
        e_inter = jnp.exp(m_inter - m_t)
        qk = lax.dot_general(q, k.astype(BF16), (((1,), (1,)), ((), ())), preferred_element_type=F32)
        s = qk * jnp.exp(d - m_t)
        nd = _bdot(s, v_aug) + e_inter * _bdot(q, c_prev)
        num = nd[:, :ML_HEAD_DIM]
        den = nd[:, ML_HEAD_DIM:ML_HEAD_DIM + 1]
        outs.append(num / jnp.maximum(jnp.abs(den), jnp.exp(-m_t)))
        m_new = jnp.maximum(b_tot + m_prev, m_loc)
        decay = jnp.exp(b_tot + m_prev - m_new)
        gain = jnp.exp(m_loc - m_new)
        dc = lax.dot_general((k * e_end).astype(BF16), v_aug, (((0,), (0,)), ((), ())),
                             preferred_element_type=F32)
        c_scr[h] = decay * c_prev + gain * dc
        m_scr[h:h + 1, :] = jnp.broadcast_to(m_new, (1, 128))
    return outs


def _mlstm_fwd_kernel(q_ref, k_ref, v_ref, gc_ref, gr_ref, h_ref, c_scr, m_scr):
    outs = _mlstm_chunk(q_ref, k_ref, v_ref, gc_ref, gr_ref, c_scr, m_scr, False, 0, ML_HEADS,
                        pl.program_id(1) == 0)
    h_ref[...] = jnp.concatenate(outs, axis=1)


def _mlstm_bwd_kernel(q_ref, k_ref, v_ref, gc_ref, gr_ref, hf_ref, o_ref, g_ref, y_ref, c_scr, m_scr):
    outs = _mlstm_chunk(q_ref, k_ref, v_ref, gc_ref, gr_ref, c_scr, m_scr, True, 2 * ML_HEADS, 3 * ML_HEADS,
                        pl.program_id(1) == 0)
    normed = []
    for h in range(ML_HEADS):
        sl = slice(h * ML_HEAD_DIM, (h + 1) * ML_HEAD_DIM)
        x = hf_ref[:, sl] + outs[h]
        ms = jnp.mean(x * x, axis=-1, keepdims=True)
        normed.append(x * lax.rsqrt(ms + EPS) * g_ref[:, sl])
    y_ref[...] = (_sigmoid(o_ref[...]) * jnp.concatenate(normed, axis=1)).astype(y_ref.dtype)


def _mixer_mlstm(proj3, norm_g):
    bsz = proj3.shape[0]
    nc = SEQ // ML_CHUNK
    gates_t = jnp.transpose(proj3[:, :, COL_GATES:COL_GATES + 4 * ML_HEADS], (0, 2, 1))
    scratch = [pltpu.VMEM((ML_HEADS, ML_HEAD_DIM, 2 * ML_HEAD_DIM), F32), pltpu.VMEM((8, 128), F32)]

    def specs(chunk_of):
        col = lambda c: pl.BlockSpec((None, ML_CHUNK, GROUP_W), lambda b, j: (b, chunk_of(j), c // GROUP_W))
        return [col(COL_DQ), col(COL_DK), col(COL_DV),
                pl.BlockSpec((None, ML_CHUNK, 128), lambda b, j: (b, chunk_of(j), COL_GATES // 128)),
                pl.BlockSpec((None, 4 * ML_HEADS, ML_CHUNK), lambda b, j: (b, 0, chunk_of(j)))]

    fwd_of = lambda j: j
    hf = pl.pallas_call(
        _mlstm_fwd_kernel,
        grid=(bsz, nc),
        in_specs=specs(fwd_of),
        out_specs=pl.BlockSpec((None, ML_CHUNK, GROUP_W), lambda b, j: (b, j, 0)),
        out_shape=jax.ShapeDtypeStruct((bsz, SEQ, GROUP_W), F32),
        scratch_shapes=scratch,
        compiler_params=_cparams(("parallel", "arbitrary")),
        name="mlstm_fwd",
    )(proj3, proj3, proj3, proj3, gates_t)

    bwd_of = lambda j: nc - 1 - j
    tile = lambda col: pl.BlockSpec((None, ML_CHUNK, GROUP_W), lambda b, j: (b, nc - 1 - j, col))
    return pl.pallas_call(
        _mlstm_bwd_kernel,
        grid=(bsz, nc),
        in_specs=specs(bwd_of) + [tile(0), tile(COL_DO // GROUP_W), pl.BlockSpec((1, GROUP_W), lambda b, j: (0, 0))],
        out_specs=tile(0),
        out_shape=jax.ShapeDtypeStruct((bsz, SEQ, GROUP_W), BF16),
        scratch_shapes=scratch,
        compiler_params=_cparams(("parallel", "arbitrary")),
        name="mlstm_bwd",
    )(proj3, proj3, proj3, proj3, gates_t, hf, proj3, norm_g.reshape(1, GROUP_W))


def _outproj_kernel(ya_ref, yb_ref, yc_ref, yd_ref, w_ref, x_ref, m_ref, o_ref):
    acc = jnp.dot(ya_ref[...], w_ref[0:GROUP_W, :], preferred_element_type=F32)
    acc += jnp.dot(yb_ref[...], w_ref[GROUP_W:2 * GROUP_W, :], preferred_element_type=F32)
    acc += jnp.dot(yc_ref[...], w_ref[2 * GROUP_W:3 * GROUP_W, :], preferred_element_type=F32)
    acc += jnp.dot(yd_ref[...], w_ref[3 * GROUP_W:, :], preferred_element_type=F32)
    o_ref[...] = x_ref[...] + m_ref[2:3, :] * acc


def _outproj(ys, w, x2, mod_l):
    n = x2.shape[0]
    tm = 512
    per_b = SEQ // tm
    yspec = pl.BlockSpec((tm, GROUP_W), lambda i: (i, 0))
    return pl.pallas_call(
        _outproj_kernel,
        grid=(n // tm,),
        in_specs=[yspec] * 4 + [pl.BlockSpec((D_MODEL, D_MODEL), lambda i: (0, 0)),
                                pl.BlockSpec((tm, D_MODEL), lambda i: (i, 0)),
                                pl.BlockSpec((None, 6, D_MODEL), lambda i: (i // per_b, 0, 0))],
        out_specs=pl.BlockSpec((tm, D_MODEL), lambda i: (i, 0)),
        out_shape=jax.ShapeDtypeStruct((n, D_MODEL), F32),
        compiler_params=_cparams(("parallel",)),
        name="out_proj",
    )(*ys, w, x2, mod_l)


def _ffn_kernel(x_ref, m_ref, g_ref, w1_ref, w3_ref, w2_ref, fg_ref, o_ref, h_scr, *, final):
    j = pl.program_id(1)

    @pl.when(j == 0)
    def _():
        h_scr[...] = _rms_mod(x_ref[...], g_ref[...], m_ref[4:5, :], m_ref[3:4, :]).astype(BF16)
        o_ref[...] = jnp.zeros_like(o_ref)

    h = h_scr[...]
    a = jnp.dot(h, w1_ref[...], preferred_element_type=F32)
    b = jnp.dot(h, w3_ref[...], preferred_element_type=F32)
    act = (a * _sigmoid(a)) * b
    o_ref[...] += jnp.dot(act.astype(BF16), w2_ref[...], preferred_element_type=F32)

    @pl.when(j == pl.num_programs(1) - 1)
    def _():
        y = x_ref[...] + m_ref[5:6, :] * o_ref[...]
        if final:
            ms = jnp.mean(y * y, axis=-1, keepdims=True)
            y = y * lax.rsqrt(ms + EPS) * fg_ref[...]
        o_ref[...] = y


def _ffn(x2, mod_l, g, w1, w3, w2, final_g, final):
    n = x2.shape[0]
    tm, tf = 512, 512
    per_b = SEQ // tm
    row = lambda: pl.BlockSpec((1, D_MODEL), lambda i, j: (0, 0))
    return pl.pallas_call(
        functools.partial(_ffn_kernel, final=final),
        grid=(n // tm, D_FF // tf),
        in_specs=[pl.BlockSpec((tm, D_MODEL), lambda i, j: (i, 0)),
                  pl.BlockSpec((None, 6, D_MODEL), lambda i, j: (i // per_b, 0, 0)),
                  row(),
                  pl.BlockSpec((D_MODEL, tf), lambda i, j: (0, j)),
                  pl.BlockSpec((D_MODEL, tf), lambda i, j: (0, j)),
                  pl.BlockSpec((tf, D_MODEL), lambda i, j: (j, 0)),
                  row()],
        out_specs=pl.BlockSpec((tm, D_MODEL), lambda i, j: (i, 0)),
        out_shape=jax.ShapeDtypeStruct((n, D_MODEL), F32),
        scratch_shapes=[pltpu.VMEM((tm, D_MODEL), BF16)],
        compiler_params=_cparams(("parallel", "arbitrary")),
        name="ffn",
    )(x2, mod_l, g.reshape(1, D_MODEL), w1, w3, w2, final_g.reshape(1, D_MODEL))


def _reorder_in(w, b):
    offs = np.concatenate([[0], np.cumsum(IN_SIZES)])
    order = (0, 1, 2, 5, 6, 7, 8, 9, 3, 4, 10)
    pad = D_IN_PAD - D_IN
    wr = jnp.concatenate([w[:, offs[i]:offs[i + 1]] for i in order] + [jnp.zeros((D_MODEL, pad), w.dtype)], axis=1)
    br = jnp.concatenate([b[offs[i]:offs[i + 1]] for i in order] + [jnp.zeros((pad,), b.dtype)])
    return wr.astype(BF16), br.reshape(1, D_IN_PAD)


def kernel(x, c, w_in, b_in, w_out, norm_mix_g, norm_ffn_g, ada_w, ada_b, lru_conv_w, lru_conv_b, lru_wa, lru_ba, lru_wx, lru_bx, lru_lambda, att_q_norm_g, att_k_norm_g, hy_conv_w, hy_conv_b, hy_w1, hy_b1, hy_w2, hy_b2, hy_w3, hy_sin_freq, hy_decay, hy_skip, ml_norm_g, ffn_w1, ffn_w3, ffn_w2, final_g):
    bsz = x.shape[0]
    assert x.shape == (bsz, SEQ, D_MODEL) and bsz % 2 == 0
    n = bsz * SEQ
    mod = _ada_all(c, ada_w, ada_b)

    consts = _dft_constants()
    consts = {k: v.astype(BF16) for k, v in consts.items()}
    hf, hb = _hy_filters(hy_w1, hy_b1, hy_w2, hy_b2, hy_w3, hy_sin_freq, hy_decay)
    half = FFT_N1 // 2
    cw = HY_ORDER * HY_W
    fview = lambda a: a.reshape(DEPTH, half, FFT_N2 * cw)
    ha = _dft_in(consts['sig_l'], consts['fil_r'], fview(hf), lambda g: g, fview(hb), lambda g: g, DEPTH)
    h5 = _dft_spectrum(consts['fwd'], ha.reshape(DEPTH, 2, FFT_N1, FFT_N2, cw))

    x2 = x.reshape(n, D_MODEL)
    for l in range(DEPTH):
        w_in_l, b_in_l = _reorder_in(w_in[l], b_in[l])
        proj = _inproj(x2, mod[l], norm_mix_g[l], w_in_l, b_in_l)
        proj3 = proj.reshape(bsz, SEQ, D_IN_PAD)
        y_a = _mixer_rglru(proj3, lru_conv_w[l], lru_conv_b[l], lru_wa[l], lru_ba[l], lru_wx[l], lru_bx[l],
                           lru_lambda[l])
        y_b = _mixer_attention(proj3, att_q_norm_g[l], att_k_norm_g[l])
        y_c = _mixer_hyena(proj3, hy_conv_w[l], hy_conv_b[l], hy_skip[l], h5, l, consts)
        y_d = _mixer_mlstm(proj3, ml_norm_g[l])
        ys = [y.reshape(n, GROUP_W) for y in (y_a, y_b, y_c, y_d)]
        x2 = _outproj(ys, w_out[l].astype(BF16), x2, mod[l])
        x2 = _ffn(x2, mod[l], norm_ffn_g[l], ffn_w1[l].astype(BF16), ffn_w3[l].astype(BF16),
                  ffn_w2[l].astype(BF16), final_g, final=(l == DEPTH - 1))
    return x2.reshape(bsz, SEQ, D_MODEL)
```

```python
import functools
import math

import numpy as np
import jax
import jax.numpy as jnp
from jax import lax
from jax.experimental import pallas as pl
from jax.experimental.pallas import tpu as pltpu

F32 = jnp.float32
BF16 = jnp.bfloat16
HIGHEST = lax.Precision.HIGHEST

D_MODEL = 2048
SEQ = 4096
DEPTH = 2
GROUP_W = 512
LRU_BLOCKS = 8
LRU_C = 8.0
ATT_HEADS = 8
ATT_KV_HEADS = 2
ATT_GROUP = ATT_HEADS // ATT_KV_HEADS
ATT_HEAD_DIM = 64
ROPE_AXIS = ATT_HEAD_DIM // 2
ROPE_THETA = 10000.0
GRID_W = 64
HY_W = GROUP_W
HY_ORDER = 2
HY_BANDS = 8
HY_EMB = 2 * HY_BANDS + 1
HY_EMB_PAD = 32
HY_FFN = 64
ML_HEADS = 4
ML_HEAD_DIM = 128
ML_CHUNK = 128
D_FF = 5632
EPS = 1e-6
IN_SIZES = (512, 512, 512, 128, 128, 1536, 512, 512, 512, 512, 16)
D_IN = sum(IN_SIZES)
D_IN_PAD = 5632

COL_AX, COL_AG, COL_BQ, COL_BK, COL_BV, COL_CU = 0, 512, 1024, 1536, 1664, 1792
COL_DQ, COL_DK, COL_DV, COL_DO, COL_GATES = 3328, 3840, 4352, 4864, 5376
LANES = 128

FFT_N = 2 * SEQ
FFT_N1 = 64
FFT_N2 = 128
PITCH = 136

VMEM_LIMIT = 56 * 1024 * 1024


def _cparams(sem, vmem=VMEM_LIMIT):
    return pltpu.CompilerParams(dimension_semantics=sem, vmem_limit_bytes=vmem)


def _bdot(a, b):
    return jnp.dot(a.astype(BF16), b.astype(BF16), preferred_element_type=F32)


def _sigmoid(x):
    return jax.nn.sigmoid(x)


def _log_sigmoid(x):
    return jnp.minimum(x, 0.0) - jnp.log1p(jnp.exp(-jnp.abs(x)))


def _softplus(x):
    return jnp.maximum(x, 0.0) + jnp.log1p(jnp.exp(-jnp.abs(x)))


def _ada_kernel(c_ref, w_ref, b_ref, o_ref):
    c = c_ref[...]
    o_ref[...] = _bdot(c * _sigmoid(c), w_ref[...]) + b_ref[...]


def _ada_all(c, ada_w, ada_b):
    bsz = c.shape[0]
    rows = 8
    cp = jnp.zeros((rows, D_MODEL), F32).at[:bsz].set(c)
    tn = 1024
    out = pl.pallas_call(
        _ada_kernel,
        grid=(DEPTH, 6 * D_MODEL // tn),
        in_specs=[pl.BlockSpec((rows, D_MODEL), lambda l, j: (0, 0)),
                  pl.BlockSpec((None, D_MODEL, tn), lambda l, j: (l, 0, j)),
                  pl.BlockSpec((None, 1, tn), lambda l, j: (l, 0, j))],
        out_specs=pl.BlockSpec((None, rows, tn), lambda l, j: (l, 0, j)),
        out_shape=jax.ShapeDtypeStruct((DEPTH, rows, 6 * D_MODEL), F32),
        compiler_params=_cparams(("parallel", "parallel")),
        name="ada_mod",
    )(cp, ada_w, ada_b.reshape(DEPTH, 1, 6 * D_MODEL))
    return out[:, :bsz].reshape(DEPTH, bsz, 6, D_MODEL)


def _rms_mod(x, g, scale, shift):
    ms = jnp.mean(x * x, axis=-1, keepdims=True)
    return (x * lax.rsqrt(ms + EPS) * g) * (1.0 + scale) + shift


def _inproj_kernel(x_ref, m_ref, g_ref, w_ref, b_ref, o_ref, h_scr):
    @pl.when(pl.program_id(1) == 0)
    def _():
        h_scr[...] = _rms_mod(x_ref[...], g_ref[...], m_ref[1:2, :], m_ref[0:1, :]).astype(BF16)

    o_ref[...] = jnp.dot(h_scr[...], w_ref[...], preferred_element_type=F32) + b_ref[...]


def _inproj(x2, mod_l, g, w_all, b_all, layer):
    n = x2.shape[0]
    tm, tn = 1024, 1408
    per_b = SEQ // tm
    return pl.pallas_call(
        _inproj_kernel,
        grid=(n // tm, D_IN_PAD // tn),
        in_specs=[pl.BlockSpec((tm, D_MODEL), lambda i, j: (i, 0)),
                  pl.BlockSpec((None, 6, D_MODEL), lambda i, j: (i // per_b, 0, 0)),
                  pl.BlockSpec((1, D_MODEL), lambda i, j: (0, 0)),
                  pl.BlockSpec((None, D_MODEL, tn), lambda i, j: (layer, 0, j)),
                  pl.BlockSpec((None, 1, tn), lambda i, j: (layer, 0, j))],
        out_specs=pl.BlockSpec((tm, tn), lambda i, j: (i, j)),
        out_shape=jax.ShapeDtypeStruct((n, D_IN_PAD), F32),
        scratch_shapes=[pltpu.VMEM((tm, D_MODEL), BF16)],
        compiler_params=_cparams(("parallel", "arbitrary")),
        name="in_proj",
    )(x2, mod_l, g.reshape(1, D_MODEL), w_all, b_all)


def _shifted(ext, off, rows):
    total = ext.shape[0]
    if off == 0:
        return ext[8:8 + rows]
    return pltpu.roll(ext, (-off) % total, axis=0)[8:8 + rows]


def _dwconv_ext(ext, w_ref, b_ref, left, rows):
    out = b_ref[...]
    for j in range(w_ref.shape[0]):
        out = out + _shifted(ext, j - left, rows) * w_ref[j:j + 1, :]
    return out


def _dwconv_tile(x_ref, p_ref, n_ref, w_ref, b_ref, tile, n_tiles, left):
    prev = jnp.where(tile > 0, p_ref[...], 0.0)
    nxt = jnp.where(tile < n_tiles - 1, n_ref[...], 0.0)
    ext = jnp.concatenate([prev, x_ref[...], nxt], axis=0)
    return _dwconv_ext(ext, w_ref, b_ref, left, x_ref.shape[0])


def _dwconv_rows(ref, lo, rows, w_ref, b_ref, left):
    zeros = jnp.zeros((8, ref.shape[1]), F32)
    prev = ref[lo - 8:lo, :] if lo > 0 else zeros
    nxt = ref[lo + rows:lo + rows + 8, :] if lo + rows < SEQ else zeros
    ext = jnp.concatenate([prev, ref[lo:lo + rows, :], nxt], axis=0)
    return _dwconv_ext(ext, w_ref, b_ref, left, rows)


def _halo_specs(tile_rows, width, col_block, tile_of):
    r8 = tile_rows // 8
    last8 = SEQ // 8 - 1

    def main(b, t):
        return (b, tile_of(t), col_block)

    def prev(b, t):
        return (b, jnp.maximum(tile_of(t) * r8 - 1, 0), col_block)

    def nxt(b, t):
        return (b, jnp.minimum((tile_of(t) + 1) * r8, last8), col_block)

    return [pl.BlockSpec((None, tile_rows, width), main),
            pl.BlockSpec((None, 8, width), prev),
            pl.BlockSpec((None, 8, width), nxt)]


def _lru_gates(xc, wg, bg, lam):
    gates = _bdot(xc, wg) + bg
    r = _sigmoid(gates[:, :GROUP_W])
    i = _sigmoid(gates[:, GROUP_W:])
    log_a = (-LRU_C * _softplus(-lam)) * r
    a = jnp.exp(log_a)
    th = jnp.tanh(log_a)
    u = jnp.sqrt(-2.0 * th / (1.0 - th)) * (i * xc)
    return a, u


def _scan8(a, u, ridx, reverse):
    for k in (1, 2, 4):
        if reverse:
            keep = ridx < 8 - k
            sh = 8 - k
        else:
            keep = ridx >= k
            sh = k
        a_sh = jnp.where(keep, pltpu.roll(a, sh, axis=0), 1.0)
        u_sh = jnp.where(keep, pltpu.roll(u, sh, axis=0), 0.0)
        u = a * u_sh + u
        a = a * a_sh
    return a, u


def _lru_kernel(xf_ref, pf_ref, nf_ref, xb_ref, pb_ref, nb_ref, cw_ref, cb_ref, wg_ref, bg_ref, lam_ref,
                hf_ref, hb_ref, af_scr, uf_scr, ab_scr, ub_scr, c_scr, *, n_tiles):
    t = pl.program_id(1)
    xc = _dwconv_tile(xf_ref, pf_ref, nf_ref, cw_ref, cb_ref, t, n_tiles, 2)
    a, u = _lru_gates(xc, wg_ref[0], bg_ref[0], lam_ref[0])
    af_scr[...] = a
    uf_scr[...] = u
    xc = _dwconv_tile(xb_ref, pb_ref, nb_ref, cw_ref, cb_ref, n_tiles - 1 - t, n_tiles, 2)
    a, u = _lru_gates(xc, wg_ref[1], bg_ref[1], lam_ref[1])
    ab_scr[...] = a
    ub_scr[...] = u

    @pl.when(t == 0)
    def _():
        c_scr[...] = jnp.zeros_like(c_scr)

    n_chunks = af_scr.shape[0] // 8
    ridx = lax.broadcasted_iota(jnp.int32, (8, GROUP_W), 0)

    def body(c, carry):
        cf, cb = carry
        r0 = pl.multiple_of(c * 8, 8)
        a, u = _scan8(af_scr[pl.ds(r0, 8), :], uf_scr[pl.ds(r0, 8), :], ridx, False)
        h = u + a * cf
        hf_ref[pl.ds(r0, 8), :] = h
        cf = h[7:8, :]
        r1 = pl.multiple_of((n_chunks - 1 - c) * 8, 8)
        a, u = _scan8(ab_scr[pl.ds(r1, 8), :], ub_scr[pl.ds(r1, 8), :], ridx, True)
        h = u + a * cb
        hb_ref[pl.ds(r1, 8), :] = h
        return cf, h[0:1, :]

    cf, cb = lax.fori_loop(0, n_chunks, body, (c_scr[0:1, :], c_scr[1:2, :]), unroll=2)
    c_scr[0:1, :] = cf
    c_scr[1:2, :] = cb


def _block_diag(w):
    nb, k, j = w.shape
    eye = jnp.eye(nb, dtype=w.dtype)
    return jnp.einsum('nkj,nm->nkmj', w, eye).reshape(nb * k, nb * j)


def _mixer_rglru(proj3, conv_w, conv_b, wa, ba, wx, bx, lam):
    bsz = proj3.shape[0]
    ts = 512
    n_tiles = SEQ // ts
    wg = jnp.stack([jnp.concatenate([_block_diag(wa[d]), _block_diag(wx[d])], axis=1) for d in range(2)]).astype(BF16)
    bg = jnp.stack([jnp.concatenate([ba[d], bx[d]]).reshape(1, 2 * GROUP_W) for d in range(2)])
    small = lambda *shape: pl.BlockSpec(shape, lambda b, t: (0,) * len(shape))
    fwd_of = lambda t: t
    bwd_of = lambda t: n_tiles - 1 - t
    out = lambda tile_of: pl.BlockSpec((None, ts, GROUP_W), lambda b, t: (b, tile_of(t), 0))
    tile_scr = pltpu.VMEM((ts, GROUP_W), F32)
    return pl.pallas_call(
        functools.partial(_lru_kernel, n_tiles=n_tiles),
        grid=(bsz, n_tiles),
        in_specs=_halo_specs(ts, GROUP_W, COL_AX // GROUP_W, fwd_of) + _halo_specs(ts, GROUP_W, COL_AX // GROUP_W, bwd_of)
        + [small(4, GROUP_W), small(1, GROUP_W), small(2, GROUP_W, 2 * GROUP_W), small(2, 1, 2 * GROUP_W),
           small(2, 1, GROUP_W)],
        out_specs=[out(fwd_of), out(bwd_of)],
        out_shape=[jax.ShapeDtypeStruct((bsz, SEQ, GROUP_W), F32)] * 2,
        scratch_shapes=[tile_scr, tile_scr, tile_scr, tile_scr, pltpu.VMEM((8, GROUP_W), F32)],
        compiler_params=_cparams(("parallel", "arbitrary")),
        name="rglru",
    )(proj3, proj3, proj3, proj3, proj3, proj3, conv_w, conv_b.reshape(1, GROUP_W), wg, bg,
      lam.reshape(2, 1, GROUP_W))


def _split_dot(x, m_ref):
    hi = x.astype(BF16)
    lo = (x - hi.astype(F32)).astype(BF16)
    m = m_ref[...]
    return (jnp.dot(hi, m, preferred_element_type=F32) + jnp.dot(lo, m, preferred_element_type=F32))


def _norm_rope(x, gain, m_ref, cos, sins):
    width = x.shape[1]
    ms = _split_dot(x * x, m_ref)
    xn = x * lax.rsqrt(ms + EPS) * gain
    lane = lax.broadcasted_iota(jnp.int32, xn.shape, 1)
    first = (lane % ROPE_AXIS) < (ROPE_AXIS // 2)
    half = ROPE_AXIS // 2
    partner = jnp.where(first, pltpu.roll(xn, width - half, axis=1), pltpu.roll(xn, half, axis=1))
    return xn * cos + partner * sins


def _attn_prep_kernel(q_ref, kv_ref, gq_ref, gk_ref, mq_ref, mk_ref, cos_ref, sin_ref,
                      qo_ref, kt_ref, vo_ref):
    cos = cos_ref[...]
    sins = sin_ref[...]
    cos_q = jnp.concatenate([cos] * (GROUP_W // LANES), axis=1)
    sin_q = jnp.concatenate([sins] * (GROUP_W // LANES), axis=1)
    q = _norm_rope(q_ref[...], gq_ref[...], mq_ref, cos_q, sin_q)
    qo_ref[...] = (q * (ATT_HEAD_DIM ** -0.5)).astype(BF16)
    kv = kv_ref[...]
    k = _norm_rope(kv[:, :LANES], gk_ref[...], mk_ref, cos, sins)
    kt_ref[...] = k.T.astype(BF16)
    vo_ref[...] = kv[:, LANES:].astype(BF16)


def _rope_tables():
    rows = SEQ // GRID_W
    row = jnp.repeat(jnp.arange(rows, dtype=F32), GRID_W)
    col = jnp.tile(jnp.arange(GRID_W, dtype=F32), rows)
    inv = ROPE_THETA ** (-jnp.arange(0, ROPE_AXIS, 2, dtype=F32) / ROPE_AXIS)
    ar = row[:, None] * inv
    ac = col[:, None] * inv
    ang = jnp.concatenate([ar, ar, ac, ac], axis=1)
    sign = jnp.concatenate([-jnp.ones((ROPE_AXIS // 2,), F32), jnp.ones((ROPE_AXIS // 2,), F32)] * 2)
    cos = jnp.tile(jnp.cos(ang), (1, 2))
    sins = jnp.tile(jnp.sin(ang) * sign, (1, 2))
    return cos, sins


def _head_mean_matrix(width):
    idx = np.arange(width) // ATT_HEAD_DIM
    return jnp.asarray((idx[:, None] == idx[None, :]).astype(np.float32) / ATT_HEAD_DIM, dtype=BF16)


def _attn_kernel(q_ref, kt_ref, v_ref, o_ref):
    outs = []
    v = v_ref[...]
    for h in range(ATT_HEADS):
        g = h // ATT_GROUP
        q = q_ref[:, h * ATT_HEAD_DIM:(h + 1) * ATT_HEAD_DIM]
        kt = kt_ref[g * ATT_HEAD_DIM:(g + 1) * ATT_HEAD_DIM, :]
        s = jnp.dot(q, kt, preferred_element_type=F32)
        m = jnp.max(s, axis=-1, keepdims=True)
        p = jnp.exp(s - m)
        l = jnp.sum(p, axis=-1, keepdims=True)
        pv = jnp.dot(p.astype(BF16), v, preferred_element_type=F32)
        outs.append(pv[:, g * ATT_HEAD_DIM:(g + 1) * ATT_HEAD_DIM] / l)
    o_ref[...] = jnp.concatenate(outs, axis=1).astype(o_ref.dtype)


def _mixer_attention(proj3, q_g, k_g):
    bsz = proj3.shape[0]
    ts = 512
    cos, sins = _rope_tables()
    gq = jnp.tile(q_g, ATT_HEADS).reshape(1, GROUP_W)
    gk = jnp.tile(k_g, ATT_KV_HEADS).reshape(1, LANES)
    const = lambda shape: pl.BlockSpec(shape, lambda b, t: (0, 0))
    qp, kt, vp = pl.pallas_call(
        _attn_prep_kernel,
        grid=(bsz, SEQ // ts),
        in_specs=[pl.BlockSpec((None, ts, GROUP_W), lambda b, t: (b, t, COL_BQ // GROUP_W)),
                  pl.BlockSpec((None, ts, 2 * LANES), lambda b, t: (b, t, COL_BK // (2 * LANES))),
                  const((1, GROUP_W)), const((1, LANES)), const((GROUP_W, GROUP_W)), const((LANES, LANES)),
                  pl.BlockSpec((ts, LANES), lambda b, t: (t, 0)),
                  pl.BlockSpec((ts, LANES), lambda b, t: (t, 0))],
        out_specs=[pl.BlockSpec((None, ts, GROUP_W), lambda b, t: (b, t, 0)),
                   pl.BlockSpec((None, LANES, ts), lambda b, t: (b, 0, t)),
                   pl.BlockSpec((None, ts, LANES), lambda b, t: (b, t, 0))],
        out_shape=[jax.ShapeDtypeStruct((bsz, SEQ, GROUP_W), BF16),
                   jax.ShapeDtypeStruct((bsz, LANES, SEQ), BF16),
                   jax.ShapeDtypeStruct((bsz, SEQ, LANES), BF16)],
        compiler_params=_cparams(("parallel", "parallel")),
        name="attn_prep",
    )(proj3, proj3, gq, gk, _head_mean_matrix(GROUP_W), _head_mean_matrix(LANES), cos, sins)

    tq = 256
    return pl.pallas_call(
        _attn_kernel,
        grid=(bsz, SEQ // tq),
        in_specs=[pl.BlockSpec((None, tq, GROUP_W), lambda b, t: (b, t, 0)),
                  pl.BlockSpec((None, LANES, SEQ), lambda b, t: (b, 0, 0)),
                  pl.BlockSpec((None, SEQ, LANES), lambda b, t: (b, 0, 0))],
        out_specs=pl.BlockSpec((None, tq, GROUP_W), lambda b, t: (b, t, 0)),
        out_shape=jax.ShapeDtypeStruct((bsz, SEQ, GROUP_W), BF16),
        compiler_params=_cparams(("parallel", "parallel")),
        name="attention",
    )(qp, kt, vp)


def _hy_features():
    L = SEQ
    pos = jnp.arange(L, dtype=F32)
    t = pos / max(L - 1, 1)
    bands = jnp.linspace(1e-4, HY_BANDS - 1, HY_BANDS, dtype=F32)
    ang = (2.0 * math.pi * pos / L)[:, None] * bands
    feat = jnp.concatenate([t[:, None], jnp.cos(ang), -jnp.sin(ang)], axis=-1)
    feat = jnp.pad(feat, ((0, 0), (0, HY_EMB_PAD - HY_EMB)))
    rev_idx = np.concatenate([[0], np.arange(L - 1, 0, -1)])
    return feat, feat[rev_idx]


def _hy_mlp(feat, feat_t, w1t_ref, b1_ref, w2t_ref, b2_ref, w3_ref, sf_ref, dec_ref):
    sf = sf_ref[...]
    h = jnp.sin(sf * (jnp.dot(w1t_ref[...], feat_t, precision=HIGHEST, preferred_element_type=F32) + b1_ref[...]))
    h = jnp.sin(sf * (jnp.dot(w2t_ref[...], h, precision=HIGHEST, preferred_element_type=F32) + b2_ref[...]))
    out = jnp.dot(h.T, w3_ref[...], precision=HIGHEST, preferred_element_type=F32)
    return out * jnp.exp(-feat[:, 0:1] * jnp.abs(dec_ref[...]))


def _hy_filter_kernel(ff_ref, fft_ref, fr_ref, frt_ref, w1t_ref, b1_ref, w2t_ref, b2_ref, w3f_ref, w3b_ref, sf_ref,
                      decf_ref, decb_ref, of_ref, ob_ref, ssq_ref):
    i = pl.program_id(1)
    hf = _hy_mlp(ff_ref[...], fft_ref[...], w1t_ref, b1_ref, w2t_ref, b2_ref, w3f_ref, sf_ref, decf_ref)
    hb = _hy_mlp(fr_ref[...], frt_ref[...], w1t_ref, b1_ref, w2t_ref, b2_ref, w3b_ref, sf_ref, decb_ref)

    @pl.when(i == 0)
    def _():
        ssq_ref[...] = jnp.zeros_like(ssq_ref)

    ssq_ref[0:1, :] += jnp.sum(hf * hf + hb * hb, axis=0, keepdims=True)
    of_ref[...] = hf
    row = lax.broadcasted_iota(jnp.int32, hb.shape, 0)
    ob_ref[...] = jnp.where(jnp.logical_and(i == 0, row == 0), 0.0, hb)


def _hy_filters(hy_w1, hy_b1, hy_w2, hy_b2, hy_w3, hy_sin_freq, hy_decay):
    feat, feat_rev = _hy_features()
    tr = 512
    cw = HY_ORDER * HY_W
    w1t = jnp.transpose(jnp.pad(hy_w1, ((0, 0), (0, HY_EMB_PAD - HY_EMB), (0, 0))), (0, 2, 1))
    w2t = jnp.transpose(hy_w2, (0, 2, 1))
    w3 = hy_w3.reshape(DEPTH, HY_FFN, HY_ORDER, 2, HY_W)
    dec = hy_decay.reshape(DEPTH, HY_ORDER, 2, HY_W)
    w3f = w3[:, :, :, 0].reshape(DEPTH, HY_FFN, cw)
    w3b = w3[:, :, :, 1].reshape(DEPTH, HY_FFN, cw)
    decf = dec[:, :, 0].reshape(DEPTH, 1, cw)
    decb = dec[:, :, 1].reshape(DEPTH, 1, cw)
    col = lambda a: a.reshape(DEPTH, HY_FFN, 1)
    row_spec = pl.BlockSpec((tr, HY_EMB_PAD), lambda l, i: (i, 0))
    rowt_spec = pl.BlockSpec((HY_EMB_PAD, tr), lambda l, i: (0, i))
    per_layer = lambda a, b: pl.BlockSpec((None, a, b), lambda l, i: (l, 0, 0))
    out_spec = pl.BlockSpec((None, tr, cw), lambda l, i: (l, i, 0))
    return pl.pallas_call(
        _hy_filter_kernel,
        grid=(DEPTH, SEQ // tr),
        in_specs=[row_spec, rowt_spec, row_spec, rowt_spec, per_layer(HY_FFN, HY_EMB_PAD), per_layer(HY_FFN, 1),
                  per_layer(HY_FFN, HY_FFN), per_layer(HY_FFN, 1), per_layer(HY_FFN, cw), per_layer(HY_FFN, cw),
                  per_layer(HY_FFN, 1), per_layer(1, cw), per_layer(1, cw)],
        out_specs=[out_spec, out_spec, per_layer(8, cw)],
        out_shape=[jax.ShapeDtypeStruct((DEPTH, SEQ, cw), F32)] * 2 + [jax.ShapeDtypeStruct((DEPTH, 8, cw), F32)],
        compiler_params=_cparams(("arbitrary", "arbitrary")),
        name="hyena_filter",
    )(feat, feat.T, feat_rev, feat_rev.T, w1t, col(hy_b1), w2t, col(hy_b2), w3f, w3b, col(hy_sin_freq), decf, decb)


def _dft_constants():
    n1 = np.arange(FFT_N1)
    n2 = np.arange(FFT_N2)
    f1 = np.exp(-2j * np.pi * np.outer(n1, n1) / FFT_N1)
    stack = lambda m: np.concatenate([m.real, m.imag], axis=0)
    half = FFT_N1 // 2
    sig_l = stack(f1[:, :half])
    sig_r = np.concatenate([-f1[:, :half].imag, f1[:, :half].real], axis=0)
    fil_r = stack(f1[:, half:])
    f2 = np.exp(-2j * np.pi * np.outer(n2, n2) / FFT_N2)
    tw = np.exp(-2j * np.pi * np.outer(n1, n2) / FFT_N)
    fwd = f2[None, :, :] * tw[:, None, :]
    inv = np.conj(np.transpose(fwd, (0, 2, 1))) / FFT_N
    block = lambda m: np.concatenate([np.concatenate([m.real, -m.imag], axis=2),
                                      np.concatenate([m.imag, m.real], axis=2)], axis=1)
    g1 = np.conj(f1[:half, :])
    out_l = stack(g1)
    out_r = np.concatenate([-g1.imag, g1.real], axis=0)
    as32 = lambda a: jnp.asarray(a.astype(np.float32))
    return dict(sig_l=as32(sig_l), sig_r=as32(sig_r), fil_r=as32(fil_r), fwd=as32(block(fwd)),
                inv=as32(block(inv)), out_l=as32(out_l), out_r=as32(out_r))


_N2_GROUP = 8


def _dft_a_kernel(*refs, conv):
    if conv:
        u0_ref, u1_ref, cw_ref, cb_ref, ml_ref, mr_ref, o_ref, xs = refs
    else:
        u0_ref, u1_ref, ml_ref, mr_ref, o_ref, xs = refs
    half = FFT_N1 // 2
    for r, u_ref in enumerate((u0_ref, u1_ref)):
        for n1 in range(half):
            lo = FFT_N2 * n1
            rows = _dwconv_rows(u_ref, lo, FFT_N2, cw_ref, cb_ref, 1) if conv else u_ref[lo:lo + FFT_N2, :]
            xs[r, PITCH * n1:PITCH * n1 + FFT_N2, :] = rows
    ml = ml_ref[...]
    mr = mr_ref[...]

    def body(g, carry):
        n2 = g * _N2_GROUP
        x0 = jnp.concatenate([xs[0, pl.ds(n2 + i, half, stride=PITCH), :] for i in range(_N2_GROUP)], axis=1)
        x1 = jnp.concatenate([xs[1, pl.ds(n2 + i, half, stride=PITCH), :] for i in range(_N2_GROUP)], axis=1)
        y = _bdot(ml, x0) + _bdot(mr, x1)
        for i in range(_N2_GROUP):
            o_ref[pl.ds(n2 + i, 2 * FFT_N1, stride=PITCH), :] = y[:, LANES * i:LANES * (i + 1)]
        return carry

    lax.fori_loop(0, FFT_N2 // _N2_GROUP, body, 0)
    pad = jnp.zeros((PITCH - FFT_N2, LANES), F32)
    for row in range(2 * FFT_N1):
        o_ref[PITCH * row + FFT_N2:PITCH * (row + 1), :] = pad


def _dft_a(ml, mr, srcs, groups, n_slabs, conv_args=None):
    conv = conv_args is not None
    slab = lambda arr_map: pl.BlockSpec((None, SEQ, LANES), arr_map)
    mspec = pl.BlockSpec((2 * FFT_N1, FFT_N1 // 2), lambda g, s: (0, 0))
    in_specs = [slab(srcs[0][1]), slab(srcs[1][1])]
    args = [srcs[0][0], srcs[1][0]]
    if conv:
        cw, cb, col0 = conv_args
        in_specs += [pl.BlockSpec((cw.shape[0], LANES), lambda g, s: (0, col0 + s)),
                     pl.BlockSpec((1, LANES), lambda g, s: (0, col0 + s))]
        args += [cw, cb]
    rows = 2 * FFT_N1 * PITCH
    return pl.pallas_call(
        functools.partial(_dft_a_kernel, conv=conv),
        grid=(groups, n_slabs),
        in_specs=in_specs + [mspec, mspec],
        out_specs=pl.BlockSpec((None, rows, LANES), lambda g, s: (g, 0, s)),
        out_shape=jax.ShapeDtypeStruct((groups, rows, n_slabs * LANES), F32),
        scratch_shapes=[pltpu.VMEM((2, (FFT_N1 // 2) * PITCH, LANES), F32)],
        compiler_params=_cparams(("parallel", "parallel")),
        name="dft_a",
    )(*args, ml, mr)


def _dft_spec_kernel(f_ref, a_ref, ssq_ref, o_ref):
    x = jnp.concatenate([a_ref[0, :FFT_N2, :], a_ref[1, :FFT_N2, :]], axis=0)
    y = _bdot(f_ref[...], x) * lax.rsqrt(ssq_ref[0:1, :] + EPS)
    o_ref[0] = y[:FFT_N2]
    o_ref[1] = y[FFT_N2:]


def _dft_spectrum(fwd, a5, ssq):
    depth, _, _, _, cw = a5.shape
    return pl.pallas_call(
        _dft_spec_kernel,
        grid=(depth, FFT_N1),
        in_specs=[pl.BlockSpec((None, 2 * FFT_N2, 2 * FFT_N2), lambda l, k: (k, 0, 0)),
                  pl.BlockSpec((None, 2, None, PITCH, cw), lambda l, k: (l, 0, k, 0, 0)),
                  pl.BlockSpec((None, 8, cw), lambda l, k: (l, 0, 0))],
        out_specs=pl.BlockSpec((None, 2, None, FFT_N2, cw), lambda l, k: (l, 0, k, 0, 0)),
        out_shape=jax.ShapeDtypeStruct((depth, 2, FFT_N1, FFT_N2, cw), F32),
        compiler_params=_cparams(("parallel", "parallel")),
        name="dft_spectrum",
    )(fwd, a5, ssq)


def _dft_mid_kernel(f_ref, g_ref, h_ref, a_ref, o_ref):
    hr = h_ref[0]
    hi = h_ref[1]
    pad = jnp.zeros((PITCH - FFT_N2, o_ref.shape[-1]), F32)
    for p in range(a_ref.shape[0]):
        x = jnp.concatenate([a_ref[p, 0, :FFT_N2, :], a_ref[p, 1, :FFT_N2, :]], axis=0)
        y = _bdot(f_ref[...], x)
        yr = y[:FFT_N2]
        yi = y[FFT_N2:]
        z = jnp.concatenate([yr * hr - yi * hi, yr * hi + yi * hr], axis=0)
        w = _bdot(g_ref[...], z)
        o_ref[p, 0, :FFT_N2, :] = w[:FFT_N2]
        o_ref[p, 1, :FFT_N2, :] = w[FFT_N2:]
        o_ref[p, 0, FFT_N2:, :] = pad
        o_ref[p, 1, FFT_N2:, :] = pad


def _dft_mid(fwd, inv, h5, layer, order, a5):
    pairs = a5.shape[0]
    blk = pl.BlockSpec((pairs, 2, None, PITCH, HY_W), lambda k: (0, 0, k, 0, 0))
    mat = pl.BlockSpec((None, 2 * FFT_N2, 2 * FFT_N2), lambda k: (k, 0, 0))
    return pl.pallas_call(
        _dft_mid_kernel,
        grid=(FFT_N1,),
        in_specs=[mat, mat,
                  pl.BlockSpec((None, 2, None, FFT_N2, HY_W), lambda k: (layer, 0, k, 0, order)),
                  blk],
        out_specs=blk,
        out_shape=jax.ShapeDtypeStruct(a5.shape, F32),
        compiler_params=_cparams(("parallel",)),
        name="dft_mid",
    )(fwd, inv, h5, a5)


def _dft_c_kernel(*refs, u_conv):
    if u_conv:
        (b_ref, u0_ref, u1_ref, g0_ref, g1_ref, ucw_ref, ucb_ref, gcw_ref, gcb_ref, ml_ref, mr_ref, skip_ref,
         o_ref, ys) = refs
    else:
        b_ref, u0_ref, u1_ref, g0_ref, g1_ref, gcw_ref, gcb_ref, ml_ref, mr_ref, skip_ref, o_ref, ys = refs
    half = FFT_N1 // 2
    ml = ml_ref[...]
    mr = mr_ref[...]
    im0 = FFT_N1 * PITCH

    def body(g, carry):
        n2 = g * _N2_GROUP
        br = jnp.concatenate([b_ref[pl.ds(n2 + i, FFT_N1, stride=PITCH), :] for i in range(_N2_GROUP)], axis=1)
        bi = jnp.concatenate([b_ref[pl.ds(im0 + n2 + i, FFT_N1, stride=PITCH), :] for i in range(_N2_GROUP)], axis=1)
        y = _bdot(ml, br) + _bdot(mr, bi)
        for i in range(_N2_GROUP):
            ys[0, pl.ds(n2 + i, half, stride=PITCH), :] = y[:half, LANES * i:LANES * (i + 1)]
            ys[1, pl.ds(n2 + i, half, stride=PITCH), :] = y[half:, LANES * i:LANES * (i + 1)]
        return carry

    lax.fori_loop(0, FFT_N2 // _N2_GROUP, body, 0)
    skip = skip_ref[...]
    for r, (u_ref, g_ref) in enumerate(((u0_ref, g0_ref), (u1_ref, g1_ref))):
        for n1 in range(half):
            lo = FFT_N2 * n1
            u = _dwconv_rows(u_ref, lo, FFT_N2, ucw_ref, ucb_ref, 1) if u_conv else u_ref[lo:lo + FFT_N2, :]
            gate = _dwconv_rows(g_ref, lo, FFT_N2, gcw_ref, gcb_ref, 1)
            conv = ys[r, PITCH * n1:PITCH * n1 + FFT_N2, :]
            o_ref[r, lo:lo + FFT_N2, :] = (gate * (conv + u * skip)).astype(o_ref.dtype)


def _dft_c(ml, mr, b3, u_src, gate_src, u_conv_args, gate_conv_args, skip, out_dtype):
    pairs = b3.shape[0]
    n_slabs = HY_W // LANES
    u_conv = u_conv_args is not None
    slab = lambda m: pl.BlockSpec((None, SEQ, LANES), m)
    wspecs = lambda cw, col0: [pl.BlockSpec((cw.shape[0], LANES), lambda p, s: (0, col0 + s)),
                               pl.BlockSpec((1, LANES), lambda p, s: (0, col0 + s))]
    in_specs = [pl.BlockSpec((None, 2 * FFT_N1 * PITCH, LANES), lambda p, s: (p, 0, s)),
                slab(u_src[1]), slab(u_src[2]), slab(gate_src[1]), slab(gate_src[2])]
    args = [b3, u_src[0], u_src[0], gate_src[0], gate_src[0]]
    if u_conv:
        in_specs += wspecs(u_conv_args[0], u_conv_args[2])
        args += [u_conv_args[0], u_conv_args[1]]
    in_specs += wspecs(gate_conv_args[0], gate_conv_args[2])
    args += [gate_conv_args[0], gate_conv_args[1]]
    mspec = pl.BlockSpec((FFT_N1, FFT_N1), lambda p, s: (0, 0))
    in_specs += [mspec, mspec, pl.BlockSpec((1, LANES), lambda p, s: (0, s))]
    args += [ml, mr, skip.reshape(1, HY_W)]
    return pl.pallas_call(
        functools.partial(_dft_c_kernel, u_conv=u_conv),
        grid=(pairs, n_slabs),
        in_specs=in_specs,
        out_specs=pl.BlockSpec((2, SEQ, LANES), lambda p, s: (p, 0, s)),
        out_shape=jax.ShapeDtypeStruct((2 * pairs, SEQ, HY_W), out_dtype),
        scratch_shapes=[pltpu.VMEM((2, (FFT_N1 // 2) * PITCH, LANES), F32)],
        compiler_params=_cparams(("parallel", "parallel")),
        name="dft_c",
    )(*args)


def _mixer_hyena(proj3, conv_w, conv_b, skip, h5, layer, consts):
    bsz = proj3.shape[0]
    pairs = bsz // 2
    n_slabs = HY_W // LANES
    cb = conv_b.reshape(1, 3 * HY_W)
    col = lambda which: (COL_CU + which * HY_W) // LANES
    proj_map = lambda which, odd: (lambda p, s: (2 * p + odd, 0, col(which) + s))
    plain_map = lambda odd: (lambda p, s: (2 * p + odd, 0, s))
    conv_args = lambda which: (conv_w, cb, which * n_slabs)
    a5_shape = (pairs, 2, FFT_N1, PITCH, HY_W)

    a = _dft_a(consts['sig_l'], consts['sig_r'], [(proj3, proj_map(0, 0)), (proj3, proj_map(0, 1))],
               pairs, n_slabs, conv_args(0))
    b = _dft_mid(consts['fwd'], consts['inv'], h5, layer, 0, a.reshape(a5_shape))
    z1 = _dft_c(consts['out_l'], consts['out_r'], b.reshape(a.shape),
                (proj3, proj_map(0, 0), proj_map(0, 1)), (proj3, proj_map(1, 0), proj_map(1, 1)),
                conv_args(0), conv_args(1), skip[0], F32)
    a = _dft_a(consts['sig_l'], consts['sig_r'], [(z1, plain_map(0)), (z1, plain_map(1))], pairs, n_slabs)
    b = _dft_mid(consts['fwd'], consts['inv'], h5, layer, 1, a.reshape(a5_shape))
    return _dft_c(consts['out_l'], consts['out_r'], b.reshape(a.shape),
                  (z1, plain_map(0), plain_map(1)), (proj3, proj_map(2, 0), proj_map(2, 1)),
                  None, conv_args(2), skip[1], BF16)


def _hyena_spectra(consts, hy_w1, hy_b1, hy_w2, hy_b2, hy_w3, hy_sin_freq, hy_decay):
    hf, hb, ssq = _hy_filters(hy_w1, hy_b1, hy_w2, hy_b2, hy_w3, hy_sin_freq, hy_decay)
    cw = HY_ORDER * HY_W
    fmap = lambda l, s: (l, 0, s)
    ha = _dft_a(consts['sig_l'], consts['fil_r'], [(hf, fmap), (hb, fmap)], DEPTH, cw // LANES)
    return _dft_spectrum(consts['fwd'], ha.reshape(DEPTH, 2, FFT_N1, PITCH, cw), ssq)


def _mlstm_chunk(q_ref, k_ref, v_ref, gc_ref, c_scr, m_scr, reverse, i_off, f_off, state_off):
    ch = ML_CHUNK
    ri = lax.broadcasted_iota(jnp.int32, (ch, ch), 0)
    ci = lax.broadcasted_iota(jnp.int32, (ch, ch), 1)
    tri = (ci >= ri) if reverse else (ci <= ri)
    tri_f = tri.astype(F32)
    gc = gc_ref[...]
    gr = gc.T
    b_col = jnp.dot(tri_f, _log_sigmoid(gc), precision=HIGHEST, preferred_element_type=F32)
    b_row = lax.dot_general(_log_sigmoid(gr[:4 * ML_HEADS, :]), tri_f, (((1,), (1,)), ((), ())), precision=HIGHEST,
                            preferred_element_type=F32)
    last = 0 if reverse else ch - 1
    lane = lax.broadcasted_iota(jnp.int32, (ch, ML_HEAD_DIM), 1)
    ones_col = jnp.where(lane == 0, 1.0, 0.0).astype(BF16)

    outs = []
    for h in range(ML_HEADS):
        sl = slice(h * ML_HEAD_DIM, (h + 1) * ML_HEAD_DIM)
        st = state_off + h
        q = (q_ref[:, sl] * (ML_HEAD_DIM ** -0.5)).astype(BF16)
        k = k_ref[:, sl]
        v_aug = jnp.concatenate([v_ref[:, sl].astype(BF16), ones_col], axis=1)
        bc = b_col[:, f_off + h:f_off + h + 1]
        lic = gc[:, i_off + h:i_off + h + 1]
        br = b_row[f_off + h:f_off + h + 1, :]
        lir = gr[i_off + h:i_off + h + 1, :]
        b_tot = bc[last:last + 1, :]
        d = jnp.where(tri, bc - br + lir, -jnp.inf)
        w_end = b_tot - bc + lic
        m_loc = jnp.max(w_end, axis=0, keepdims=True)
        e_end = jnp.exp(w_end - m_loc)
        m_prev = m_scr[st:st + 1, 0:1]
        c_prev = c_scr[st]
        m_inter = bc + m_prev
        m_t = jnp.maximum(m_inter, jnp.max(d, axis=-1, keepdims=True))
        e_inter = jnp.exp(m_inter - m_t)
        qk = lax.dot_general(q, k.astype(BF16), (((1,), (1,)), ((), ())), preferred_element_type=F32)
        s = qk * jnp.exp(d - m_t)
        nd = _bdot(s, v_aug) + e_inter * _bdot(q, c_prev)
        num = nd[:, :ML_HEAD_DIM]
        den = nd[:, ML_HEAD_DIM:ML_HEAD_DIM + 1]
        outs.append(num / jnp.maximum(jnp.abs(den), jnp.exp(-m_t)))
        m_new = jnp.maximum(b_tot + m_prev, m_loc)
        decay = jnp.exp(b_tot + m_prev - m_new)
        gain = jnp.exp(m_loc - m_new)
        dc = lax.dot_general((k * e_end).astype(BF16), v_aug, (((0,), (0,)), ((), ())),
                             preferred_element_type=F32)
        c_scr[st] = decay * c_prev + gain * dc
        m_scr[st:st + 1, :] = jnp.broadcast_to(m_new, (1, LANES))
    return outs


def _mlstm_kernel(qf_ref, kf_ref, vf_ref, gf_ref, qb_ref, kb_ref, vb_ref, gb_ref, hf_ref, hb_ref, c_scr, m_scr):
    @pl.when(pl.program_id(1) == 0)
    def _():
        c_scr[...] = jnp.zeros_like(c_scr)
        m_scr[...] = jnp.zeros_like(m_scr)

    outs_f = _mlstm_chunk(qf_ref, kf_ref, vf_ref, gf_ref, c_scr, m_scr, False, 0, ML_HEADS, 0)
    outs_b = _mlstm_chunk(qb_ref, kb_ref, vb_ref, gb_ref, c_scr, m_scr, True, 2 * ML_HEADS, 3 * ML_HEADS, ML_HEADS)
    hf_ref[...] = jnp.concatenate(outs_f, axis=1)
    hb_ref[...] = jnp.concatenate(outs_b, axis=1)


def _mixer_mlstm(proj2, bsz):
    nc = SEQ // ML_CHUNK

    def specs(chunk_of):
        at = lambda col, width: pl.BlockSpec((pl.Element(ML_CHUNK), pl.Element(width)),
                                             lambda b, j: (pl.multiple_of(b * SEQ + chunk_of(j) * ML_CHUNK, ML_CHUNK),
                                                           col))
        return [at(COL_DQ, GROUP_W), at(COL_DK, GROUP_W), at(COL_DV, GROUP_W), at(COL_GATES, LANES)]

    fwd_of = lambda j: j
    bwd_of = lambda j: nc - 1 - j
    out = lambda chunk_of: pl.BlockSpec((None, ML_CHUNK, GROUP_W), lambda b, j: (b, chunk_of(j), 0))
    return pl.pallas_call(
        _mlstm_kernel,
        grid=(bsz, nc),
        in_specs=specs(fwd_of) + specs(bwd_of),
        out_specs=[out(fwd_of), out(bwd_of)],
        out_shape=[jax.ShapeDtypeStruct((bsz, SEQ, GROUP_W), F32)] * 2,
        scratch_shapes=[pltpu.VMEM((2 * ML_HEADS, ML_HEAD_DIM, 2 * ML_HEAD_DIM), F32), pltpu.VMEM((8, LANES), F32)],
        compiler_params=_cparams(("parallel", "arbitrary")),
        name="mlstm",
    )(*([proj2] * 8))


def _outproj_kernel(lf_ref, lb_ref, ga_ref, yb_ref, yc_ref, mf_ref, mb_ref, o_ref, mg_ref, w_ref, x_ref, m_ref,
                    out_ref):
    ya = (jax.nn.gelu(ga_ref[...]) * (lf_ref[...] + lb_ref[...])).astype(BF16)
    normed = []
    for h in range(ML_HEADS):
        sl = slice(h * ML_HEAD_DIM, (h + 1) * ML_HEAD_DIM)
        hh = mf_ref[:, sl] + mb_ref[:, sl]
        ms = jnp.mean(hh * hh, axis=-1, keepdims=True)
        normed.append(hh * lax.rsqrt(ms + EPS) * mg_ref[:, sl])
    yd = (_sigmoid(o_ref[...]) * jnp.concatenate(normed, axis=1)).astype(BF16)
    acc = jnp.dot(ya, w_ref[0:GROUP_W, :], preferred_element_type=F32)
    acc += jnp.dot(yb_ref[...], w_ref[GROUP_W:2 * GROUP_W, :], preferred_element_type=F32)
    acc += jnp.dot(yc_ref[...], w_ref[2 * GROUP_W:3 * GROUP_W, :], preferred_element_type=F32)
    acc += jnp.dot(yd, w_ref[3 * GROUP_W:, :], preferred_element_type=F32)
    out_ref[...] = x_ref[...] + m_ref[2:3, :] * acc


def _outproj(lru_f, lru_b, proj2, y_b, y_c, ml_f, ml_b, ml_g, w_all, layer, x2, mod_l):
    n = x2.shape[0]
    tm = 512
    per_b = SEQ // tm
    grp = pl.BlockSpec((tm, GROUP_W), lambda i: (i, 0))
    return pl.pallas_call(
        _outproj_kernel,
        grid=(n // tm,),
        in_specs=[grp, grp, pl.BlockSpec((tm, GROUP_W), lambda i: (i, COL_AG // GROUP_W)), grp, grp, grp, grp,
                  pl.BlockSpec((pl.Element(tm), pl.Element(GROUP_W)), lambda i: (pl.multiple_of(i * tm, tm), COL_DO)),
                  pl.BlockSpec((1, GROUP_W), lambda i: (0, 0)),
                  pl.BlockSpec((None, D_MODEL, D_MODEL), lambda i: (layer, 0, 0)),
                  pl.BlockSpec((tm, D_MODEL), lambda i: (i, 0)),
                  pl.BlockSpec((None, 6, D_MODEL), lambda i: (i // per_b, 0, 0))],
        out_specs=pl.BlockSpec((tm, D_MODEL), lambda i: (i, 0)),
        out_shape=jax.ShapeDtypeStruct((n, D_MODEL), F32),
        compiler_params=_cparams(("parallel",)),
        name="out_proj",
    )(lru_f, lru_b, proj2, y_b, y_c, ml_f, ml_b, proj2, ml_g.reshape(1, GROUP_W), w_all, x2, mod_l)


def _ffn_kernel(x_ref, m_ref, g_ref, w1_ref, w3_ref, w2_ref, fg_ref, o_ref, h_scr, *, final):
    j = pl.program_id(1)

    @pl.when(j == 0)
    def _():
        h_scr[...] = _rms_mod(x_ref[...], g_ref[...], m_ref[4:5, :], m_ref[3:4, :]).astype(BF16)
        o_ref[...] = jnp.zeros_like(o_ref)

    h = h_scr[...]
    a = jnp.dot(h, w1_ref[...], preferred_element_type=F32)
    b = jnp.dot(h, w3_ref[...], preferred_element_type=F32)
    act = (a * _sigmoid(a)) * b
    o_ref[...] += jnp.dot(act.astype(BF16), w2_ref[...], preferred_element_type=F32)

    @pl.when(j == pl.num_programs(1) - 1)
    def _():
        y = x_ref[...] + m_ref[5:6, :] * o_ref[...]
        if final:
            ms = jnp.mean(y * y, axis=-1, keepdims=True)
            y = y * lax.rsqrt(ms + EPS) * fg_ref[...]
        o_ref[...] = y


def _ffn(x2, mod_l, g, w1_all, w3_all, w2_all, layer, final_g, final):
    n = x2.shape[0]
    tm, tf = 512, 512
    per_b = SEQ // tm
    row = lambda: pl.BlockSpec((1, D_MODEL), lambda i, j: (0, 0))
    return pl.pallas_call(
        functools.partial(_ffn_kernel, final=final),
        grid=(n // tm, D_FF // tf),
        in_specs=[pl.BlockSpec((tm, D_MODEL), lambda i, j: (i, 0)),
                  pl.BlockSpec((None, 6, D_MODEL), lambda i, j: (i // per_b, 0, 0)),
                  row(),
                  pl.BlockSpec((None, D_MODEL, tf), lambda i, j: (layer, 0, j)),
                  pl.BlockSpec((None, D_MODEL, tf), lambda i, j: (layer, 0, j)),
                  pl.BlockSpec((None, tf, D_MODEL), lambda i, j: (layer, j, 0)),
                  row()],
        out_specs=pl.BlockSpec((tm, D_MODEL), lambda i, j: (i, 0)),
        out_shape=jax.ShapeDtypeStruct((n, D_MODEL), F32),
        scratch_shapes=[pltpu.VMEM((tm, D_MODEL), BF16)],
        compiler_params=_cparams(("parallel", "arbitrary")),
        name="ffn",
    )(x2, mod_l, g.reshape(1, D_MODEL), w1_all, w3_all, w2_all, final_g.reshape(1, D_MODEL))


def kernel(x, c, w_in, b_in, w_out, norm_mix_g, norm_ffn_g, ada_w, ada_b, lru_conv_w, lru_conv_b, lru_wa, lru_ba, lru_wx, lru_bx, lru_lambda, att_q_norm_g, att_k_norm_g, hy_conv_w, hy_conv_b, hy_w1, hy_b1, hy_w2, hy_b2, hy_w3, hy_sin_freq, hy_decay, hy_skip, ml_norm_g, ffn_w1, ffn_w3, ffn_w2, final_g):
    bsz = x.shape[0]
    assert x.shape == (bsz, SEQ, D_MODEL) and bsz % 2 == 0
    n = bsz * SEQ
    mod = _ada_all(c, ada_w, ada_b)

    consts = {k: v.astype(BF16) for k, v in _dft_constants().items()}
    h5 = _hyena_spectra(consts, hy_w1, hy_b1, hy_w2, hy_b2, hy_w3, hy_sin_freq, hy_decay)

    pad = D_IN_PAD - D_IN
    w_in_b = jnp.pad(w_in.astype(BF16), ((0, 0), (0, 0), (0, pad)))
    b_in_p = jnp.pad(b_in, ((0, 0), (0, pad))).reshape(DEPTH, 1, D_IN_PAD)
    w_out_b = w_out.astype(BF16)
    w1_b, w3_b, w2_b = ffn_w1.astype(BF16), ffn_w3.astype(BF16), ffn_w2.astype(BF16)

    x2 = x.reshape(n, D_MODEL)
    for l in range(DEPTH):
        proj2 = _inproj(x2, mod[l], norm_mix_g[l], w_in_b, b_in_p, l)
        proj3 = proj2.reshape(bsz, SEQ, D_IN_PAD)
        lru_f, lru_b = _mixer_rglru(proj3, lru_conv_w[l], lru_conv_b[l], lru_wa[l], lru_ba[l], lru_wx[l], lru_bx[l],
                                    lru_lambda[l])
        y_b = _mixer_attention(proj3, att_q_norm_g[l], att_k_norm_g[l])
        y_c = _mixer_hyena(proj3, hy_conv_w[l], hy_conv_b[l], hy_skip[l], h5, l, consts)
        ml_f, ml_b = _mixer_mlstm(proj2, bsz)
        flat = lambda a: a.reshape(n, GROUP_W)
        x2 = _outproj(flat(lru_f), flat(lru_b), proj2, flat(y_b), flat(y_c), flat(ml_f), flat(ml_b), ml_norm_g[l],
                      w_out_b, l, x2, mod[l])
        x2 = _ffn(x2, mod[l], norm_ffn_g[l], w1_b, w3_b, w2_b, l, final_g, final=(l == DEPTH - 1))
    return x2.reshape(bsz, SEQ, D_MODEL)
```

```python
import functools
import math

import numpy as np
import jax
import jax.numpy as jnp
from jax import lax
from jax.experimental import pallas as pl
from jax.experimental.pallas import tpu as pltpu

F32 = jnp.float32
BF16 = jnp.bfloat16
HIGHEST = lax.Precision.HIGHEST

D_MODEL = 2048
SEQ = 4096
DEPTH = 2
GROUP_W = 512
LRU_BLOCKS = 8
LRU_C = 8.0
ATT_HEADS = 8
ATT_KV_HEADS = 2
ATT_GROUP = ATT_HEADS // ATT_KV_HEADS
ATT_HEAD_DIM = 64
ROPE_AXIS = ATT_HEAD_DIM // 2
ROPE_THETA = 10000.0
GRID_W = 64
ATT_V_ROWS = 80
LOG2_E = math.log2(math.e)
HY_W = GROUP_W
HY_ORDER = 2
HY_BANDS = 8
HY_EMB = 2 * HY_BANDS + 1
HY_EMB_PAD = 32
HY_FFN = 64
ML_HEADS = 4
ML_HEAD_DIM = 128
ML_CHUNK = 128
D_FF = 5632
EPS = 1e-6
IN_SIZES = (512, 512, 512, 128, 128, 1536, 512, 512, 512, 512, 16)
D_IN = sum(IN_SIZES)
D_IN_PAD = 5632

COL_AX, COL_AG, COL_BQ, COL_BK, COL_BV, COL_CU = 0, 512, 1024, 1536, 1664, 1792
COL_DQ, COL_DK, COL_DV, COL_DO, COL_GATES = 3328, 3840, 4352, 4864, 5376
LANES = 128

FFT_N = 2 * SEQ
FFT_N1 = 64
FFT_N2 = 128
PITCH = 136

VMEM_LIMIT = 56 * 1024 * 1024


def _cparams(sem, vmem=VMEM_LIMIT):
    return pltpu.CompilerParams(dimension_semantics=sem, vmem_limit_bytes=vmem)


def _bdot(a, b):
    return jnp.dot(a.astype(BF16), b.astype(BF16), preferred_element_type=F32)


def _sigmoid(x):
    return jax.nn.sigmoid(x)


def _log_sigmoid(x):
    return jnp.minimum(x, 0.0) - jnp.log1p(jnp.exp(-jnp.abs(x)))


def _softplus(x):
    return jnp.maximum(x, 0.0) + jnp.log1p(jnp.exp(-jnp.abs(x)))


def _ada_kernel(c_ref, w_ref, b_ref, o_ref):
    c = c_ref[...]
    o_ref[...] = _bdot(c * _sigmoid(c), w_ref[...]) + b_ref[...]


def _ada_all(c, ada_w, ada_b):
    bsz = c.shape[0]
    rows = 8
    cp = jnp.zeros((rows, D_MODEL), F32).at[:bsz].set(c)
    tn = 1024
    out = pl.pallas_call(
        _ada_kernel,
        grid=(DEPTH, 6 * D_MODEL // tn),
        in_specs=[pl.BlockSpec((rows, D_MODEL), lambda l, j: (0, 0)),
                  pl.BlockSpec((None, D_MODEL, tn), lambda l, j: (l, 0, j)),
                  pl.BlockSpec((None, 1, tn), lambda l, j: (l, 0, j))],
        out_specs=pl.BlockSpec((None, rows, tn), lambda l, j: (l, 0, j)),
        out_shape=jax.ShapeDtypeStruct((DEPTH, rows, 6 * D_MODEL), F32),
        compiler_params=_cparams(("parallel", "parallel")),
        name="ada_mod",
    )(cp, ada_w, ada_b.reshape(DEPTH, 1, 6 * D_MODEL))
    return out[:, :bsz].reshape(DEPTH, bsz, 6, D_MODEL)


def _rms_mod(x, g, scale, shift):
    ms = jnp.mean(x * x, axis=-1, keepdims=True)
    return (x * lax.rsqrt(ms + EPS) * g) * (1.0 + scale) + shift


def _inproj_kernel(x_ref, m_ref, g_ref, w_ref, b_ref, o_ref, h_scr):
    @pl.when(pl.program_id(1) == 0)
    def _():
        h_scr[...] = _rms_mod(x_ref[...], g_ref[...], m_ref[1:2, :], m_ref[0:1, :]).astype(BF16)

    o_ref[...] = jnp.dot(h_scr[...], w_ref[...], preferred_element_type=F32) + b_ref[...]


def _inproj(x2, mod_l, g, w_all, b_all, layer):
    n = x2.shape[0]
    tm, tn = 1024, 1408
    per_b = SEQ // tm
    return pl.pallas_call(
        _inproj_kernel,
        grid=(n // tm, D_IN_PAD // tn),
        in_specs=[pl.BlockSpec((tm, D_MODEL), lambda i, j: (i, 0)),
                  pl.BlockSpec((None, 6, D_MODEL), lambda i, j: (i // per_b, 0, 0)),
                  pl.BlockSpec((1, D_MODEL), lambda i, j: (0, 0)),
                  pl.BlockSpec((None, D_MODEL, tn), lambda i, j: (layer, 0, j)),
                  pl.BlockSpec((None, 1, tn), lambda i, j: (layer, 0, j))],
        out_specs=pl.BlockSpec((tm, tn), lambda i, j: (i, j)),
        out_shape=jax.ShapeDtypeStruct((n, D_IN_PAD), F32),
        scratch_shapes=[pltpu.VMEM((tm, D_MODEL), BF16)],
        compiler_params=_cparams(("parallel", "arbitrary")),
        name="in_proj",
    )(x2, mod_l, g.reshape(1, D_MODEL), w_all, b_all)


def _shifted(ext, off, rows):
    total = ext.shape[0]
    if off == 0:
        return ext[8:8 + rows]
    return pltpu.roll(ext, (-off) % total, axis=0)[8:8 + rows]


def _dwconv_ext(ext, w_ref, b_ref, left, rows):
    out = b_ref[...]
    for j in range(w_ref.shape[0]):
        out = out + _shifted(ext, j - left, rows) * w_ref[j:j + 1, :]
    return out


def _dwconv_tile(x_ref, p_ref, n_ref, w_ref, b_ref, tile, n_tiles, left):
    prev = jnp.where(tile > 0, p_ref[...], 0.0)
    nxt = jnp.where(tile < n_tiles - 1, n_ref[...], 0.0)
    ext = jnp.concatenate([prev, x_ref[...], nxt], axis=0)
    return _dwconv_ext(ext, w_ref, b_ref, left, x_ref.shape[0])


def _dwconv_rows(ref, lo, rows, w_ref, b_ref, left):
    zeros = jnp.zeros((8, ref.shape[1]), F32)
    prev = ref[lo - 8:lo, :] if lo > 0 else zeros
    nxt = ref[lo + rows:lo + rows + 8, :] if lo + rows < SEQ else zeros
    ext = jnp.concatenate([prev, ref[lo:lo + rows, :], nxt], axis=0)
    return _dwconv_ext(ext, w_ref, b_ref, left, rows)


def _halo_specs(tile_rows, width, col_block, tile_of):
    r8 = tile_rows // 8
    last8 = SEQ // 8 - 1

    def main(b, t):
        return (b, tile_of(t), col_block)

    def prev(b, t):
        return (b, jnp.maximum(tile_of(t) * r8 - 1, 0), col_block)

    def nxt(b, t):
        return (b, jnp.minimum((tile_of(t) + 1) * r8, last8), col_block)

    return [pl.BlockSpec((None, tile_rows, width), main),
            pl.BlockSpec((None, 8, width), prev),
            pl.BlockSpec((None, 8, width), nxt)]


def _lru_gates(xc, wg, bg, lam):
    gates = _bdot(xc, wg) + bg
    r = _sigmoid(gates[:, :GROUP_W])
    i = _sigmoid(gates[:, GROUP_W:])
    log_a = (-LRU_C * _softplus(-lam)) * r
    a = jnp.exp(log_a)
    th = jnp.tanh(log_a)
    u = jnp.sqrt(-2.0 * th / (1.0 - th)) * (i * xc)
    return a, u


def _scan8(a, u, ridx, reverse):
    for k in (1, 2, 4):
        if reverse:
            keep = ridx < 8 - k
            sh = 8 - k
        else:
            keep = ridx >= k
            sh = k
        a_sh = jnp.where(keep, pltpu.roll(a, sh, axis=0), 1.0)
        u_sh = jnp.where(keep, pltpu.roll(u, sh, axis=0), 0.0)
        u = a * u_sh + u
        a = a * a_sh
    return a, u


def _lru_kernel(xf_ref, pf_ref, nf_ref, xb_ref, pb_ref, nb_ref, cw_ref, cb_ref, wg_ref, bg_ref, lam_ref,
                hf_ref, hb_ref, af_scr, uf_scr, ab_scr, ub_scr, c_scr, *, n_tiles):
    t = pl.program_id(1)
    xc = _dwconv_tile(xf_ref, pf_ref, nf_ref, cw_ref, cb_ref, t, n_tiles, 2)
    a, u = _lru_gates(xc, wg_ref[0], bg_ref[0], lam_ref[0])
    af_scr[...] = a
    uf_scr[...] = u
    xc = _dwconv_tile(xb_ref, pb_ref, nb_ref, cw_ref, cb_ref, n_tiles - 1 - t, n_tiles, 2)
    a, u = _lru_gates(xc, wg_ref[1], bg_ref[1], lam_ref[1])
    ab_scr[...] = a
    ub_scr[...] = u

    @pl.when(t == 0)
    def _():
        c_scr[...] = jnp.zeros_like(c_scr)

    n_chunks = af_scr.shape[0] // 8
    ridx = lax.broadcasted_iota(jnp.int32, (8, GROUP_W), 0)

    def body(c, carry):
        cf, cb = carry
        r0 = pl.multiple_of(c * 8, 8)
        a, u = _scan8(af_scr[pl.ds(r0, 8), :], uf_scr[pl.ds(r0, 8), :], ridx, False)
        h = u + a * cf
        hf_ref[pl.ds(r0, 8), :] = h
        cf = h[7:8, :]
        r1 = pl.multiple_of((n_chunks - 1 - c) * 8, 8)
        a, u = _scan8(ab_scr[pl.ds(r1, 8), :], ub_scr[pl.ds(r1, 8), :], ridx, True)
        h = u + a * cb
        hb_ref[pl.ds(r1, 8), :] = h
        return cf, h[0:1, :]

    cf, cb = lax.fori_loop(0, n_chunks, body, (c_scr[0:1, :], c_scr[1:2, :]), unroll=2)
    c_scr[0:1, :] = cf
    c_scr[1:2, :] = cb


def _block_diag(w):
    nb, k, j = w.shape
    eye = jnp.eye(nb, dtype=w.dtype)
    return jnp.einsum('nkj,nm->nkmj', w, eye).reshape(nb * k, nb * j)


def _mixer_rglru(proj3, conv_w, conv_b, wa, ba, wx, bx, lam):
    bsz = proj3.shape[0]
    ts = 512
    n_tiles = SEQ // ts
    wg = jnp.stack([jnp.concatenate([_block_diag(wa[d]), _block_diag(wx[d])], axis=1) for d in range(2)]).astype(BF16)
    bg = jnp.stack([jnp.concatenate([ba[d], bx[d]]).reshape(1, 2 * GROUP_W) for d in range(2)])
    small = lambda *shape: pl.BlockSpec(shape, lambda b, t: (0,) * len(shape))
    fwd_of = lambda t: t
    bwd_of = lambda t: n_tiles - 1 - t
    out = lambda tile_of: pl.BlockSpec((None, ts, GROUP_W), lambda b, t: (b, tile_of(t), 0))
    tile_scr = pltpu.VMEM((ts, GROUP_W), F32)
    return pl.pallas_call(
        functools.partial(_lru_kernel, n_tiles=n_tiles),
        grid=(bsz, n_tiles),
        in_specs=_halo_specs(ts, GROUP_W, COL_AX // GROUP_W, fwd_of) + _halo_specs(ts, GROUP_W, COL_AX // GROUP_W, bwd_of)
        + [small(4, GROUP_W), small(1, GROUP_W), small(2, GROUP_W, 2 * GROUP_W), small(2, 1, 2 * GROUP_W),
           small(2, 1, GROUP_W)],
        out_specs=[out(fwd_of), out(bwd_of)],
        out_shape=[jax.ShapeDtypeStruct((bsz, SEQ, GROUP_W), F32)] * 2,
        scratch_shapes=[tile_scr, tile_scr, tile_scr, tile_scr, pltpu.VMEM((8, GROUP_W), F32)],
        compiler_params=_cparams(("parallel", "arbitrary")),
        name="rglru",
    )(proj3, proj3, proj3, proj3, proj3, proj3, conv_w, conv_b.reshape(1, GROUP_W), wg, bg,
      lam.reshape(2, 1, GROUP_W))


def _split_dot(x, m_ref):
    hi = x.astype(BF16)
    lo = (x - hi.astype(F32)).astype(BF16)
    m = m_ref[...]
    return (jnp.dot(hi, m, preferred_element_type=F32) + jnp.dot(lo, m, preferred_element_type=F32))


def _norm_rope(x, gain, m_ref, cos, sins):
    width = x.shape[1]
    ms = _split_dot(x * x, m_ref)
    xn = x * lax.rsqrt(ms + EPS) * gain
    lane = lax.broadcasted_iota(jnp.int32, xn.shape, 1)
    first = (lane % ROPE_AXIS) < (ROPE_AXIS // 2)
    half = ROPE_AXIS // 2
    partner = jnp.where(first, pltpu.roll(xn, width - half, axis=1), pltpu.roll(xn, half, axis=1))
    return xn * cos + partner * sins


def _attn_prep_kernel(q_ref, kv_ref, gq_ref, gk_ref, mq_ref, mk_ref, cos_ref, sin_ref,
                      qo_ref, kt_ref, vt_ref):
    cos = cos_ref[...]
    sins = sin_ref[...]
    cos_q = jnp.concatenate([cos] * (GROUP_W // LANES), axis=1)
    sin_q = jnp.concatenate([sins] * (GROUP_W // LANES), axis=1)
    q = _norm_rope(q_ref[...], gq_ref[...], mq_ref, cos_q, sin_q)
    qo_ref[...] = (q * (ATT_HEAD_DIM ** -0.5 * LOG2_E)).astype(BF16)
    kv = kv_ref[...]
    kt_ref[...] = _norm_rope(kv[:, :LANES], gk_ref[...], mk_ref, cos, sins).T.astype(BF16)
    v_t = kv[:, LANES:].T
    row = lax.broadcasted_iota(jnp.int32, (ATT_V_ROWS - ATT_HEAD_DIM, v_t.shape[1]), 0)
    ones_rows = jnp.where(row == 0, 1.0, 0.0)
    for g in range(ATT_KV_HEADS):
        vt_ref[g] = jnp.concatenate([v_t[g * ATT_HEAD_DIM:(g + 1) * ATT_HEAD_DIM, :], ones_rows], axis=0).astype(BF16)


def _rope_tables():
    rows = SEQ // GRID_W
    row = jnp.repeat(jnp.arange(rows, dtype=F32), GRID_W)
    col = jnp.tile(jnp.arange(GRID_W, dtype=F32), rows)
    inv = ROPE_THETA ** (-jnp.arange(0, ROPE_AXIS, 2, dtype=F32) / ROPE_AXIS)
    ar = row[:, None] * inv
    ac = col[:, None] * inv
    ang = jnp.concatenate([ar, ar, ac, ac], axis=1)
    sign = jnp.concatenate([-jnp.ones((ROPE_AXIS // 2,), F32), jnp.ones((ROPE_AXIS // 2,), F32)] * 2)
    cos = jnp.tile(jnp.cos(ang), (1, 2))
    sins = jnp.tile(jnp.sin(ang) * sign, (1, 2))
    return cos, sins


def _head_mean_matrix(width):
    idx = np.arange(width) // ATT_HEAD_DIM
    return jnp.asarray((idx[:, None] == idx[None, :]).astype(np.float32) / ATT_HEAD_DIM, dtype=BF16)


def _attn_kernel(q_ref, kt_ref, vt_ref, o_ref):
    outs = []
    for h in range(ATT_HEADS):
        g = h // ATT_GROUP
        q = q_ref[:, h * ATT_HEAD_DIM:(h + 1) * ATT_HEAD_DIM]
        s = jnp.dot(q, kt_ref[g * ATT_HEAD_DIM:(g + 1) * ATT_HEAD_DIM, :], preferred_element_type=F32)
        m = jnp.max(s, axis=-1, keepdims=True)
        p = jnp.exp2(s - m).astype(BF16)
        o_aug = lax.dot_general(vt_ref[g], p, (((1,), (1,)), ((), ())), preferred_element_type=F32)
        o_t = o_aug[:ATT_HEAD_DIM] / o_aug[ATT_HEAD_DIM:ATT_HEAD_DIM + 1]
        outs.append(o_t.T)
    o_ref[...] = jnp.concatenate(outs, axis=1).astype(o_ref.dtype)


def _mixer_attention(proj3, q_g, k_g):
    bsz = proj3.shape[0]
    ts = 512
    cos, sins = _rope_tables()
    gq = jnp.tile(q_g, ATT_HEADS).reshape(1, GROUP_W)
    gk = jnp.tile(k_g, ATT_KV_HEADS).reshape(1, LANES)
    const = lambda shape: pl.BlockSpec(shape, lambda b, t: (0, 0))
    qp, kt, vt = pl.pallas_call(
        _attn_prep_kernel,
        grid=(bsz, SEQ // ts),
        in_specs=[pl.BlockSpec((None, ts, GROUP_W), lambda b, t: (b, t, COL_BQ // GROUP_W)),
                  pl.BlockSpec((None, ts, 2 * LANES), lambda b, t: (b, t, COL_BK // (2 * LANES))),
                  const((1, GROUP_W)), const((1, LANES)), const((GROUP_W, GROUP_W)), const((LANES, LANES)),
                  pl.BlockSpec((ts, LANES), lambda b, t: (t, 0)),
                  pl.BlockSpec((ts, LANES), lambda b, t: (t, 0))],
        out_specs=[pl.BlockSpec((None, ts, GROUP_W), lambda b, t: (b, t, 0)),
                   pl.BlockSpec((None, LANES, ts), lambda b, t: (b, 0, t)),
                   pl.BlockSpec((None, ATT_KV_HEADS, ATT_V_ROWS, ts), lambda b, t: (b, 0, 0, t))],
        out_shape=[jax.ShapeDtypeStruct((bsz, SEQ, GROUP_W), BF16),
                   jax.ShapeDtypeStruct((bsz, LANES, SEQ), BF16),
                   jax.ShapeDtypeStruct((bsz, ATT_KV_HEADS, ATT_V_ROWS, SEQ), BF16)],
        compiler_params=_cparams(("parallel", "parallel")),
        name="attn_prep",
    )(proj3, proj3, gq, gk, _head_mean_matrix(GROUP_W), _head_mean_matrix(LANES), cos, sins)

    tq = 256
    return pl.pallas_call(
        _attn_kernel,
        grid=(bsz, SEQ // tq),
        in_specs=[pl.BlockSpec((None, tq, GROUP_W), lambda b, t: (b, t, 0)),
                  pl.BlockSpec((None, LANES, SEQ), lambda b, t: (b, 0, 0)),
                  pl.BlockSpec((None, ATT_KV_HEADS, ATT_V_ROWS, SEQ), lambda b, t: (b, 0, 0, 0))],
        out_specs=pl.BlockSpec((None, tq, GROUP_W), lambda b, t: (b, t, 0)),
        out_shape=jax.ShapeDtypeStruct((bsz, SEQ, GROUP_W), BF16),
        compiler_params=_cparams(("parallel", "parallel")),
        name="attention",
    )(qp, kt, vt)


def _hy_features():
    L = SEQ
    pos = jnp.arange(L, dtype=F32)
    t = pos / max(L - 1, 1)
    bands = jnp.linspace(1e-4, HY_BANDS - 1, HY_BANDS, dtype=F32)
    ang = (2.0 * math.pi * pos / L)[:, None] * bands
    feat = jnp.concatenate([t[:, None], jnp.cos(ang), -jnp.sin(ang)], axis=-1)
    feat = jnp.pad(feat, ((0, 0), (0, HY_EMB_PAD - HY_EMB)))
    rev_idx = np.concatenate([[0], np.arange(L - 1, 0, -1)])
    return feat, feat[rev_idx]


def _hy_mlp(feat, feat_t, w1t_ref, b1_ref, w2t_ref, b2_ref, w3_ref, sf_ref, dec_ref):
    sf = sf_ref[...]
    h = jnp.sin(sf * (jnp.dot(w1t_ref[...], feat_t, precision=HIGHEST, preferred_element_type=F32) + b1_ref[...]))
    h = jnp.sin(sf * (jnp.dot(w2t_ref[...], h, precision=HIGHEST, preferred_element_type=F32) + b2_ref[...]))
    out = jnp.dot(h.T, w3_ref[...], precision=HIGHEST, preferred_element_type=F32)
    return out * jnp.exp(-feat[:, 0:1] * jnp.abs(dec_ref[...]))


def _hy_filter_kernel(ff_ref, fft_ref, fr_ref, frt_ref, w1t_ref, b1_ref, w2t_ref, b2_ref, w3f_ref, w3b_ref, sf_ref,
                      decf_ref, decb_ref, of_ref, ob_ref, ssq_ref):
    i = pl.program_id(1)
    hf = _hy_mlp(ff_ref[...], fft_ref[...], w1t_ref, b1_ref, w2t_ref, b2_ref, w3f_ref, sf_ref, decf_ref)
    hb = _hy_mlp(fr_ref[...], frt_ref[...], w1t_ref, b1_ref, w2t_ref, b2_ref, w3b_ref, sf_ref, decb_ref)

    @pl.when(i == 0)
    def _():
        ssq_ref[...] = jnp.zeros_like(ssq_ref)

    ssq_ref[0:1, :] += jnp.sum(hf * hf + hb * hb, axis=0, keepdims=True)
    of_ref[...] = hf
    row = lax.broadcasted_iota(jnp.int32, hb.shape, 0)
    ob_ref[...] = jnp.where(jnp.logical_and(i == 0, row == 0), 0.0, hb)


def _hy_filters(hy_w1, hy_b1, hy_w2, hy_b2, hy_w3, hy_sin_freq, hy_decay):
    feat, feat_rev = _hy_features()
    tr = 512
    cw = HY_ORDER * HY_W
    w1t = jnp.transpose(jnp.pad(hy_w1, ((0, 0), (0, HY_EMB_PAD - HY_EMB), (0, 0))), (0, 2, 1))
    w2t = jnp.transpose(hy_w2, (0, 2, 1))
    w3 = hy_w3.reshape(DEPTH, HY_FFN, HY_ORDER, 2, HY_W)
    dec = hy_decay.reshape(DEPTH, HY_ORDER, 2, HY_W)
    w3f = w3[:, :, :, 0].reshape(DEPTH, HY_FFN, cw)
    w3b = w3[:, :, :, 1].reshape(DEPTH, HY_FFN, cw)
    decf = dec[:, :, 0].reshape(DEPTH, 1, cw)
    decb = dec[:, :, 1].reshape(DEPTH, 1, cw)
    col = lambda a: a.reshape(DEPTH, HY_FFN, 1)
    row_spec = pl.BlockSpec((tr, HY_EMB_PAD), lambda l, i: (i, 0))
    rowt_spec = pl.BlockSpec((HY_EMB_PAD, tr), lambda l, i: (0, i))
    per_layer = lambda a, b: pl.BlockSpec((None, a, b), lambda l, i: (l, 0, 0))
    out_spec = pl.BlockSpec((None, tr, cw), lambda l, i: (l, i, 0))
    return pl.pallas_call(
        _hy_filter_kernel,
        grid=(DEPTH, SEQ // tr),
        in_specs=[row_spec, rowt_spec, row_spec, rowt_spec, per_layer(HY_FFN, HY_EMB_PAD), per_layer(HY_FFN, 1),
                  per_layer(HY_FFN, HY_FFN), per_layer(HY_FFN, 1), per_layer(HY_FFN, cw), per_layer(HY_FFN, cw),
                  per_layer(HY_FFN, 1), per_layer(1, cw), per_layer(1, cw)],
        out_specs=[out_spec, out_spec, per_layer(8, cw)],
        out_shape=[jax.ShapeDtypeStruct((DEPTH, SEQ, cw), F32)] * 2 + [jax.ShapeDtypeStruct((DEPTH, 8, cw), F32)],
        compiler_params=_cparams(("arbitrary", "arbitrary")),
        name="hyena_filter",
    )(feat, feat.T, feat_rev, feat_rev.T, w1t, col(hy_b1), w2t, col(hy_b2), w3f, w3b, col(hy_sin_freq), decf, decb)


def _dft_constants():
    n1 = np.arange(FFT_N1)
    n2 = np.arange(FFT_N2)
    f1 = np.exp(-2j * np.pi * np.outer(n1, n1) / FFT_N1)
    stack = lambda m: np.concatenate([m.real, m.imag], axis=0)
    half = FFT_N1 // 2
    sig_l = stack(f1[:, :half])
    sig_r = np.concatenate([-f1[:, :half].imag, f1[:, :half].real], axis=0)
    fil_r = stack(f1[:, half:])
    f2 = np.exp(-2j * np.pi * np.outer(n2, n2) / FFT_N2)
    tw = np.exp(-2j * np.pi * np.outer(n1, n2) / FFT_N)
    fwd = f2[None, :, :] * tw[:, None, :]
    inv = np.conj(np.transpose(fwd, (0, 2, 1))) / FFT_N
    block = lambda m: np.concatenate([np.concatenate([m.real, -m.imag], axis=2),
                                      np.concatenate([m.imag, m.real], axis=2)], axis=1)
    g1 = np.conj(f1[:half, :])
    out_l = stack(g1)
    out_r = np.concatenate([-g1.imag, g1.real], axis=0)
    as32 = lambda a: jnp.asarray(a.astype(np.float32))
    return dict(sig_l=as32(sig_l), sig_r=as32(sig_r), fil_r=as32(fil_r), fwd=as32(block(fwd)),
                inv=as32(block(inv)), out_l=as32(out_l), out_r=as32(out_r))


_N2_GROUP = 8


def _dft_a_kernel(*refs, conv):
    if conv:
        u0_ref, u1_ref, cw_ref, cb_ref, ml_ref, mr_ref, o_ref, xs, ysc = refs
    else:
        u0_ref, u1_ref, ml_ref, mr_ref, o_ref, xs, ysc = refs
    half = FFT_N1 // 2
    for r, u_ref in enumerate((u0_ref, u1_ref)):
        for n1 in range(half):
            lo = FFT_N2 * n1
            rows = _dwconv_rows(u_ref, lo, FFT_N2, cw_ref, cb_ref, 1) if conv else u_ref[lo:lo + FFT_N2, :]
            xs[r, PITCH * n1:PITCH * n1 + FFT_N2, :] = rows
    ml = ml_ref[...]
    mr = mr_ref[...]

    def body(g, carry):
        n2 = g * _N2_GROUP
        x0 = jnp.concatenate([xs[0, pl.ds(n2 + i, half, stride=PITCH), :] for i in range(_N2_GROUP)], axis=1)
        x1 = jnp.concatenate([xs[1, pl.ds(n2 + i, half, stride=PITCH), :] for i in range(_N2_GROUP)], axis=1)
        y = _bdot(ml, x0) + _bdot(mr, x1)
        for i in range(_N2_GROUP):
            ysc[pl.ds(n2 + i, 2 * FFT_N1, stride=PITCH), :] = y[:, LANES * i:LANES * (i + 1)]
        return carry

    lax.fori_loop(0, FFT_N2 // _N2_GROUP, body, 0)
    for row in range(2 * FFT_N1):
        o_ref[FFT_N2 * row:FFT_N2 * (row + 1), :] = ysc[PITCH * row:PITCH * row + FFT_N2, :].astype(o_ref.dtype)


def _dft_a(ml, mr, srcs, groups, n_slabs, conv_args=None):
    conv = conv_args is not None
    slab = lambda arr_map: pl.BlockSpec((None, SEQ, LANES), arr_map)
    mspec = pl.BlockSpec((2 * FFT_N1, FFT_N1 // 2), lambda g, s: (0, 0))
    in_specs = [slab(srcs[0][1]), slab(srcs[1][1])]
    args = [srcs[0][0], srcs[1][0]]
    if conv:
        cw, cb, col0 = conv_args
        in_specs += [pl.BlockSpec((cw.shape[0], LANES), lambda g, s: (0, col0 + s)),
                     pl.BlockSpec((1, LANES), lambda g, s: (0, col0 + s))]
        args += [cw, cb]
    rows = 2 * FFT_N1 * FFT_N2
    return pl.pallas_call(
        functools.partial(_dft_a_kernel, conv=conv),
        grid=(groups, n_slabs),
        in_specs=in_specs + [mspec, mspec],
        out_specs=pl.BlockSpec((None, rows, LANES), lambda g, s: (g, 0, s)),
        out_shape=jax.ShapeDtypeStruct((groups, rows, n_slabs * LANES), BF16),
        scratch_shapes=[pltpu.VMEM((2, (FFT_N1 // 2) * PITCH, LANES), F32),
                        pltpu.VMEM((2 * FFT_N1 * PITCH, LANES), F32)],
        compiler_params=_cparams(("parallel", "parallel")),
        name="dft_a",
    )(*args, ml, mr)


def _dft_mid_kernel(f_ref, g_ref, ah_ref, ssq_ref, a_ref, o_ref):
    f = f_ref[...]
    h = _bdot(f, jnp.concatenate([ah_ref[0], ah_ref[1]], axis=0)) * lax.rsqrt(ssq_ref[0:1, :] + EPS)
    hr = h[:FFT_N2]
    hi = h[FFT_N2:]
    for p in range(a_ref.shape[0]):
        y = _bdot(f, jnp.concatenate([a_ref[p, 0], a_ref[p, 1]], axis=0))
        yr = y[:FFT_N2]
        yi = y[FFT_N2:]
        z = jnp.concatenate([yr * hr - yi * hi, yr * hi + yi * hr], axis=0)
        w = _bdot(g_ref[...], z)
        o_ref[p, 0] = w[:FFT_N2].astype(o_ref.dtype)
        o_ref[p, 1] = w[FFT_N2:].astype(o_ref.dtype)


def _dft_mid(fwd, inv, ah5, ssq, layer, order, a5):
    pairs = a5.shape[0]
    blk = pl.BlockSpec((pairs, 2, None, FFT_N2, HY_W), lambda k: (0, 0, k, 0, 0))
    mat = pl.BlockSpec((None, 2 * FFT_N2, 2 * FFT_N2), lambda k: (k, 0, 0))
    return pl.pallas_call(
        _dft_mid_kernel,
        grid=(FFT_N1,),
        in_specs=[mat, mat,
                  pl.BlockSpec((None, 2, None, FFT_N2, HY_W), lambda k: (layer, 0, k, 0, order)),
                  pl.BlockSpec((None, 8, HY_W), lambda k: (layer, 0, order)),
                  blk],
        out_specs=blk,
        out_shape=jax.ShapeDtypeStruct(a5.shape, BF16),
        compiler_params=_cparams(("parallel",)),
        name="dft_mid",
    )(fwd, inv, ah5, ssq, a5)


def _dft_c_kernel(*refs, u_conv):
    if u_conv:
        (b_ref, u0_ref, u1_ref, g0_ref, g1_ref, ucw_ref, ucb_ref, gcw_ref, gcb_ref, ml_ref, mr_ref, skip_ref,
         o_ref, bs, ys) = refs
    else:
        b_ref, u0_ref, u1_ref, g0_ref, g1_ref, gcw_ref, gcb_ref, ml_ref, mr_ref, skip_ref, o_ref, bs, ys = refs
    half = FFT_N1 // 2
    ml = ml_ref[...]
    mr = mr_ref[...]
    im0 = FFT_N1 * PITCH
    for row in range(2 * FFT_N1):
        bs[PITCH * row:PITCH * row + FFT_N2, :] = b_ref[FFT_N2 * row:FFT_N2 * (row + 1), :].astype(F32)

    def body(g, carry):
        n2 = g * _N2_GROUP
        br = jnp.concatenate([bs[pl.ds(n2 + i, FFT_N1, stride=PITCH), :] for i in range(_N2_GROUP)], axis=1)
        bi = jnp.concatenate([bs[pl.ds(im0 + n2 + i, FFT_N1, stride=PITCH), :] for i in range(_N2_GROUP)], axis=1)
        y = _bdot(ml, br) + _bdot(mr, bi)
        for i in range(_N2_GROUP):
            ys[0, pl.ds(n2 + i, half, stride=PITCH), :] = y[:half, LANES * i:LANES * (i + 1)]
            ys[1, pl.ds(n2 + i, half, stride=PITCH), :] = y[half:, LANES * i:LANES * (i + 1)]
        return carry

    lax.fori_loop(0, FFT_N2 // _N2_GROUP, body, 0)
    skip = skip_ref[...]
    for r, (u_ref, g_ref) in enumerate(((u0_ref, g0_ref), (u1_ref, g1_ref))):
        for n1 in range(half):
            lo = FFT_N2 * n1
            u = _dwconv_rows(u_ref, lo, FFT_N2, ucw_ref, ucb_ref, 1) if u_conv else u_ref[lo:lo + FFT_N2, :]
            gate = _dwconv_rows(g_ref, lo, FFT_N2, gcw_ref, gcb_ref, 1)
            conv = ys[r, PITCH * n1:PITCH * n1 + FFT_N2, :]
            o_ref[r, lo:lo + FFT_N2, :] = (gate * (conv + u * skip)).astype(o_ref.dtype)


def _dft_c(ml, mr, b3, u_src, gate_src, u_conv_args, gate_conv_args, skip, out_dtype):
    pairs = b3.shape[0]
    n_slabs = HY_W // LANES
    u_conv = u_conv_args is not None
    slab = lambda m: pl.BlockSpec((None, SEQ, LANES), m)
    wspecs = lambda cw, col0: [pl.BlockSpec((cw.shape[0], LANES), lambda p, s: (0, col0 + s)),
                               pl.BlockSpec((1, LANES), lambda p, s: (0, col0 + s))]
    in_specs = [pl.BlockSpec((None, 2 * FFT_N1 * FFT_N2, LANES), lambda p, s: (p, 0, s)),
                slab(u_src[1]), slab(u_src[2]), slab(gate_src[1]), slab(gate_src[2])]
    args = [b3, u_src[0], u_src[0], gate_src[0], gate_src[0]]
    if u_conv:
        in_specs += wspecs(u_conv_args[0], u_conv_args[2])
        args += [u_conv_args[0], u_conv_args[1]]
    in_specs += wspecs(gate_conv_args[0], gate_conv_args[2])
    args += [gate_conv_args[0], gate_conv_args[1]]
    mspec = pl.BlockSpec((FFT_N1, FFT_N1), lambda p, s: (0, 0))
    in_specs += [mspec, mspec, pl.BlockSpec((1, LANES), lambda p, s: (0, s))]
    args += [ml, mr, skip.reshape(1, HY_W)]
    return pl.pallas_call(
        functools.partial(_dft_c_kernel, u_conv=u_conv),
        grid=(pairs, n_slabs),
        in_specs=in_specs,
        out_specs=pl.BlockSpec((2, SEQ, LANES), lambda p, s: (p, 0, s)),
        out_shape=jax.ShapeDtypeStruct((2 * pairs, SEQ, HY_W), out_dtype),
        scratch_shapes=[pltpu.VMEM((2 * FFT_N1 * PITCH, LANES), F32),
                        pltpu.VMEM((2, (FFT_N1 // 2) * PITCH, LANES), F32)],
        compiler_params=_cparams(("parallel", "parallel")),
        name="dft_c",
    )(*args)


def _mixer_hyena(proj3, conv_w, conv_b, skip, ah5, ssq, layer, consts):
    bsz = proj3.shape[0]
    pairs = bsz // 2
    n_slabs = HY_W // LANES
    cb = conv_b.reshape(1, 3 * HY_W)
    col = lambda which: (COL_CU + which * HY_W) // LANES
    proj_map = lambda which, odd: (lambda p, s: (2 * p + odd, 0, col(which) + s))
    plain_map = lambda odd: (lambda p, s: (2 * p + odd, 0, s))
    conv_args = lambda which: (conv_w, cb, which * n_slabs)
    a5_shape = (pairs, 2, FFT_N1, FFT_N2, HY_W)

    a = _dft_a(consts['sig_l'], consts['sig_r'], [(proj3, proj_map(0, 0)), (proj3, proj_map(0, 1))],
               pairs, n_slabs, conv_args(0))
    b = _dft_mid(consts['fwd'], consts['inv'], ah5, ssq, layer, 0, a.reshape(a5_shape))
    z1 = _dft_c(consts['out_l'], consts['out_r'], b.reshape(a.shape),
                (proj3, proj_map(0, 0), proj_map(0, 1)), (proj3, proj_map(1, 0), proj_map(1, 1)),
                conv_args(0), conv_args(1), skip[0], F32)
    a = _dft_a(consts['sig_l'], consts['sig_r'], [(z1, plain_map(0)), (z1, plain_map(1))], pairs, n_slabs)
    b = _dft_mid(consts['fwd'], consts['inv'], ah5, ssq, layer, 1, a.reshape(a5_shape))
    return _dft_c(consts['out_l'], consts['out_r'], b.reshape(a.shape),
                  (z1, plain_map(0), plain_map(1)), (proj3, proj_map(2, 0), proj_map(2, 1)),
                  None, conv_args(2), skip[1], BF16)


def _hyena_filter_stage(consts, hy_w1, hy_b1, hy_w2, hy_b2, hy_w3, hy_sin_freq, hy_decay):
    hf, hb, ssq = _hy_filters(hy_w1, hy_b1, hy_w2, hy_b2, hy_w3, hy_sin_freq, hy_decay)
    cw = HY_ORDER * HY_W
    fmap = lambda l, s: (l, 0, s)
    ha = _dft_a(consts['sig_l'], consts['fil_r'], [(hf, fmap), (hb, fmap)], DEPTH, cw // LANES)
    return ha.reshape(DEPTH, 2, FFT_N1, FFT_N2, cw), ssq


def _mlstm_chunk(q_ref, k_ref, v_ref, gc_ref, c_scr, m_scr, reverse, i_off, f_off, state_off):
    ch = ML_CHUNK
    ri = lax.broadcasted_iota(jnp.int32, (ch, ch), 0)
    ci = lax.broadcasted_iota(jnp.int32, (ch, ch), 1)
    tri = (ci >= ri) if reverse else (ci <= ri)
    tri_f = tri.astype(F32)
    gc = gc_ref[...]
    gr = gc.T
    b_col = jnp.dot(tri_f, _log_sigmoid(gc), precision=HIGHEST, preferred_element_type=F32)
    b_row = lax.dot_general(_log_sigmoid(gr[:4 * ML_HEADS, :]), tri_f, (((1,), (1,)), ((), ())), precision=HIGHEST,
                            preferred_element_type=F32)
    last = 0 if reverse else ch - 1
    lane = lax.broadcasted_iota(jnp.int32, (ch, ML_HEAD_DIM), 1)
    ones_col = jnp.where(lane == 0, 1.0, 0.0).astype(BF16)

    outs = []
    for h in range(ML_HEADS):
        sl = slice(h * ML_HEAD_DIM, (h + 1) * ML_HEAD_DIM)
        st = state_off + h
        q = (q_ref[:, sl] * (ML_HEAD_DIM ** -0.5)).astype(BF16)
        k = k_ref[:, sl]
        v_aug = jnp.concatenate([v_ref[:, sl].astype(BF16), ones_col], axis=1)
        bc = b_col[:, f_off + h:f_off + h + 1]
        lic = gc[:, i_off + h:i_off + h + 1]
        br = b_row[f_off + h:f_off + h + 1, :]
        lir = gr[i_off + h:i_off + h + 1, :]
        b_tot = bc[last:last + 1, :]
        d = jnp.where(tri, bc - br + lir, -jnp.inf)
        w_end = b_tot - bc + lic
        m_loc = jnp.max(w_end, axis=0, keepdims=True)
        e_end = jnp.exp(w_end - m_loc)
        m_prev = m_scr[st:st + 1, 0:1]
        c_prev = c_scr[st]
        m_inter = bc + m_prev
        m_t = jnp.maximum(m_inter, jnp.max(d, axis=-1, keepdims=True))
        e_inter = jnp.exp(m_inter - m_t)
        qk = lax.dot_general(q, k.astype(BF16), (((1,), (1,)), ((), ())), preferred_element_type=F32)
        s = qk * jnp.exp(d - m_t)
        nd = _bdot(s, v_aug) + e_inter * _bdot(q, c_prev)
        num = nd[:, :ML_HEAD_DIM]
        den = nd[:, ML_HEAD_DIM:ML_HEAD_DIM + 1]
        outs.append(num / jnp.maximum(jnp.abs(den), jnp.exp(-m_t)))
        m_new = jnp.maximum(b_tot + m_prev, m_loc)
        decay = jnp.exp(b_tot + m_prev - m_new)
        gain = jnp.exp(m_loc - m_new)
        dc = lax.dot_general((k * e_end).astype(BF16), v_aug, (((0,), (0,)), ((), ())),
                             preferred_element_type=F32)
        c_scr[st] = decay * c_prev + gain * dc
        m_scr[st:st + 1, :] = jnp.broadcast_to(m_new, (1, LANES))
    return outs


def _mlstm_kernel(qf_ref, kf_ref, vf_ref, gf_ref, qb_ref, kb_ref, vb_ref, gb_ref, hf_ref, hb_ref, c_scr, m_scr):
    @pl.when(pl.program_id(1) == 0)
    def _():
        c_scr[...] = jnp.zeros_like(c_scr)
        m_scr[...] = jnp.zeros_like(m_scr)

    outs_f = _mlstm_chunk(qf_ref, kf_ref, vf_ref, gf_ref, c_scr, m_scr, False, 0, ML_HEADS, 0)
    outs_b = _mlstm_chunk(qb_ref, kb_ref, vb_ref, gb_ref, c_scr, m_scr, True, 2 * ML_HEADS, 3 * ML_HEADS, ML_HEADS)
    hf_ref[...] = jnp.concatenate(outs_f, axis=1)
    hb_ref[...] = jnp.concatenate(outs_b, axis=1)


def _mixer_mlstm(proj2, bsz):
    nc = SEQ // ML_CHUNK

    def specs(chunk_of):
        at = lambda col, width: pl.BlockSpec((pl.Element(ML_CHUNK), pl.Element(width)),
                                             lambda b, j: (pl.multiple_of(b * SEQ + chunk_of(j) * ML_CHUNK, ML_CHUNK),
                                                           col))
        return [at(COL_DQ, GROUP_W), at(COL_DK, GROUP_W), at(COL_DV, GROUP_W), at(COL_GATES, LANES)]

    fwd_of = lambda j: j
    bwd_of = lambda j: nc - 1 - j
    out = lambda chunk_of: pl.BlockSpec((None, ML_CHUNK, GROUP_W), lambda b, j: (b, chunk_of(j), 0))
    return pl.pallas_call(
        _mlstm_kernel,
        grid=(bsz, nc),
        in_specs=specs(fwd_of) + specs(bwd_of),
        out_specs=[out(fwd_of), out(bwd_of)],
        out_shape=[jax.ShapeDtypeStruct((bsz, SEQ, GROUP_W), F32)] * 2,
        scratch_shapes=[pltpu.VMEM((2 * ML_HEADS, ML_HEAD_DIM, 2 * ML_HEAD_DIM), F32), pltpu.VMEM((8, LANES), F32)],
        compiler_params=_cparams(("parallel", "arbitrary")),
        name="mlstm",
    )(*([proj2] * 8))


def _outproj_kernel(lf_ref, lb_ref, ga_ref, yb_ref, yc_ref, mf_ref, mb_ref, o_ref, mg_ref, w_ref, x_ref, m_ref,
                    out_ref):
    ya = (jax.nn.gelu(ga_ref[...]) * (lf_ref[...] + lb_ref[...])).astype(BF16)
    normed = []
    for h in range(ML_HEADS):
        sl = slice(h * ML_HEAD_DIM, (h + 1) * ML_HEAD_DIM)
        hh = mf_ref[:, sl] + mb_ref[:, sl]
        ms = jnp.mean(hh * hh, axis=-1, keepdims=True)
        normed.append(hh * lax.rsqrt(ms + EPS) * mg_ref[:, sl])
    yd = (_sigmoid(o_ref[...]) * jnp.concatenate(normed, axis=1)).astype(BF16)
    acc = jnp.dot(ya, w_ref[0:GROUP_W, :], preferred_element_type=F32)
    acc += jnp.dot(yb_ref[...], w_ref[GROUP_W:2 * GROUP_W, :], preferred_element_type=F32)
    acc += jnp.dot(yc_ref[...], w_ref[2 * GROUP_W:3 * GROUP_W, :], preferred_element_type=F32)
    acc += jnp.dot(yd, w_ref[3 * GROUP_W:, :], preferred_element_type=F32)
    out_ref[...] = x_ref[...] + m_ref[2:3, :] * acc


def _outproj(lru_f, lru_b, proj2, y_b, y_c, ml_f, ml_b, ml_g, w_all, layer, x2, mod_l):
    n = x2.shape[0]
    tm = 512
    per_b = SEQ // tm
    grp = pl.BlockSpec((tm, GROUP_W), lambda i: (i, 0))
    return pl.pallas_call(
        _outproj_kernel,
        grid=(n // tm,),
        in_specs=[grp, grp, pl.BlockSpec((tm, GROUP_W), lambda i: (i, COL_AG // GROUP_W)), grp, grp, grp, grp,
                  pl.BlockSpec((pl.Element(tm), pl.Element(GROUP_W)), lambda i: (pl.multiple_of(i * tm, tm), COL_DO)),
                  pl.BlockSpec((1, GROUP_W), lambda i: (0, 0)),
                  pl.BlockSpec((None, D_MODEL, D_MODEL), lambda i: (layer, 0, 0)),
                  pl.BlockSpec((tm, D_MODEL), lambda i: (i, 0)),
                  pl.BlockSpec((None, 6, D_MODEL), lambda i: (i // per_b, 0, 0))],
        out_specs=pl.BlockSpec((tm, D_MODEL), lambda i: (i, 0)),
        out_shape=jax.ShapeDtypeStruct((n, D_MODEL), F32),
        compiler_params=_cparams(("parallel",)),
        name="out_proj",
    )(lru_f, lru_b, proj2, y_b, y_c, ml_f, ml_b, proj2, ml_g.reshape(1, GROUP_W), w_all, x2, mod_l)


def _ffn_kernel(x_ref, m_ref, g_ref, w1_ref, w3_ref, w2_ref, fg_ref, o_ref, h_scr, *, final):
    j = pl.program_id(1)

    @pl.when(j == 0)
    def _():
        h_scr[...] = _rms_mod(x_ref[...], g_ref[...], m_ref[4:5, :], m_ref[3:4, :]).astype(BF16)
        o_ref[...] = jnp.zeros_like(o_ref)

    h = h_scr[...]
    a = jnp.dot(h, w1_ref[...], preferred_element_type=F32)
    b = jnp.dot(h, w3_ref[...], preferred_element_type=F32)
    act = (a * _sigmoid(a)) * b
    o_ref[...] += jnp.dot(act.astype(BF16), w2_ref[...], preferred_element_type=F32)

    @pl.when(j == pl.num_programs(1) - 1)
    def _():
        y = x_ref[...] + m_ref[5:6, :] * o_ref[...]
        if final:
            ms = jnp.mean(y * y, axis=-1, keepdims=True)
            y = y * lax.rsqrt(ms + EPS) * fg_ref[...]
        o_ref[...] = y


def _ffn(x2, mod_l, g, w1_all, w3_all, w2_all, layer, final_g, final):
    n = x2.shape[0]
    tm, tf = 512, 512
    per_b = SEQ // tm
    row = lambda: pl.BlockSpec((1, D_MODEL), lambda i, j: (0, 0))
    return pl.pallas_call(
        functools.partial(_ffn_kernel, final=final),
        grid=(n // tm, D_FF // tf),
        in_specs=[pl.BlockSpec((tm, D_MODEL), lambda i, j: (i, 0)),
                  pl.BlockSpec((None, 6, D_MODEL), lambda i, j: (i // per_b, 0, 0)),
                  row(),
                  pl.BlockSpec((None, D_MODEL, tf), lambda i, j: (layer, 0, j)),
                  pl.BlockSpec((None, D_MODEL, tf), lambda i, j: (layer, 0, j)),
                  pl.BlockSpec((None, tf, D_MODEL), lambda i, j: (layer, j, 0)),
                  row()],
        out_specs=pl.BlockSpec((tm, D_MODEL), lambda i, j: (i, 0)),
        out_shape=jax.ShapeDtypeStruct((n, D_MODEL), F32),
        scratch_shapes=[pltpu.VMEM((tm, D_MODEL), BF16)],
        compiler_params=_cparams(("parallel", "arbitrary")),
        name="ffn",
    )(x2, mod_l, g.reshape(1, D_MODEL), w1_all, w3_all, w2_all, final_g.reshape(1, D_MODEL))


def kernel(x, c, w_in, b_in, w_out, norm_mix_g, norm_ffn_g, ada_w, ada_b, lru_conv_w, lru_conv_b, lru_wa, lru_ba, lru_wx, lru_bx, lru_lambda, att_q_norm_g, att_k_norm_g, hy_conv_w, hy_conv_b, hy_w1, hy_b1, hy_w2, hy_b2, hy_w3, hy_sin_freq, hy_decay, hy_skip, ml_norm_g, ffn_w1, ffn_w3, ffn_w2, final_g):
    bsz = x.shape[0]
    assert x.shape == (bsz, SEQ, D_MODEL) and bsz % 2 == 0
    n = bsz * SEQ
    mod = _ada_all(c, ada_w, ada_b)

    consts = {k: v.astype(BF16) for k, v in _dft_constants().items()}
    ah5, ssq = _hyena_filter_stage(consts, hy_w1, hy_b1, hy_w2, hy_b2, hy_w3, hy_sin_freq, hy_decay)

    pad = D_IN_PAD - D_IN
    w_in_b = jnp.pad(w_in.astype(BF16), ((0, 0), (0, 0), (0, pad)))
    b_in_p = jnp.pad(b_in, ((0, 0), (0, pad))).reshape(DEPTH, 1, D_IN_PAD)
    w_out_b = w_out.astype(BF16)
    w1_b, w3_b, w2_b = ffn_w1.astype(BF16), ffn_w3.astype(BF16), ffn_w2.astype(BF16)

    x2 = x.reshape(n, D_MODEL)
    for l in range(DEPTH):
        proj2 = _inproj(x2, mod[l], norm_mix_g[l], w_in_b, b_in_p, l)
        proj3 = proj2.reshape(bsz, SEQ, D_IN_PAD)
        lru_f, lru_b = _mixer_rglru(proj3, lru_conv_w[l], lru_conv_b[l], lru_wa[l], lru_ba[l], lru_wx[l], lru_bx[l],
                                    lru_lambda[l])
        y_b = _mixer_attention(proj3, att_q_norm_g[l], att_k_norm_g[l])
        y_c = _mixer_hyena(proj3, hy_conv_w[l], hy_conv_b[l], hy_skip[l], ah5, ssq, l, consts)
        ml_f, ml_b = _mixer_mlstm(proj2, bsz)
        flat = lambda a: a.reshape(n, GROUP_W)
        x2 = _outproj(flat(lru_f), flat(lru_b), proj2, flat(y_b), flat(y_c), flat(ml_f), flat(ml_b), ml_norm_g[l],
                      w_out_b, l, x2, mod[l])
        x2 = _ffn(x2, mod[l], norm_ffn_g[l], w1_b, w3_b, w2_b, l, final_g, final=(l == DEPTH - 1))
    return x2.reshape(bsz, SEQ, D_MODEL)
```

```python
import functools
import math

import numpy as np
import jax
import jax.numpy as jnp
from jax import lax
from jax.experimental import pallas as pl
from jax.experimental.pallas import tpu as pltpu

F32 = jnp.float32
BF16 = jnp.bfloat16
HIGHEST = lax.Precision.HIGHEST

D_MODEL = 2048
SEQ = 4096
DEPTH = 2
GROUP_W = 512
LRU_BLOCKS = 8
LRU_C = 8.0
ATT_HEADS = 8
ATT_KV_HEADS = 2
ATT_GROUP = ATT_HEADS // ATT_KV_HEADS
ATT_HEAD_DIM = 64
ROPE_AXIS = ATT_HEAD_DIM // 2
ROPE_THETA = 10000.0
GRID_W = 64
ATT_HEAD_GROUP = 4
ATT_V_ROWS = 80
LOG2_E = math.log2(math.e)
HY_W = GROUP_W
HY_ORDER = 2
HY_BANDS = 8
HY_EMB = 2 * HY_BANDS + 1
HY_EMB_PAD = 32
HY_FFN = 64
ML_HEADS = 4
ML_HEAD_DIM = 128
ML_CHUNK = 128
D_FF = 5632
EPS = 1e-6
IN_SIZES = (512, 512, 512, 128, 128, 1536, 512, 512, 512, 512, 16)
D_IN = sum(IN_SIZES)
D_IN_PAD = 5632

COL_AX, COL_AG, COL_BQ, COL_BK, COL_BV, COL_CU = 0, 512, 1024, 1536, 1664, 1792
COL_DQ, COL_DK, COL_DV, COL_DO, COL_GATES = 3328, 3840, 4352, 4864, 5376
LANES = 128

FFT_N = 2 * SEQ
FFT_N1 = 64
FFT_N2 = 128
PITCH = 136

VMEM_LIMIT = 56 * 1024 * 1024


def _cparams(sem, vmem=VMEM_LIMIT):
    return pltpu.CompilerParams(dimension_semantics=sem, vmem_limit_bytes=vmem)


def _bdot(a, b):
    return jnp.dot(a.astype(BF16), b.astype(BF16), preferred_element_type=F32)


def _sigmoid(x):
    return 0.5 * jnp.tanh(0.5 * x) + 0.5


def _log_sigmoid(x):
    return jnp.minimum(x, 0.0) - jnp.log1p(jnp.exp(-jnp.abs(x)))


def _softplus(x):
    return jnp.maximum(x, 0.0) + jnp.log1p(jnp.exp(-jnp.abs(x)))


def _ada_kernel(c_ref, w_ref, b_ref, o_ref):
    c = c_ref[...]
    o_ref[...] = _bdot(c * _sigmoid(c), w_ref[...]) + b_ref[...]


def _ada_all(c, ada_w, ada_b):
    bsz = c.shape[0]
    rows = 8
    cp = jnp.zeros((rows, D_MODEL), F32).at[:bsz].set(c)
    tn = 1024
    out = pl.pallas_call(
        _ada_kernel,
        grid=(DEPTH, 6 * D_MODEL // tn),
        in_specs=[pl.BlockSpec((rows, D_MODEL), lambda l, j: (0, 0)),
                  pl.BlockSpec((None, D_MODEL, tn), lambda l, j: (l, 0, j)),
                  pl.BlockSpec((None, 1, tn), lambda l, j: (l, 0, j))],
        out_specs=pl.BlockSpec((None, rows, tn), lambda l, j: (l, 0, j)),
        out_shape=jax.ShapeDtypeStruct((DEPTH, rows, 6 * D_MODEL), F32),
        compiler_params=_cparams(("parallel", "parallel")),
        name="ada_mod",
    )(cp, ada_w, ada_b.reshape(DEPTH, 1, 6 * D_MODEL))
    return out[:, :bsz].reshape(DEPTH, bsz, 6, D_MODEL)


def _rms_mod(x, g, scale, shift):
    ms = jnp.mean(x * x, axis=-1, keepdims=True)
    return (x * lax.rsqrt(ms + EPS) * g) * (1.0 + scale) + shift


def _inproj_kernel(x_ref, m_ref, g_ref, w_ref, b_ref, o_ref, h_scr):
    @pl.when(pl.program_id(1) == 0)
    def _():
        h_scr[...] = _rms_mod(x_ref[...], g_ref[...], m_ref[1:2, :], m_ref[0:1, :]).astype(BF16)

    o_ref[...] = jnp.dot(h_scr[...], w_ref[...], preferred_element_type=F32) + b_ref[...]


def _inproj(x2, mod_l, g, w_all, b_all, layer):
    n = x2.shape[0]
    tm, tn = 1024, 1408
    per_b = SEQ // tm
    return pl.pallas_call(
        _inproj_kernel,
        grid=(n // tm, D_IN_PAD // tn),
        in_specs=[pl.BlockSpec((tm, D_MODEL), lambda i, j: (i, 0)),
                  pl.BlockSpec((None, 6, D_MODEL), lambda i, j: (i // per_b, 0, 0)),
                  pl.BlockSpec((1, D_MODEL), lambda i, j: (0, 0)),
                  pl.BlockSpec((None, D_MODEL, tn), lambda i, j: (layer, 0, j)),
                  pl.BlockSpec((None, 1, tn), lambda i, j: (layer, 0, j))],
        out_specs=pl.BlockSpec((tm, tn), lambda i, j: (i, j)),
        out_shape=jax.ShapeDtypeStruct((n, D_IN_PAD), F32),
        scratch_shapes=[pltpu.VMEM((tm, D_MODEL), BF16)],
        compiler_params=_cparams(("parallel", "arbitrary")),
        name="in_proj",
    )(x2, mod_l, g.reshape(1, D_MODEL), w_all, b_all)


def _shifted(ext, off, rows):
    total = ext.shape[0]
    if off == 0:
        return ext[8:8 + rows]
    return pltpu.roll(ext, (-off) % total, axis=0)[8:8 + rows]


def _dwconv_ext(ext, w_ref, b_ref, left, rows):
    out = b_ref[...]
    for j in range(w_ref.shape[0]):
        out = out + _shifted(ext, j - left, rows) * w_ref[j:j + 1, :]
    return out


def _dwconv_tile(x_ref, p_ref, n_ref, w_ref, b_ref, tile, n_tiles, left):
    prev = jnp.where(tile > 0, p_ref[...], 0.0)
    nxt = jnp.where(tile < n_tiles - 1, n_ref[...], 0.0)
    ext = jnp.concatenate([prev, x_ref[...], nxt], axis=0)
    return _dwconv_ext(ext, w_ref, b_ref, left, x_ref.shape[0])


def _dwconv_rows(ref, lo, rows, w_ref, b_ref, left):
    zeros = jnp.zeros((8, ref.shape[1]), F32)
    prev = ref[lo - 8:lo, :] if lo > 0 else zeros
    nxt = ref[lo + rows:lo + rows + 8, :] if lo + rows < SEQ else zeros
    ext = jnp.concatenate([prev, ref[lo:lo + rows, :], nxt], axis=0)
    return _dwconv_ext(ext, w_ref, b_ref, left, rows)


def _halo_specs(tile_rows, width, col_block, tile_of):
    r8 = tile_rows // 8
    last8 = SEQ // 8 - 1

    def main(b, t):
        return (b, tile_of(t), col_block)

    def prev(b, t):
        return (b, jnp.maximum(tile_of(t) * r8 - 1, 0), col_block)

    def nxt(b, t):
        return (b, jnp.minimum((tile_of(t) + 1) * r8, last8), col_block)

    return [pl.BlockSpec((None, tile_rows, width), main),
            pl.BlockSpec((None, 8, width), prev),
            pl.BlockSpec((None, 8, width), nxt)]


def _lru_gates(xc, wg, bg, lam):
    gates = _bdot(xc, wg) + bg
    r = _sigmoid(gates[:, :GROUP_W])
    i = _sigmoid(gates[:, GROUP_W:])
    log_a = (-LRU_C * _softplus(-lam)) * r
    a = jnp.exp(log_a)
    th = jnp.tanh(log_a)
    u = jnp.sqrt(-2.0 * th / (1.0 - th)) * (i * xc)
    return a, u


def _scan8(a, u, ridx, reverse):
    for k in (1, 2, 4):
        if reverse:
            keep = ridx < 8 - k
            sh = 8 - k
        else:
            keep = ridx >= k
            sh = k
        a_sh = jnp.where(keep, pltpu.roll(a, sh, axis=0), 1.0)
        u_sh = jnp.where(keep, pltpu.roll(u, sh, axis=0), 0.0)
        u = a * u_sh + u
        a = a * a_sh
    return a, u


def _lru_kernel(xf_ref, pf_ref, nf_ref, xb_ref, pb_ref, nb_ref, cw_ref, cb_ref, wg_ref, bg_ref, lam_ref,
                hf_ref, hb_ref, af_scr, uf_scr, ab_scr, ub_scr, cf_scr, cb_scr, c_scr, *, n_tiles):
    t = pl.program_id(1)
    n_chunks = af_scr.shape[0]
    xc = _dwconv_tile(xf_ref, pf_ref, nf_ref, cw_ref, cb_ref, t, n_tiles, 2)
    a, u = _lru_gates(xc, wg_ref[0], bg_ref[0], lam_ref[0])
    af_scr[...] = a.reshape(n_chunks, 8, GROUP_W)
    uf_scr[...] = u.reshape(n_chunks, 8, GROUP_W)
    xc = _dwconv_tile(xb_ref, pb_ref, nb_ref, cw_ref, cb_ref, n_tiles - 1 - t, n_tiles, 2)
    a, u = _lru_gates(xc, wg_ref[1], bg_ref[1], lam_ref[1])
    ab_scr[...] = a.reshape(n_chunks, 8, GROUP_W)
    ub_scr[...] = u.reshape(n_chunks, 8, GROUP_W)

    @pl.when(t == 0)
    def _():
        c_scr[...] = jnp.zeros_like(c_scr)

    ridx = lax.broadcasted_iota(jnp.int32, (8, GROUP_W), 0)

    def local(c, carry):
        a, u = _scan8(af_scr[c], uf_scr[c], ridx, False)
        af_scr[c] = a
        uf_scr[c] = u
        a, u = _scan8(ab_scr[c], ub_scr[c], ridx, True)
        ab_scr[c] = a
        ub_scr[c] = u
        return carry

    lax.fori_loop(0, n_chunks, local, 0, unroll=4)

    def chain(c, carry):
        cf, cb = carry
        cf_scr[c] = jnp.broadcast_to(cf, (8, GROUP_W))
        cf = af_scr[c][7:8, :] * cf + uf_scr[c][7:8, :]
        cr = n_chunks - 1 - c
        cb_scr[cr] = jnp.broadcast_to(cb, (8, GROUP_W))
        cb = ab_scr[cr][0:1, :] * cb + ub_scr[cr][0:1, :]
        return cf, cb

    cf, cb = lax.fori_loop(0, n_chunks, chain, (c_scr[0:1, :], c_scr[1:2, :]), unroll=4)
    c_scr[0:1, :] = cf
    c_scr[1:2, :] = cb

    def apply(c, carry):
        r0 = pl.multiple_of(c * 8, 8)
        hf_ref[pl.ds(r0, 8), :] = uf_scr[c] + af_scr[c] * cf_scr[c]
        hb_ref[pl.ds(r0, 8), :] = ub_scr[c] + ab_scr[c] * cb_scr[c]
        return carry

    lax.fori_loop(0, n_chunks, apply, 0, unroll=4)


def _block_diag(w):
    nb, k, j = w.shape
    eye = jnp.eye(nb, dtype=w.dtype)
    return jnp.einsum('nkj,nm->nkmj', w, eye).reshape(nb * k, nb * j)


def _mixer_rglru(proj3, conv_w, conv_b, wa, ba, wx, bx, lam):
    bsz = proj3.shape[0]
    ts = 512
    n_tiles = SEQ // ts
    wg = jnp.stack([jnp.concatenate([_block_diag(wa[d]), _block_diag(wx[d])], axis=1) for d in range(2)]).astype(BF16)
    bg = jnp.stack([jnp.concatenate([ba[d], bx[d]]).reshape(1, 2 * GROUP_W) for d in range(2)])
    small = lambda *shape: pl.BlockSpec(shape, lambda b, t: (0,) * len(shape))
    fwd_of = lambda t: t
    bwd_of = lambda t: n_tiles - 1 - t
    out = lambda tile_of: pl.BlockSpec((None, ts, GROUP_W), lambda b, t: (b, tile_of(t), 0))
    tile_scr = pltpu.VMEM((ts // 8, 8, GROUP_W), F32)
    return pl.pallas_call(
        functools.partial(_lru_kernel, n_tiles=n_tiles),
        grid=(bsz, n_tiles),
        in_specs=_halo_specs(ts, GROUP_W, COL_AX // GROUP_W, fwd_of) + _halo_specs(ts, GROUP_W, COL_AX // GROUP_W, bwd_of)
        + [small(4, GROUP_W), small(1, GROUP_W), small(2, GROUP_W, 2 * GROUP_W), small(2, 1, 2 * GROUP_W),
           small(2, 1, GROUP_W)],
        out_specs=[out(fwd_of), out(bwd_of)],
        out_shape=[jax.ShapeDtypeStruct((bsz, SEQ, GROUP_W), F32)] * 2,
        scratch_shapes=[tile_scr] * 6 + [pltpu.VMEM((8, GROUP_W), F32)],
        compiler_params=_cparams(("parallel", "arbitrary")),
        name="rglru",
    )(proj3, proj3, proj3, proj3, proj3, proj3, conv_w, conv_b.reshape(1, GROUP_W), wg, bg,
      lam.reshape(2, 1, GROUP_W))


def _split_dot(x, m_ref):
    hi = x.astype(BF16)
    lo = (x - hi.astype(F32)).astype(BF16)
    m = m_ref[...]
    return (jnp.dot(hi, m, preferred_element_type=F32) + jnp.dot(lo, m, preferred_element_type=F32))


def _norm_rope(x, gain, m_ref, cos, sins):
    width = x.shape[1]
    ms = _split_dot(x * x, m_ref)
    xn = x * lax.rsqrt(ms + EPS) * gain
    lane = lax.broadcasted_iota(jnp.int32, xn.shape, 1)
    first = (lane % ROPE_AXIS) < (ROPE_AXIS // 2)
    half = ROPE_AXIS // 2
    partner = jnp.where(first, pltpu.roll(xn, width - half, axis=1), pltpu.roll(xn, half, axis=1))
    return xn * cos + partner * sins


def _attn_prep_kernel(q_ref, kv_ref, gq_ref, gk_ref, mq_ref, mk_ref, cos_ref, sin_ref,
                      qo_ref, kt_ref, vt_ref):
    cos = cos_ref[...]
    sins = sin_ref[...]
    cos_q = jnp.concatenate([cos] * (GROUP_W // LANES), axis=1)
    sin_q = jnp.concatenate([sins] * (GROUP_W // LANES), axis=1)
    q = _norm_rope(q_ref[...], gq_ref[...], mq_ref, cos_q, sin_q)
    qo_ref[...] = (q * (ATT_HEAD_DIM ** -0.5 * LOG2_E)).astype(BF16)
    kv = kv_ref[...]
    kt_ref[...] = _norm_rope(kv[:, :LANES], gk_ref[...], mk_ref, cos, sins).T.astype(BF16)
    v_t = kv[:, LANES:].T
    row = lax.broadcasted_iota(jnp.int32, (ATT_V_ROWS - ATT_HEAD_DIM, v_t.shape[1]), 0)
    ones_rows = jnp.where(row == 0, 1.0, 0.0)
    for g in range(ATT_KV_HEADS):
        vt_ref[g] = jnp.concatenate([v_t[g * ATT_HEAD_DIM:(g + 1) * ATT_HEAD_DIM, :], ones_rows], axis=0).astype(BF16)


def _rope_tables():
    rows = SEQ // GRID_W
    row = jnp.repeat(jnp.arange(rows, dtype=F32), GRID_W)
    col = jnp.tile(jnp.arange(GRID_W, dtype=F32), rows)
    inv = ROPE_THETA ** (-jnp.arange(0, ROPE_AXIS, 2, dtype=F32) / ROPE_AXIS)
    ar = row[:, None] * inv
    ac = col[:, None] * inv
    ang = jnp.concatenate([ar, ar, ac, ac], axis=1)
    sign = jnp.concatenate([-jnp.ones((ROPE_AXIS // 2,), F32), jnp.ones((ROPE_AXIS // 2,), F32)] * 2)
    cos = jnp.tile(jnp.cos(ang), (1, 2))
    sins = jnp.tile(jnp.sin(ang) * sign, (1, 2))
    return cos, sins


def _head_mean_matrix(width):
    idx = np.arange(width) // ATT_HEAD_DIM
    return jnp.asarray((idx[:, None] == idx[None, :]).astype(np.float32) / ATT_HEAD_DIM, dtype=BF16)


def _attn_kernel(q_ref, kt_ref, vt_ref, o_ref):
    def scores(h):
        g = h // ATT_GROUP
        q = q_ref[:, h * ATT_HEAD_DIM:(h + 1) * ATT_HEAD_DIM]
        return jnp.dot(q, kt_ref[g * ATT_HEAD_DIM:(g + 1) * ATT_HEAD_DIM, :], preferred_element_type=F32)

    def probs(s):
        return jnp.exp2(s - jnp.max(s, axis=-1, keepdims=True)).astype(BF16)

    def values(h, p):
        o_aug = lax.dot_general(vt_ref[h // ATT_GROUP], p, (((1,), (1,)), ((), ())), preferred_element_type=F32)
        return (o_aug[:ATT_HEAD_DIM] / o_aug[ATT_HEAD_DIM:ATT_HEAD_DIM + 1]).T

    outs = []
    for h0 in range(0, ATT_HEADS, ATT_HEAD_GROUP):
        heads = range(h0, h0 + ATT_HEAD_GROUP)
        ps = [probs(s) for s in [scores(h) for h in heads]]
        outs += [values(h, p) for h, p in zip(heads, ps)]
    o_ref[...] = jnp.concatenate(outs, axis=1).astype(o_ref.dtype)


def _mixer_attention(proj3, q_g, k_g):
    bsz = proj3.shape[0]
    ts = 512
    cos, sins = _rope_tables()
    gq = jnp.tile(q_g, ATT_HEADS).reshape(1, GROUP_W)
    gk = jnp.tile(k_g, ATT_KV_HEADS).reshape(1, LANES)
    const = lambda shape: pl.BlockSpec(shape, lambda b, t: (0, 0))
    qp, kt, vt = pl.pallas_call(
        _attn_prep_kernel,
        grid=(bsz, SEQ // ts),
        in_specs=[pl.BlockSpec((None, ts, GROUP_W), lambda b, t: (b, t, COL_BQ // GROUP_W)),
                  pl.BlockSpec((None, ts, 2 * LANES), lambda b, t: (b, t, COL_BK // (2 * LANES))),
                  const((1, GROUP_W)), const((1, LANES)), const((GROUP_W, GROUP_W)), const((LANES, LANES)),
                  pl.BlockSpec((ts, LANES), lambda b, t: (t, 0)),
                  pl.BlockSpec((ts, LANES), lambda b, t: (t, 0))],
        out_specs=[pl.BlockSpec((None, ts, GROUP_W), lambda b, t: (b, t, 0)),
                   pl.BlockSpec((None, LANES, ts), lambda b, t: (b, 0, t)),
                   pl.BlockSpec((None, ATT_KV_HEADS, ATT_V_ROWS, ts), lambda b, t: (b, 0, 0, t))],
        out_shape=[jax.ShapeDtypeStruct((bsz, SEQ, GROUP_W), BF16),
                   jax.ShapeDtypeStruct((bsz, LANES, SEQ), BF16),
                   jax.ShapeDtypeStruct((bsz, ATT_KV_HEADS, ATT_V_ROWS, SEQ), BF16)],
        compiler_params=_cparams(("parallel", "parallel")),
        name="attn_prep",
    )(proj3, proj3, gq, gk, _head_mean_matrix(GROUP_W), _head_mean_matrix(LANES), cos, sins)

    tq = 256
    return pl.pallas_call(
        _attn_kernel,
        grid=(bsz, SEQ // tq),
        in_specs=[pl.BlockSpec((None, tq, GROUP_W), lambda b, t: (b, t, 0)),
                  pl.BlockSpec((None, LANES, SEQ), lambda b, t: (b, 0, 0)),
                  pl.BlockSpec((None, ATT_KV_HEADS, ATT_V_ROWS, SEQ), lambda b, t: (b, 0, 0, 0))],
        out_specs=pl.BlockSpec((None, tq, GROUP_W), lambda b, t: (b, t, 0)),
        out_shape=jax.ShapeDtypeStruct((bsz, SEQ, GROUP_W), BF16),
        compiler_params=_cparams(("parallel", "parallel")),
        name="attention",
    )(qp, kt, vt)


def _hy_features():
    L = SEQ
    pos = jnp.arange(L, dtype=F32)
    t = pos / max(L - 1, 1)
    bands = jnp.linspace(1e-4, HY_BANDS - 1, HY_BANDS, dtype=F32)
    ang = (2.0 * math.pi * pos / L)[:, None] * bands
    feat = jnp.concatenate([t[:, None], jnp.cos(ang), -jnp.sin(ang)], axis=-1)
    feat = jnp.pad(feat, ((0, 0), (0, HY_EMB_PAD - HY_EMB)))
    rev_idx = np.concatenate([[0], np.arange(L - 1, 0, -1)])
    return feat, feat[rev_idx]


def _hy_mlp(feat, feat_t, w1t_ref, b1_ref, w2t_ref, b2_ref, w3_ref, sf_ref, dec_ref):
    sf = sf_ref[...]
    h = jnp.sin(sf * (jnp.dot(w1t_ref[...], feat_t, precision=HIGHEST, preferred_element_type=F32) + b1_ref[...]))
    h = jnp.sin(sf * (jnp.dot(w2t_ref[...], h, precision=HIGHEST, preferred_element_type=F32) + b2_ref[...]))
    out = jnp.dot(h.T, w3_ref[...], precision=HIGHEST, preferred_element_type=F32)
    return out * jnp.exp(-feat[:, 0:1] * jnp.abs(dec_ref[...]))


def _hy_filter_kernel(ff_ref, fft_ref, fr_ref, frt_ref, w1t_ref, b1_ref, w2t_ref, b2_ref, w3f_ref, w3b_ref, sf_ref,
                      decf_ref, decb_ref, of_ref, ob_ref, ssq_ref):
    i = pl.program_id(1)
    hf = _hy_mlp(ff_ref[...], fft_ref[...], w1t_ref, b1_ref, w2t_ref, b2_ref, w3f_ref, sf_ref, decf_ref)
    hb = _hy_mlp(fr_ref[...], frt_ref[...], w1t_ref, b1_ref, w2t_ref, b2_ref, w3b_ref, sf_ref, decb_ref)

    @pl.when(i == 0)
    def _():
        ssq_ref[...] = jnp.zeros_like(ssq_ref)

    ssq_ref[0:1, :] += jnp.sum(hf * hf + hb * hb, axis=0, keepdims=True)
    of_ref[...] = hf
    row = lax.broadcasted_iota(jnp.int32, hb.shape, 0)
    ob_ref[...] = jnp.where(jnp.logical_and(i == 0, row == 0), 0.0, hb)


def _hy_filters(hy_w1, hy_b1, hy_w2, hy_b2, hy_w3, hy_sin_freq, hy_decay):
    feat, feat_rev = _hy_features()
    tr = 512
    cw = HY_ORDER * HY_W
    w1t = jnp.transpose(jnp.pad(hy_w1, ((0, 0), (0, HY_EMB_PAD - HY_EMB), (0, 0))), (0, 2, 1))
    w2t = jnp.transpose(hy_w2, (0, 2, 1))
    w3 = hy_w3.reshape(DEPTH, HY_FFN, HY_ORDER, 2, HY_W)
    dec = hy_decay.reshape(DEPTH, HY_ORDER, 2, HY_W)
    w3f = w3[:, :, :, 0].reshape(DEPTH, HY_FFN, cw)
    w3b = w3[:, :, :, 1].reshape(DEPTH, HY_FFN, cw)
    decf = dec[:, :, 0].reshape(DEPTH, 1, cw)
    decb = dec[:, :, 1].reshape(DEPTH, 1, cw)
    col = lambda a: a.reshape(DEPTH, HY_FFN, 1)
    row_spec = pl.BlockSpec((tr, HY_EMB_PAD), lambda l, i: (i, 0))
    rowt_spec = pl.BlockSpec((HY_EMB_PAD, tr), lambda l, i: (0, i))
    per_layer = lambda a, b: pl.BlockSpec((None, a, b), lambda l, i: (l, 0, 0))
    out_spec = pl.BlockSpec((None, tr, cw), lambda l, i: (l, i, 0))
    return pl.pallas_call(
        _hy_filter_kernel,
        grid=(DEPTH, SEQ // tr),
        in_specs=[row_spec, rowt_spec, row_spec, rowt_spec, per_layer(HY_FFN, HY_EMB_PAD), per_layer(HY_FFN, 1),
                  per_layer(HY_FFN, HY_FFN), per_layer(HY_FFN, 1), per_layer(HY_FFN, cw), per_layer(HY_FFN, cw),
                  per_layer(HY_FFN, 1), per_layer(1, cw), per_layer(1, cw)],
        out_specs=[out_spec, out_spec, per_layer(8, cw)],
        out_shape=[jax.ShapeDtypeStruct((DEPTH, SEQ, cw), F32)] * 2 + [jax.ShapeDtypeStruct((DEPTH, 8, cw), F32)],
        compiler_params=_cparams(("arbitrary", "arbitrary")),
        name="hyena_filter",
    )(feat, feat.T, feat_rev, feat_rev.T, w1t, col(hy_b1), w2t, col(hy_b2), w3f, w3b, col(hy_sin_freq), decf, decb)


def _dft_constants():
    n1 = np.arange(FFT_N1)
    n2 = np.arange(FFT_N2)
    f1 = np.exp(-2j * np.pi * np.outer(n1, n1) / FFT_N1)
    stack = lambda m: np.concatenate([m.real, m.imag], axis=0)
    half = FFT_N1 // 2
    sig_l = stack(f1[:, :half])
    sig_r = np.concatenate([-f1[:, :half].imag, f1[:, :half].real], axis=0)
    fil_r = stack(f1[:, half:])
    f2 = np.exp(-2j * np.pi * np.outer(n2, n2) / FFT_N2)
    tw = np.exp(-2j * np.pi * np.outer(n1, n2) / FFT_N)
    fwd = f2[None, :, :] * tw[:, None, :]
    inv = np.conj(np.transpose(fwd, (0, 2, 1))) / FFT_N
    block = lambda m: np.concatenate([np.concatenate([m.real, -m.imag], axis=2),
                                      np.concatenate([m.imag, m.real], axis=2)], axis=1)
    g1 = np.conj(f1[:half, :])
    out_l = stack(g1)
    out_r = np.concatenate([-g1.imag, g1.real], axis=0)
    as32 = lambda a: jnp.asarray(a.astype(np.float32))
    return dict(sig_l=as32(sig_l), sig_r=as32(sig_r), fil_r=as32(fil_r), fwd=as32(block(fwd)),
                inv=as32(block(inv)), out_l=as32(out_l), out_r=as32(out_r))


_N2_GROUP = 8


def _dft_a_kernel(*refs, conv):
    if conv:
        u0_ref, u1_ref, cw_ref, cb_ref, ml_ref, mr_ref, o_ref, xs, ysc = refs
    else:
        u0_ref, u1_ref, ml_ref, mr_ref, o_ref, xs, ysc = refs
    half = FFT_N1 // 2
    for r, u_ref in enumerate((u0_ref, u1_ref)):
        for n1 in range(half):
            lo = FFT_N2 * n1
            rows = _dwconv_rows(u_ref, lo, FFT_N2, cw_ref, cb_ref, 1) if conv else u_ref[lo:lo + FFT_N2, :]
            xs[r, PITCH * n1:PITCH * n1 + FFT_N2, :] = rows
    ml = ml_ref[...]
    mr = mr_ref[...]

    def body(g, carry):
        n2 = g * _N2_GROUP
        x0 = jnp.concatenate([xs[0, pl.ds(n2 + i, half, stride=PITCH), :] for i in range(_N2_GROUP)], axis=1)
        x1 = jnp.concatenate([xs[1, pl.ds(n2 + i, half, stride=PITCH), :] for i in range(_N2_GROUP)], axis=1)
        y = _bdot(ml, x0) + _bdot(mr, x1)
        for i in range(_N2_GROUP):
            ysc[pl.ds(n2 + i, 2 * FFT_N1, stride=PITCH), :] = y[:, LANES * i:LANES * (i + 1)]
        return carry

    lax.fori_loop(0, FFT_N2 // _N2_GROUP, body, 0)
    for row in range(2 * FFT_N1):
        o_ref[FFT_N2 * row:FFT_N2 * (row + 1), :] = ysc[PITCH * row:PITCH * row + FFT_N2, :].astype(o_ref.dtype)


def _dft_a(ml, mr, srcs, groups, n_slabs, conv_args=None):
    conv = conv_args is not None
    slab = lambda arr_map: pl.BlockSpec((None, SEQ, LANES), arr_map)
    mspec = pl.BlockSpec((2 * FFT_N1, FFT_N1 // 2), lambda g, s: (0, 0))
    in_specs = [slab(srcs[0][1]), slab(srcs[1][1])]
    args = [srcs[0][0], srcs[1][0]]
    if conv:
        cw, cb, col0 = conv_args
        in_specs += [pl.BlockSpec((cw.shape[0], LANES), lambda g, s: (0, col0 + s)),
                     pl.BlockSpec((1, LANES), lambda g, s: (0, col0 + s))]
        args += [cw, cb]
    rows = 2 * FFT_N1 * FFT_N2
    return pl.pallas_call(
        functools.partial(_dft_a_kernel, conv=conv),
        grid=(groups, n_slabs),
        in_specs=in_specs + [mspec, mspec],
        out_specs=pl.BlockSpec((None, rows, LANES), lambda g, s: (g, 0, s)),
        out_shape=jax.ShapeDtypeStruct((groups, rows, n_slabs * LANES), BF16),
        scratch_shapes=[pltpu.VMEM((2, (FFT_N1 // 2) * PITCH, LANES), F32),
                        pltpu.VMEM((2 * FFT_N1 * PITCH, LANES), F32)],
        compiler_params=_cparams(("parallel", "parallel")),
        name="dft_a",
    )(*args, ml, mr)


def _dft_mid_kernel(f_ref, g_ref, ah_ref, ssq_ref, a_ref, o_ref):
    scale = lax.rsqrt(ssq_ref[0:1, :] + EPS)
    for kk in range(f_ref.shape[0]):
        f = f_ref[kk]
        g = g_ref[kk]
        h = _bdot(f, jnp.concatenate([ah_ref[0, kk], ah_ref[1, kk]], axis=0)) * scale
        hr = h[:FFT_N2]
        hi = h[FFT_N2:]
        for p in range(a_ref.shape[0]):
            y = _bdot(f, jnp.concatenate([a_ref[p, 0, kk], a_ref[p, 1, kk]], axis=0))
            yr = y[:FFT_N2]
            yi = y[FFT_N2:]
            z = jnp.concatenate([yr * hr - yi * hi, yr * hi + yi * hr], axis=0)
            w = _bdot(g, z)
            o_ref[p, 0, kk] = w[:FFT_N2].astype(o_ref.dtype)
            o_ref[p, 1, kk] = w[FFT_N2:].astype(o_ref.dtype)


def _dft_mid(fwd, inv, ah5, ssq, layer, order, a5):
    pairs = a5.shape[0]
    kb = 2
    blk = pl.BlockSpec((pairs, 2, kb, FFT_N2, HY_W), lambda k: (0, 0, k, 0, 0))
    mat = pl.BlockSpec((kb, 2 * FFT_N2, 2 * FFT_N2), lambda k: (k, 0, 0))
    return pl.pallas_call(
        _dft_mid_kernel,
        grid=(FFT_N1 // kb,),
        in_specs=[mat, mat,
                  pl.BlockSpec((None, 2, kb, FFT_N2, HY_W), lambda k: (layer, 0, k, 0, order)),
                  pl.BlockSpec((None, 8, HY_W), lambda k: (layer, 0, order)),
                  blk],
        out_specs=blk,
        out_shape=jax.ShapeDtypeStruct(a5.shape, BF16),
        compiler_params=_cparams(("parallel",)),
        name="dft_mid",
    )(fwd, inv, ah5, ssq, a5)


def _dft_c_kernel(*refs, u_conv):
    if u_conv:
        (b_ref, u0_ref, u1_ref, g0_ref, g1_ref, ucw_ref, ucb_ref, gcw_ref, gcb_ref, ml_ref, mr_ref, skip_ref,
         o_ref, bs, ys) = refs
    else:
        b_ref, u0_ref, u1_ref, g0_ref, g1_ref, gcw_ref, gcb_ref, ml_ref, mr_ref, skip_ref, o_ref, bs, ys = refs
    half = FFT_N1 // 2
    ml = ml_ref[...]
    mr = mr_ref[...]
    im0 = FFT_N1 * PITCH
    for row in range(2 * FFT_N1):
        bs[PITCH * row:PITCH * row + FFT_N2, :] = b_ref[FFT_N2 * row:FFT_N2 * (row + 1), :].astype(F32)

    def body(g, carry):
        n2 = g * _N2_GROUP
        br = jnp.concatenate([bs[pl.ds(n2 + i, FFT_N1, stride=PITCH), :] for i in range(_N2_GROUP)], axis=1)
        bi = jnp.concatenate([bs[pl.ds(im0 + n2 + i, FFT_N1, stride=PITCH), :] for i in range(_N2_GROUP)], axis=1)
        y = _bdot(ml, br) + _bdot(mr, bi)
        for i in range(_N2_GROUP):
            ys[0, pl.ds(n2 + i, half, stride=PITCH), :] = y[:half, LANES * i:LANES * (i + 1)]
            ys[1, pl.ds(n2 + i, half, stride=PITCH), :] = y[half:, LANES * i:LANES * (i + 1)]
        return carry

    lax.fori_loop(0, FFT_N2 // _N2_GROUP, body, 0)
    skip = skip_ref[...]
    for r, (u_ref, g_ref) in enumerate(((u0_ref, g0_ref), (u1_ref, g1_ref))):
        for n1 in range(half):
            lo = FFT_N2 * n1
            u = _dwconv_rows(u_ref, lo, FFT_N2, ucw_ref, ucb_ref, 1) if u_conv else u_ref[lo:lo + FFT_N2, :]
            gate = _dwconv_rows(g_ref, lo, FFT_N2, gcw_ref, gcb_ref, 1)
            conv = ys[r, PITCH * n1:PITCH * n1 + FFT_N2, :]
            o_ref[r, lo:lo + FFT_N2, :] = (gate * (conv + u * skip)).astype(o_ref.dtype)


def _dft_c(ml, mr, b3, u_src, gate_src, u_conv_args, gate_conv_args, skip, out_dtype):
    pairs = b3.shape[0]
    n_slabs = HY_W // LANES
    u_conv = u_conv_args is not None
    slab = lambda m: pl.BlockSpec((None, SEQ, LANES), m)
    wspecs = lambda cw, col0: [pl.BlockSpec((cw.shape[0], LANES), lambda p, s: (0, col0 + s)),
                               pl.BlockSpec((1, LANES), lambda p, s: (0, col0 + s))]
    in_specs = [pl.BlockSpec((None, 2 * FFT_N1 * FFT_N2, LANES), lambda p, s: (p, 0, s)),
                slab(u_src[1]), slab(u_src[2]), slab(gate_src[1]), slab(gate_src[2])]
    args = [b3, u_src[0], u_src[0], gate_src[0], gate_src[0]]
    if u_conv:
        in_specs += wspecs(u_conv_args[0], u_conv_args[2])
        args += [u_conv_args[0], u_conv_args[1]]
    in_specs += wspecs(gate_conv_args[0], gate_conv_args[2])
    args += [gate_conv_args[0], gate_conv_args[1]]
    mspec = pl.BlockSpec((FFT_N1, FFT_N1), lambda p, s: (0, 0))
    in_specs += [mspec, mspec, pl.BlockSpec((1, LANES), lambda p, s: (0, s))]
    args += [ml, mr, skip.reshape(1, HY_W)]
    return pl.pallas_call(
        functools.partial(_dft_c_kernel, u_conv=u_conv),
        grid=(pairs, n_slabs),
        in_specs=in_specs,
        out_specs=pl.BlockSpec((2, SEQ, LANES), lambda p, s: (p, 0, s)),
        out_shape=jax.ShapeDtypeStruct((2 * pairs, SEQ, HY_W), out_dtype),
        scratch_shapes=[pltpu.VMEM((2 * FFT_N1 * PITCH, LANES), F32),
                        pltpu.VMEM((2, (FFT_N1 // 2) * PITCH, LANES), F32)],
        compiler_params=_cparams(("parallel", "parallel")),
        name="dft_c",
    )(*args)


def _mixer_hyena(proj3, conv_w, conv_b, skip, ah5, ssq, layer, consts):
    bsz = proj3.shape[0]
    pairs = bsz // 2
    n_slabs = HY_W // LANES
    cb = conv_b.reshape(1, 3 * HY_W)
    col = lambda which: (COL_CU + which * HY_W) // LANES
    proj_map = lambda which, odd: (lambda p, s: (2 * p + odd, 0, col(which) + s))
    plain_map = lambda odd: (lambda p, s: (2 * p + odd, 0, s))
    conv_args = lambda which: (conv_w, cb, which * n_slabs)
    a5_shape = (pairs, 2, FFT_N1, FFT_N2, HY_W)

    a = _dft_a(consts['sig_l'], consts['sig_r'], [(proj3, proj_map(0, 0)), (proj3, proj_map(0, 1))],
               pairs, n_slabs, conv_args(0))
    b = _dft_mid(consts['fwd'], consts['inv'], ah5, ssq, layer, 0, a.reshape(a5_shape))
    z1 = _dft_c(consts['out_l'], consts['out_r'], b.reshape(a.shape),
                (proj3, proj_map(0, 0), proj_map(0, 1)), (proj3, proj_map(1, 0), proj_map(1, 1)),
                conv_args(0), conv_args(1), skip[0], F32)
    a = _dft_a(consts['sig_l'], consts['sig_r'], [(z1, plain_map(0)), (z1, plain_map(1))], pairs, n_slabs)
    b = _dft_mid(consts['fwd'], consts['inv'], ah5, ssq, layer, 1, a.reshape(a5_shape))
    return _dft_c(consts['out_l'], consts['out_r'], b.reshape(a.shape),
                  (z1, plain_map(0), plain_map(1)), (proj3, proj_map(2, 0), proj_map(2, 1)),
                  None, conv_args(2), skip[1], BF16)


def _hyena_filter_stage(consts, hy_w1, hy_b1, hy_w2, hy_b2, hy_w3, hy_sin_freq, hy_decay):
    hf, hb, ssq = _hy_filters(hy_w1, hy_b1, hy_w2, hy_b2, hy_w3, hy_sin_freq, hy_decay)
    cw = HY_ORDER * HY_W
    fmap = lambda l, s: (l, 0, s)
    ha = _dft_a(consts['sig_l'], consts['fil_r'], [(hf, fmap), (hb, fmap)], DEPTH, cw // LANES)
    return ha.reshape(DEPTH, 2, FFT_N1, FFT_N2, cw), ssq


def _mlstm_chunk(q_ref, k_ref, v_ref, gc_ref, c_scr, m_scr, reverse, i_off, f_off, state_off):
    ch = ML_CHUNK
    ri = lax.broadcasted_iota(jnp.int32, (ch, ch), 0)
    ci = lax.broadcasted_iota(jnp.int32, (ch, ch), 1)
    tri = (ci >= ri) if reverse else (ci <= ri)
    tri_f = tri.astype(F32)
    gc = gc_ref[...]
    gr = gc.T
    b_col = jnp.dot(tri_f, _log_sigmoid(gc), precision=HIGHEST, preferred_element_type=F32)
    b_row = lax.dot_general(_log_sigmoid(gr[:4 * ML_HEADS, :]), tri_f, (((1,), (1,)), ((), ())), precision=HIGHEST,
                            preferred_element_type=F32)
    last = 0 if reverse else ch - 1
    lane = lax.broadcasted_iota(jnp.int32, (ch, ML_HEAD_DIM), 1)
    ones_col = jnp.where(lane == 0, 1.0, 0.0).astype(BF16)

    outs = []
    for h in range(ML_HEADS):
        sl = slice(h * ML_HEAD_DIM, (h + 1) * ML_HEAD_DIM)
        st = state_off + h
        q = (q_ref[:, sl] * (ML_HEAD_DIM ** -0.5)).astype(BF16)
        k = k_ref[:, sl]
        v_aug = jnp.concatenate([v_ref[:, sl].astype(BF16), ones_col], axis=1)
        bc = b_col[:, f_off + h:f_off + h + 1]
        lic = gc[:, i_off + h:i_off + h + 1]
        br = b_row[f_off + h:f_off + h + 1, :]
        lir = gr[i_off + h:i_off + h + 1, :]
        b_tot = bc[last:last + 1, :]
        d = jnp.where(tri, bc - br + lir, -jnp.inf)
        w_end = b_tot - bc + lic
        m_loc = jnp.max(w_end, axis=0, keepdims=True)
        e_end = jnp.exp(w_end - m_loc)
        m_prev = m_scr[st:st + 1, 0:1]
        c_prev = c_scr[st]
        m_inter = bc + m_prev
        m_t = jnp.maximum(m_inter, jnp.max(d, axis=-1, keepdims=True))
        e_inter = jnp.exp(m_inter - m_t)
        qk = lax.dot_general(q, k.astype(BF16), (((1,), (1,)), ((), ())), preferred_element_type=F32)
        s = qk * jnp.exp(d - m_t)
        nd = _bdot(s, v_aug) + e_inter * _bdot(q, c_prev)
        num = nd[:, :ML_HEAD_DIM]
        den = nd[:, ML_HEAD_DIM:ML_HEAD_DIM + 1]
        outs.append(num / jnp.maximum(jnp.abs(den), jnp.exp(-m_t)))
        m_new = jnp.maximum(b_tot + m_prev, m_loc)
        decay = jnp.exp(b_tot + m_prev - m_new)
        gain = jnp.exp(m_loc - m_new)
        dc = lax.dot_general((k * e_end).astype(BF16), v_aug, (((0,), (0,)), ((), ())),
                             preferred_element_type=F32)
        c_scr[st] = decay * c_prev + gain * dc
        m_scr[st:st + 1, :] = jnp.broadcast_to(m_new, (1, LANES))
    return outs


ML_BATCH_ROWS = 2


def _mlstm_kernel(*refs):
    n_in = 8 * ML_BATCH_ROWS
    hf_ref, hb_ref, c_scr, m_scr = refs[n_in:]

    @pl.when(pl.program_id(1) == 0)
    def _():
        c_scr[...] = jnp.zeros_like(c_scr)
        m_scr[...] = jnp.zeros_like(m_scr)

    for r in range(ML_BATCH_ROWS):
        qf_ref, kf_ref, vf_ref, gf_ref, qb_ref, kb_ref, vb_ref, gb_ref = refs[8 * r:8 * r + 8]
        st = 2 * ML_HEADS * r
        outs_f = _mlstm_chunk(qf_ref, kf_ref, vf_ref, gf_ref, c_scr, m_scr, False, 0, ML_HEADS, st)
        outs_b = _mlstm_chunk(qb_ref, kb_ref, vb_ref, gb_ref, c_scr, m_scr, True, 2 * ML_HEADS, 3 * ML_HEADS,
                              st + ML_HEADS)
        hf_ref[r] = jnp.concatenate(outs_f, axis=1)
        hb_ref[r] = jnp.concatenate(outs_b, axis=1)


def _mixer_mlstm(proj2, bsz):
    nc = SEQ // ML_CHUNK
    rows = ML_BATCH_ROWS

    def specs(r, chunk_of):
        def at(col, width):
            def index(i, j):
                return pl.multiple_of((i * rows + r) * SEQ + chunk_of(j) * ML_CHUNK, ML_CHUNK), col
            return pl.BlockSpec((pl.Element(ML_CHUNK), pl.Element(width)), index)
        return [at(COL_DQ, GROUP_W), at(COL_DK, GROUP_W), at(COL_DV, GROUP_W), at(COL_GATES, LANES)]

    fwd_of = lambda j: j
    bwd_of = lambda j: nc - 1 - j
    in_specs = []
    for r in range(rows):
        in_specs += specs(r, fwd_of) + specs(r, bwd_of)
    out = lambda chunk_of: pl.BlockSpec((rows, ML_CHUNK, GROUP_W), lambda i, j: (i, chunk_of(j), 0))
    n_state = 2 * ML_HEADS * rows
    return pl.pallas_call(
        _mlstm_kernel,
        grid=(bsz // rows, nc),
        in_specs=in_specs,
        out_specs=[out(fwd_of), out(bwd_of)],
        out_shape=[jax.ShapeDtypeStruct((bsz, SEQ, GROUP_W), F32)] * 2,
        scratch_shapes=[pltpu.VMEM((n_state, ML_HEAD_DIM, 2 * ML_HEAD_DIM), F32), pltpu.VMEM((n_state, LANES), F32)],
        compiler_params=_cparams(("parallel", "arbitrary")),
        name="mlstm",
    )(*([proj2] * (8 * rows)))


def _outproj_kernel(lf_ref, lb_ref, ga_ref, yb_ref, yc_ref, mf_ref, mb_ref, o_ref, mg_ref, w_ref, x_ref, m_ref,
                    out_ref):
    ya = (jax.nn.gelu(ga_ref[...]) * (lf_ref[...] + lb_ref[...])).astype(BF16)
    normed = []
    for h in range(ML_HEADS):
        sl = slice(h * ML_HEAD_DIM, (h + 1) * ML_HEAD_DIM)
        hh = mf_ref[:, sl] + mb_ref[:, sl]
        ms = jnp.mean(hh * hh, axis=-1, keepdims=True)
        normed.append(hh * lax.rsqrt(ms + EPS) * mg_ref[:, sl])
    yd = (_sigmoid(o_ref[...]) * jnp.concatenate(normed, axis=1)).astype(BF16)
    acc = jnp.dot(ya, w_ref[0:GROUP_W, :], preferred_element_type=F32)
    acc += jnp.dot(yb_ref[...], w_ref[GROUP_W:2 * GROUP_W, :], preferred_element_type=F32)
    acc += jnp.dot(yc_ref[...], w_ref[2 * GROUP_W:3 * GROUP_W, :], preferred_element_type=F32)
    acc += jnp.dot(yd, w_ref[3 * GROUP_W:, :], preferred_element_type=F32)
    out_ref[...] = x_ref[...] + m_ref[2:3, :] * acc


def _outproj(lru_f, lru_b, proj2, y_b, y_c, ml_f, ml_b, ml_g, w_all, layer, x2, mod_l):
    n = x2.shape[0]
    tm = 512
    per_b = SEQ // tm
    grp = pl.BlockSpec((tm, GROUP_W), lambda i: (i, 0))
    return pl.pallas_call(
        _outproj_kernel,
        grid=(n // tm,),
        in_specs=[grp, grp, pl.BlockSpec((tm, GROUP_W), lambda i: (i, COL_AG // GROUP_W)), grp, grp, grp, grp,
                  pl.BlockSpec((pl.Element(tm), pl.Element(GROUP_W)), lambda i: (pl.multiple_of(i * tm, tm), COL_DO)),
                  pl.BlockSpec((1, GROUP_W), lambda i: (0, 0)),
                  pl.BlockSpec((None, D_MODEL, D_MODEL), lambda i: (layer, 0, 0)),
                  pl.BlockSpec((tm, D_MODEL), lambda i: (i, 0)),
                  pl.BlockSpec((None, 6, D_MODEL), lambda i: (i // per_b, 0, 0))],
        out_specs=pl.BlockSpec((tm, D_MODEL), lambda i: (i, 0)),
        out_shape=jax.ShapeDtypeStruct((n, D_MODEL), F32),
        compiler_params=_cparams(("parallel",)),
        name="out_proj",
    )(lru_f, lru_b, proj2, y_b, y_c, ml_f, ml_b, proj2, ml_g.reshape(1, GROUP_W), w_all, x2, mod_l)


def _ffn_kernel(x_ref, m_ref, g_ref, w1_ref, w3_ref, w2_ref, fg_ref, o_ref, h_scr, *, final):
    j = pl.program_id(1)

    @pl.when(j == 0)
    def _():
        h_scr[...] = _rms_mod(x_ref[...], g_ref[...], m_ref[4:5, :], m_ref[3:4, :]).astype(BF16)
        o_ref[...] = jnp.zeros_like(o_ref)

    h = h_scr[...]
    a = jnp.dot(h, w1_ref[...], preferred_element_type=F32)
    b = jnp.dot(h, w3_ref[...], preferred_element_type=F32)
    act = (a * _sigmoid(a)) * b
    o_ref[...] += jnp.dot(act.astype(BF16), w2_ref[...], preferred_element_type=F32)

    @pl.when(j == pl.num_programs(1) - 1)
    def _():
        y = x_ref[...] + m_ref[5:6, :] * o_ref[...]
        if final:
            ms = jnp.mean(y * y, axis=-1, keepdims=True)
            y = y * lax.rsqrt(ms + EPS) * fg_ref[...]
        o_ref[...] = y


def _ffn(x2, mod_l, g, w1_all, w3_all, w2_all, layer, final_g, final):
    n = x2.shape[0]
    tm, tf = 512, 512
    per_b = SEQ // tm
    row = lambda: pl.BlockSpec((1, D_MODEL), lambda i, j: (0, 0))
    return pl.pallas_call(
        functools.partial(_ffn_kernel, final=final),
        grid=(n // tm, D_FF // tf),
        in_specs=[pl.BlockSpec((tm, D_MODEL), lambda i, j: (i, 0)),
                  pl.BlockSpec((None, 6, D_MODEL), lambda i, j: (i // per_b, 0, 0)),
                  row(),
                  pl.BlockSpec((None, D_MODEL, tf), lambda i, j: (layer, 0, j)),
                  pl.BlockSpec((None, D_MODEL, tf), lambda i, j: (layer, 0, j)),
                  pl.BlockSpec((None, tf, D_MODEL), lambda i, j: (layer, j, 0)),
                  row()],
        out_specs=pl.BlockSpec((tm, D_MODEL), lambda i, j: (i, 0)),
        out_shape=jax.ShapeDtypeStruct((n, D_MODEL), F32),
        scratch_shapes=[pltpu.VMEM((tm, D_MODEL), BF16)],
        compiler_params=_cparams(("parallel", "arbitrary")),
        name="ffn",
    )(x2, mod_l, g.reshape(1, D_MODEL), w1_all, w3_all, w2_all, final_g.reshape(1, D_MODEL))


def kernel(x, c, w_in, b_in, w_out, norm_mix_g, norm_ffn_g, ada_w, ada_b, lru_conv_w, lru_conv_b, lru_wa, lru_ba, lru_wx, lru_bx, lru_lambda, att_q_norm_g, att_k_norm_g, hy_conv_w, hy_conv_b, hy_w1, hy_b1, hy_w2, hy_b2, hy_w3, hy_sin_freq, hy_decay, hy_skip, ml_norm_g, ffn_w1, ffn_w3, ffn_w2, final_g):
    bsz = x.shape[0]
    assert x.shape == (bsz, SEQ, D_MODEL) and bsz % 2 == 0
    n = bsz * SEQ
    mod = _ada_all(c, ada_w, ada_b)

    consts = {k: v.astype(BF16) for k, v in _dft_constants().items()}
    ah5, ssq = _hyena_filter_stage(consts, hy_w1, hy_b1, hy_w2, hy_b2, hy_w3, hy_sin_freq, hy_decay)

    pad = D_IN_PAD - D_IN
    w_in_b = jnp.pad(w_in.astype(BF16), ((0, 0), (0, 0), (0, pad)))
    b_in_p = jnp.pad(b_in, ((0, 0), (0, pad))).reshape(DEPTH, 1, D_IN_PAD)
    w_out_b = w_out.astype(BF16)
    w1_b, w3_b, w2_b = ffn_w1.astype(BF16), ffn_w3.astype(BF16), ffn_w2.astype(BF16)

    x2 = x.reshape(n, D_MODEL)
    for l in range(DEPTH):
        proj2 = _inproj(x2, mod[l], norm_mix_g[l], w_in_b, b_in_p, l)
        proj3 = proj2.reshape(bsz, SEQ, D_IN_PAD)
        lru_f, lru_b = _mixer_rglru(proj3, lru_conv_w[l], lru_conv_b[l], lru_wa[l], lru_ba[l], lru_wx[l], lru_bx[l],
                                    lru_lambda[l])
        y_b = _mixer_attention(proj3, att_q_norm_g[l], att_k_norm_g[l])
        y_c = _mixer_hyena(proj3, hy_conv_w[l], hy_conv_b[l], hy_skip[l], ah5, ssq, l, consts)
        ml_f, ml_b = _mixer_mlstm(proj2, bsz)
        flat = lambda a: a.reshape(n, GROUP_W)
        x2 = _outproj(flat(lru_f), flat(lru_b), proj2, flat(y_b), flat(y_c), flat(ml_f), flat(ml_b), ml_norm_g[l],
                      w_out_b, l, x2, mod[l])
        x2 = _ffn(x2, mod[l], norm_ffn_g[l], w1_b, w3_b, w2_b, l, final_g, final=(l == DEPTH - 1))
    return x2.reshape(bsz, SEQ, D_MODEL)
```

```python
import functools
import math

import numpy as np
import jax
import jax.numpy as jnp
from jax import lax
from jax.experimental import pallas as pl
from jax.experimental.pallas import tpu as pltpu

F32 = jnp.float32
BF16 = jnp.bfloat16
HIGHEST = lax.Precision.HIGHEST

D_MODEL = 2048
SEQ = 4096
DEPTH = 2
GROUP_W = 512
LRU_BLOCKS = 8
LRU_C = 8.0
ATT_HEADS = 8
ATT_KV_HEADS = 2
ATT_GROUP = ATT_HEADS // ATT_KV_HEADS
ATT_HEAD_DIM = 64
ROPE_AXIS = ATT_HEAD_DIM // 2
ROPE_THETA = 10000.0
GRID_W = 64
ATT_HEAD_GROUP = 2
ATT_V_ROWS = 80
LOG2_E = math.log2(math.e)
HY_W = GROUP_W
HY_ORDER = 2
HY_BANDS = 8
HY_EMB = 2 * HY_BANDS + 1
HY_EMB_PAD = 32
HY_FFN = 64
ML_HEADS = 4
ML_HEAD_DIM = 128
ML_CHUNK = 128
D_FF = 5632
FFN_NORM_ROWS = 64
EPS = 1e-6
IN_SIZES = (512, 512, 512, 128, 128, 1536, 512, 512, 512, 512, 16)
D_IN = sum(IN_SIZES)
D_IN_PAD = 5632

COL_AX, COL_AG, COL_BQ, COL_BK, COL_BV, COL_CU = 0, 512, 1024, 1536, 1664, 1792
COL_DQ, COL_DK, COL_DV, COL_DO, COL_GATES = 3328, 3840, 4352, 4864, 5376
LANES = 128

FFT_N = 2 * SEQ
FFT_N1 = 64
FFT_N2 = 128
PITCH = 136

VMEM_LIMIT = 56 * 1024 * 1024


def _cparams(sem, vmem=VMEM_LIMIT):
    return pltpu.CompilerParams(dimension_semantics=sem, vmem_limit_bytes=vmem)


def _bdot(a, b):
    return jnp.dot(a.astype(BF16), b.astype(BF16), preferred_element_type=F32)


def _sigmoid(x):
    return 0.5 * jnp.tanh(0.5 * x) + 0.5


def _log_sigmoid(x):
    return jnp.minimum(x, 0.0) - jnp.log1p(jnp.exp(-jnp.abs(x)))


def _softplus(x):
    return jnp.maximum(x, 0.0) + jnp.log1p(jnp.exp(-jnp.abs(x)))


def _ada_kernel(c_ref, w_ref, b_ref, o_ref):
    c = c_ref[...]
    o_ref[...] = _bdot(c * _sigmoid(c), w_ref[...]) + b_ref[...]


def _ada_all(c, ada_w, ada_b):
    bsz = c.shape[0]
    rows = 8
    cp = jnp.zeros((rows, D_MODEL), F32).at[:bsz].set(c)
    tn = 1024
    out = pl.pallas_call(
        _ada_kernel,
        grid=(DEPTH, 6 * D_MODEL // tn),
        in_specs=[pl.BlockSpec((rows, D_MODEL), lambda l, j: (0, 0)),
                  pl.BlockSpec((None, D_MODEL, tn), lambda l, j: (l, 0, j)),
                  pl.BlockSpec((None, 1, tn), lambda l, j: (l, 0, j))],
        out_specs=pl.BlockSpec((None, rows, tn), lambda l, j: (l, 0, j)),
        out_shape=jax.ShapeDtypeStruct((DEPTH, rows, 6 * D_MODEL), F32),
        compiler_params=_cparams(("parallel", "parallel")),
        name="ada_mod",
    )(cp, ada_w, ada_b.reshape(DEPTH, 1, 6 * D_MODEL))
    return out[:, :bsz].reshape(DEPTH, bsz, 6, D_MODEL)


def _rms_mod(x, g, scale, shift):
    ms = jnp.mean(x * x, axis=-1, keepdims=True)
    return (x * lax.rsqrt(ms + EPS)) * (g * (1.0 + scale)) + shift


def _inproj_kernel(x_ref, m_ref, g_ref, w_ref, b_ref, o_ref, h_scr):
    @pl.when(pl.program_id(1) == 0)
    def _():
        h_scr[...] = _rms_mod(x_ref[...], g_ref[...], m_ref[1:2, :], m_ref[0:1, :]).astype(BF16)

    o_ref[...] = jnp.dot(h_scr[...], w_ref[...], preferred_element_type=F32) + b_ref[...]


def _inproj(x2, mod_l, g, w_all, b_all, layer):
    n = x2.shape[0]
    tm, tn = 1024, 1408
    per_b = SEQ // tm
    return pl.pallas_call(
        _inproj_kernel,
        grid=(n // tm, D_IN_PAD // tn),
        in_specs=[pl.BlockSpec((tm, D_MODEL), lambda i, j: (i, 0)),
                  pl.BlockSpec((None, 6, D_MODEL), lambda i, j: (i // per_b, 0, 0)),
                  pl.BlockSpec((1, D_MODEL), lambda i, j: (0, 0)),
                  pl.BlockSpec((None, D_MODEL, tn), lambda i, j: (layer, 0, j)),
                  pl.BlockSpec((None, 1, tn), lambda i, j: (layer, 0, j))],
        out_specs=pl.BlockSpec((tm, tn), lambda i, j: (i, j)),
        out_shape=jax.ShapeDtypeStruct((n, D_IN_PAD), F32),
        scratch_shapes=[pltpu.VMEM((tm, D_MODEL), BF16)],
        compiler_params=_cparams(("parallel", "arbitrary")),
        name="in_proj",
    )(x2, mod_l, g.reshape(1, D_MODEL), w_all, b_all)


def _shifted(ext, off, rows):
    total = ext.shape[0]
    if off == 0:
        return ext[8:8 + rows]
    return pltpu.roll(ext, (-off) % total, axis=0)[8:8 + rows]


def _dwconv_ext(ext, w_ref, b_ref, left, rows):
    out = b_ref[...]
    for j in range(w_ref.shape[0]):
        out = out + _shifted(ext, j - left, rows) * w_ref[j:j + 1, :]
    return out


def _dwconv_tile(x_ref, p_ref, n_ref, w_ref, b_ref, tile, n_tiles, left):
    prev = jnp.where(tile > 0, p_ref[...], 0.0)
    nxt = jnp.where(tile < n_tiles - 1, n_ref[...], 0.0)
    ext = jnp.concatenate([prev, x_ref[...], nxt], axis=0)
    return _dwconv_ext(ext, w_ref, b_ref, left, x_ref.shape[0])


def _dwconv_rows(ref, lo, rows, w_ref, b_ref, left):
    zeros = jnp.zeros((8, ref.shape[1]), F32)
    prev = ref[lo - 8:lo, :] if lo > 0 else zeros
    nxt = ref[lo + rows:lo + rows + 8, :] if lo + rows < SEQ else zeros
    ext = jnp.concatenate([prev, ref[lo:lo + rows, :], nxt], axis=0)
    return _dwconv_ext(ext, w_ref, b_ref, left, rows)


def _halo_specs(tile_rows, width, col_block, tile_of):
    r8 = tile_rows // 8
    last8 = SEQ // 8 - 1

    def main(b, t):
        return (b, tile_of(t), col_block)

    def prev(b, t):
        return (b, jnp.maximum(tile_of(t) * r8 - 1, 0), col_block)

    def nxt(b, t):
        return (b, jnp.minimum((tile_of(t) + 1) * r8, last8), col_block)

    return [pl.BlockSpec((None, tile_rows, width), main),
            pl.BlockSpec((None, 8, width), prev),
            pl.BlockSpec((None, 8, width), nxt)]


def _lru_gates(xc, wg, bg, lam):
    gates = _bdot(xc, wg) + bg
    r = _sigmoid(gates[:, :GROUP_W])
    i = _sigmoid(gates[:, GROUP_W:])
    log_a = (-LRU_C * _softplus(-lam)) * r
    a = jnp.exp(log_a)
    th = jnp.tanh(log_a)
    u = jnp.sqrt(-2.0 * th / (1.0 - th)) * (i * xc)
    return a, u


def _scan8(a, u, ridx, reverse):
    for k in (1, 2, 4):
        if reverse:
            keep = ridx < 8 - k
            sh = 8 - k
        else:
            keep = ridx >= k
            sh = k
        a_sh = jnp.where(keep, pltpu.roll(a, sh, axis=0), 1.0)
        u_sh = jnp.where(keep, pltpu.roll(u, sh, axis=0), 0.0)
        u = a * u_sh + u
        a = a * a_sh
    return a, u


def _lru_kernel(xf_ref, pf_ref, nf_ref, xb_ref, pb_ref, nb_ref, cw_ref, cb_ref, wg_ref, bg_ref, lam_ref,
                hf_ref, hb_ref, af_scr, uf_scr, ab_scr, ub_scr, cf_scr, cb_scr, c_scr, *, n_tiles):
    t = pl.program_id(1)
    n_chunks = af_scr.shape[0]
    xc = _dwconv_tile(xf_ref, pf_ref, nf_ref, cw_ref, cb_ref, t, n_tiles, 2)
    a, u = _lru_gates(xc, wg_ref[0], bg_ref[0], lam_ref[0])
    af_scr[...] = a.reshape(n_chunks, 8, GROUP_W)
    uf_scr[...] = u.reshape(n_chunks, 8, GROUP_W)
    xc = _dwconv_tile(xb_ref, pb_ref, nb_ref, cw_ref, cb_ref, n_tiles - 1 - t, n_tiles, 2)
    a, u = _lru_gates(xc, wg_ref[1], bg_ref[1], lam_ref[1])
    ab_scr[...] = a.reshape(n_chunks, 8, GROUP_W)
    ub_scr[...] = u.reshape(n_chunks, 8, GROUP_W)

    @pl.when(t == 0)
    def _():
        c_scr[...] = jnp.zeros_like(c_scr)

    ridx = lax.broadcasted_iota(jnp.int32, (8, GROUP_W), 0)

    def local(c, carry):
        a, u = _scan8(af_scr[c], uf_scr[c], ridx, False)
        af_scr[c] = a
        uf_scr[c] = u
        a, u = _scan8(ab_scr[c], ub_scr[c], ridx, True)
        ab_scr[c] = a
        ub_scr[c] = u
        return carry

    lax.fori_loop(0, n_chunks, local, 0, unroll=4)

    def chain(c, carry):
        cf, cb = carry
        cf_scr[c] = jnp.broadcast_to(cf, (8, GROUP_W))
        cf = af_scr[c][7:8, :] * cf + uf_scr[c][7:8, :]
        cr = n_chunks - 1 - c
        cb_scr[cr] = jnp.broadcast_to(cb, (8, GROUP_W))
        cb = ab_scr[cr][0:1, :] * cb + ub_scr[cr][0:1, :]
        return cf, cb

    cf, cb = lax.fori_loop(0, n_chunks, chain, (c_scr[0:1, :], c_scr[1:2, :]), unroll=4)
    c_scr[0:1, :] = cf
    c_scr[1:2, :] = cb

    def apply(c, carry):
        r0 = pl.multiple_of(c * 8, 8)
        hf_ref[pl.ds(r0, 8), :] = uf_scr[c] + af_scr[c] * cf_scr[c]
        hb_ref[pl.ds(r0, 8), :] = ub_scr[c] + ab_scr[c] * cb_scr[c]
        return carry

    lax.fori_loop(0, n_chunks, apply, 0, unroll=4)


def _block_diag(w):
    nb, k, j = w.shape
    eye = jnp.eye(nb, dtype=w.dtype)
    return jnp.einsum('nkj,nm->nkmj', w, eye).reshape(nb * k, nb * j)


def _mixer_rglru(proj3, conv_w, conv_b, wa, ba, wx, bx, lam):
    bsz = proj3.shape[0]
    ts = 512
    n_tiles = SEQ // ts
    wg = jnp.stack([jnp.concatenate([_block_diag(wa[d]), _block_diag(wx[d])], axis=1) for d in range(2)]).astype(BF16)
    bg = jnp.stack([jnp.concatenate([ba[d], bx[d]]).reshape(1, 2 * GROUP_W) for d in range(2)])
    small = lambda *shape: pl.BlockSpec(shape, lambda b, t: (0,) * len(shape))
    fwd_of = lambda t: t
    bwd_of = lambda t: n_tiles - 1 - t
    out = lambda tile_of: pl.BlockSpec((None, ts, GROUP_W), lambda b, t: (b, tile_of(t), 0))
    tile_scr = pltpu.VMEM((ts // 8, 8, GROUP_W), F32)
    return pl.pallas_call(
        functools.partial(_lru_kernel, n_tiles=n_tiles),
        grid=(bsz, n_tiles),
        in_specs=_halo_specs(ts, GROUP_W, COL_AX // GROUP_W, fwd_of) + _halo_specs(ts, GROUP_W, COL_AX // GROUP_W, bwd_of)
        + [small(4, GROUP_W), small(1, GROUP_W), small(2, GROUP_W, 2 * GROUP_W), small(2, 1, 2 * GROUP_W),
           small(2, 1, GROUP_W)],
        out_specs=[out(fwd_of), out(bwd_of)],
        out_shape=[jax.ShapeDtypeStruct((bsz, SEQ, GROUP_W), F32)] * 2,
        scratch_shapes=[tile_scr] * 6 + [pltpu.VMEM((8, GROUP_W), F32)],
        compiler_params=_cparams(("parallel", "arbitrary")),
        name="rglru",
    )(proj3, proj3, proj3, proj3, proj3, proj3, conv_w, conv_b.reshape(1, GROUP_W), wg, bg,
      lam.reshape(2, 1, GROUP_W))


def _split_dot(x, m_ref):
    hi = x.astype(BF16)
    lo = (x - hi.astype(F32)).astype(BF16)
    m = m_ref[...]
    return (jnp.dot(hi, m, preferred_element_type=F32) + jnp.dot(lo, m, preferred_element_type=F32))


def _norm_rope(x, gain, m_ref, cos, sins):
    width = x.shape[1]
    ms = _split_dot(x * x, m_ref)
    xn = x * lax.rsqrt(ms + EPS) * gain
    lane = lax.broadcasted_iota(jnp.int32, xn.shape, 1)
    first = (lane % ROPE_AXIS) < (ROPE_AXIS // 2)
    half = ROPE_AXIS // 2
    partner = jnp.where(first, pltpu.roll(xn, width - half, axis=1), pltpu.roll(xn, half, axis=1))
    return xn * cos + partner * sins


def _attn_prep_kernel(q_ref, kv_ref, gq_ref, gk_ref, mq_ref, mk_ref, cos_ref, sin_ref,
                      qo_ref, kt_ref, vt_ref):
    cos = cos_ref[...]
    sins = sin_ref[...]
    cos_q = jnp.concatenate([cos] * (GROUP_W // LANES), axis=1)
    sin_q = jnp.concatenate([sins] * (GROUP_W // LANES), axis=1)
    q = _norm_rope(q_ref[...], gq_ref[...], mq_ref, cos_q, sin_q)
    qo_ref[...] = (q * (ATT_HEAD_DIM ** -0.5 * LOG2_E)).astype(BF16)
    kv = kv_ref[...]
    kt_ref[...] = _norm_rope(kv[:, :LANES], gk_ref[...], mk_ref, cos, sins).T.astype(BF16)
    v_t = kv[:, LANES:].T
    row = lax.broadcasted_iota(jnp.int32, (ATT_V_ROWS - ATT_HEAD_DIM, v_t.shape[1]), 0)
    ones_rows = jnp.where(row == 0, 1.0, 0.0)
    for g in range(ATT_KV_HEADS):
        vt_ref[g] = jnp.concatenate([v_t[g * ATT_HEAD_DIM:(g + 1) * ATT_HEAD_DIM, :], ones_rows], axis=0).astype(BF16)


def _rope_tables():
    rows = SEQ // GRID_W
    row = jnp.repeat(jnp.arange(rows, dtype=F32), GRID_W)
    col = jnp.tile(jnp.arange(GRID_W, dtype=F32), rows)
    inv = ROPE_THETA ** (-jnp.arange(0, ROPE_AXIS, 2, dtype=F32) / ROPE_AXIS)
    ar = row[:, None] * inv
    ac = col[:, None] * inv
    ang = jnp.concatenate([ar, ar, ac, ac], axis=1)
    sign = jnp.concatenate([-jnp.ones((ROPE_AXIS // 2,), F32), jnp.ones((ROPE_AXIS // 2,), F32)] * 2)
    cos = jnp.tile(jnp.cos(ang), (1, 2))
    sins = jnp.tile(jnp.sin(ang) * sign, (1, 2))
    return cos, sins


def _head_mean_matrix(width):
    idx = np.arange(width) // ATT_HEAD_DIM
    return jnp.asarray((idx[:, None] == idx[None, :]).astype(np.float32) / ATT_HEAD_DIM, dtype=BF16)


def _attn_kernel(q_ref, kt_ref, vt_ref, o_ref):
    def scores(h):
        g = h // ATT_GROUP
        q = q_ref[:, h * ATT_HEAD_DIM:(h + 1) * ATT_HEAD_DIM]
        return jnp.dot(q, kt_ref[g * ATT_HEAD_DIM:(g + 1) * ATT_HEAD_DIM, :], preferred_element_type=F32)

    def probs(s):
        return jnp.exp2(s - jnp.max(s, axis=-1, keepdims=True)).astype(BF16)

    def values(h, p):
        o_aug = lax.dot_general(vt_ref[h // ATT_GROUP], p, (((1,), (1,)), ((), ())), preferred_element_type=F32)
        return (o_aug[:ATT_HEAD_DIM] / o_aug[ATT_HEAD_DIM:ATT_HEAD_DIM + 1]).T

    outs = []
    for h0 in range(0, ATT_HEADS, ATT_HEAD_GROUP):
        heads = range(h0, h0 + ATT_HEAD_GROUP)
        ps = [probs(s) for s in [scores(h) for h in heads]]
        outs += [values(h, p) for h, p in zip(heads, ps)]
    o_ref[...] = jnp.concatenate(outs, axis=1).astype(o_ref.dtype)


def _mixer_attention(proj3, q_g, k_g):
    bsz = proj3.shape[0]
    ts = 512
    cos, sins = _rope_tables()
    gq = jnp.tile(q_g, ATT_HEADS).reshape(1, GROUP_W)
    gk = jnp.tile(k_g, ATT_KV_HEADS).reshape(1, LANES)
    const = lambda shape: pl.BlockSpec(shape, lambda b, t: (0, 0))
    qp, kt, vt = pl.pallas_call(
        _attn_prep_kernel,
        grid=(bsz, SEQ // ts),
        in_specs=[pl.BlockSpec((None, ts, GROUP_W), lambda b, t: (b, t, COL_BQ // GROUP_W)),
                  pl.BlockSpec((None, ts, 2 * LANES), lambda b, t: (b, t, COL_BK // (2 * LANES))),
                  const((1, GROUP_W)), const((1, LANES)), const((GROUP_W, GROUP_W)), const((LANES, LANES)),
                  pl.BlockSpec((ts, LANES), lambda b, t: (t, 0)),
                  pl.BlockSpec((ts, LANES), lambda b, t: (t, 0))],
        out_specs=[pl.BlockSpec((None, ts, GROUP_W), lambda b, t: (b, t, 0)),
                   pl.BlockSpec((None, LANES, ts), lambda b, t: (b, 0, t)),
                   pl.BlockSpec((None, ATT_KV_HEADS, ATT_V_ROWS, ts), lambda b, t: (b, 0, 0, t))],
        out_shape=[jax.ShapeDtypeStruct((bsz, SEQ, GROUP_W), BF16),
                   jax.ShapeDtypeStruct((bsz, LANES, SEQ), BF16),
                   jax.ShapeDtypeStruct((bsz, ATT_KV_HEADS, ATT_V_ROWS, SEQ), BF16)],
        compiler_params=_cparams(("parallel", "parallel")),
        name="attn_prep",
    )(proj3, proj3, gq, gk, _head_mean_matrix(GROUP_W), _head_mean_matrix(LANES), cos, sins)

    tq = 512
    return pl.pallas_call(
        _attn_kernel,
        grid=(bsz, SEQ // tq),
        in_specs=[pl.BlockSpec((None, tq, GROUP_W), lambda b, t: (b, t, 0)),
                  pl.BlockSpec((None, LANES, SEQ), lambda b, t: (b, 0, 0)),
                  pl.BlockSpec((None, ATT_KV_HEADS, ATT_V_ROWS, SEQ), lambda b, t: (b, 0, 0, 0))],
        out_specs=pl.BlockSpec((None, tq, GROUP_W), lambda b, t: (b, t, 0)),
        out_shape=jax.ShapeDtypeStruct((bsz, SEQ, GROUP_W), BF16),
        compiler_params=_cparams(("parallel", "parallel")),
        name="attention",
    )(qp, kt, vt)


def _hy_features():
    L = SEQ
    pos = jnp.arange(L, dtype=F32)
    t = pos / max(L - 1, 1)
    bands = jnp.linspace(1e-4, HY_BANDS - 1, HY_BANDS, dtype=F32)
    ang = (2.0 * math.pi * pos / L)[:, None] * bands
    feat = jnp.concatenate([t[:, None], jnp.cos(ang), -jnp.sin(ang)], axis=-1)
    feat = jnp.pad(feat, ((0, 0), (0, HY_EMB_PAD - HY_EMB)))
    rev_idx = np.concatenate([[0], np.arange(L - 1, 0, -1)])
    return feat, feat[rev_idx]


def _hy_mlp(feat, feat_t, w1t_ref, b1_ref, w2t_ref, b2_ref, w3_ref, sf_ref, dec_ref):
    sf = sf_ref[...]
    h = jnp.sin(sf * (jnp.dot(w1t_ref[...], feat_t, precision=HIGHEST, preferred_element_type=F32) + b1_ref[...]))
    h = jnp.sin(sf * (jnp.dot(w2t_ref[...], h, precision=HIGHEST, preferred_element_type=F32) + b2_ref[...]))
    out = jnp.dot(h.T, w3_ref[...], precision=HIGHEST, preferred_element_type=F32)
    return out * jnp.exp(-feat[:, 0:1] * jnp.abs(dec_ref[...]))


def _hy_filter_kernel(ff_ref, fft_ref, fr_ref, frt_ref, w1t_ref, b1_ref, w2t_ref, b2_ref, w3f_ref, w3b_ref, sf_ref,
                      decf_ref, decb_ref, of_ref, ob_ref, ssq_ref):
    i = pl.program_id(1)
    hf = _hy_mlp(ff_ref[...], fft_ref[...], w1t_ref, b1_ref, w2t_ref, b2_ref, w3f_ref, sf_ref, decf_ref)
    hb = _hy_mlp(fr_ref[...], frt_ref[...], w1t_ref, b1_ref, w2t_ref, b2_ref, w3b_ref, sf_ref, decb_ref)

    @pl.when(i == 0)
    def _():
        ssq_ref[...] = jnp.zeros_like(ssq_ref)

    ssq_ref[0:1, :] += jnp.sum(hf * hf + hb * hb, axis=0, keepdims=True)
    of_ref[...] = hf
    row = lax.broadcasted_iota(jnp.int32, hb.shape, 0)
    ob_ref[...] = jnp.where(jnp.logical_and(i == 0, row == 0), 0.0, hb)


def _hy_filters(hy_w1, hy_b1, hy_w2, hy_b2, hy_w3, hy_sin_freq, hy_decay):
    feat, feat_rev = _hy_features()
    tr = 512
    cw = HY_ORDER * HY_W
    w1t = jnp.transpose(jnp.pad(hy_w1, ((0, 0), (0, HY_EMB_PAD - HY_EMB), (0, 0))), (0, 2, 1))
    w2t = jnp.transpose(hy_w2, (0, 2, 1))
    w3 = hy_w3.reshape(DEPTH, HY_FFN, HY_ORDER, 2, HY_W)
    dec = hy_decay.reshape(DEPTH, HY_ORDER, 2, HY_W)
    w3f = w3[:, :, :, 0].reshape(DEPTH, HY_FFN, cw)
    w3b = w3[:, :, :, 1].reshape(DEPTH, HY_FFN, cw)
    decf = dec[:, :, 0].reshape(DEPTH, 1, cw)
    decb = dec[:, :, 1].reshape(DEPTH, 1, cw)
    col = lambda a: a.reshape(DEPTH, HY_FFN, 1)
    row_spec = pl.BlockSpec((tr, HY_EMB_PAD), lambda l, i: (i, 0))
    rowt_spec = pl.BlockSpec((HY_EMB_PAD, tr), lambda l, i: (0, i))
    per_layer = lambda a, b: pl.BlockSpec((None, a, b), lambda l, i: (l, 0, 0))
    out_spec = pl.BlockSpec((None, tr, cw), lambda l, i: (l, i, 0))
    return pl.pallas_call(
        _hy_filter_kernel,
        grid=(DEPTH, SEQ // tr),
        in_specs=[row_spec, rowt_spec, row_spec, rowt_spec, per_layer(HY_FFN, HY_EMB_PAD), per_layer(HY_FFN, 1),
                  per_layer(HY_FFN, HY_FFN), per_layer(HY_FFN, 1), per_layer(HY_FFN, cw), per_layer(HY_FFN, cw),
                  per_layer(HY_FFN, 1), per_layer(1, cw), per_layer(1, cw)],
        out_specs=[out_spec, out_spec, per_layer(8, cw)],
        out_shape=[jax.ShapeDtypeStruct((DEPTH, SEQ, cw), F32)] * 2 + [jax.ShapeDtypeStruct((DEPTH, 8, cw), F32)],
        compiler_params=_cparams(("arbitrary", "arbitrary")),
        name="hyena_filter",
    )(feat, feat.T, feat_rev, feat_rev.T, w1t, col(hy_b1), w2t, col(hy_b2), w3f, w3b, col(hy_sin_freq), decf, decb)


def _dft_constants():
    n1 = np.arange(FFT_N1)
    n2 = np.arange(FFT_N2)
    f1 = np.exp(-2j * np.pi * np.outer(n1, n1) / FFT_N1)
    stack = lambda m: np.concatenate([m.real, m.imag], axis=0)
    half = FFT_N1 // 2
    sig_l = stack(f1[:, :half])
    sig_r = np.concatenate([-f1[:, :half].imag, f1[:, :half].real], axis=0)
    fil_r = stack(f1[:, half:])
    f2 = np.exp(-2j * np.pi * np.outer(n2, n2) / FFT_N2)
    tw = np.exp(-2j * np.pi * np.outer(n1, n2) / FFT_N)
    fwd = f2[None, :, :] * tw[:, None, :]
    inv = np.conj(np.transpose(fwd, (0, 2, 1))) / FFT_N
    block = lambda m: np.concatenate([np.concatenate([m.real, -m.imag], axis=2),
                                      np.concatenate([m.imag, m.real], axis=2)], axis=1)
    g1 = np.conj(f1[:half, :])
    out_l = stack(g1)
    out_r = np.concatenate([-g1.imag, g1.real], axis=0)
    as32 = lambda a: jnp.asarray(a.astype(np.float32))
    return dict(sig_l=as32(sig_l), sig_r=as32(sig_r), fil_r=as32(fil_r), fwd=as32(block(fwd)),
                inv=as32(block(inv)), out_l=as32(out_l), out_r=as32(out_r))


_N2_GROUP = 8


def _dft_a_kernel(*refs, conv):
    if conv:
        u0_ref, u1_ref, cw_ref, cb_ref, ml_ref, mr_ref, o_ref, xs, ysc = refs
    else:
        u0_ref, u1_ref, ml_ref, mr_ref, o_ref, xs, ysc = refs
    half = FFT_N1 // 2
    for r, u_ref in enumerate((u0_ref, u1_ref)):
        for n1 in range(half):
            lo = FFT_N2 * n1
            rows = _dwconv_rows(u_ref, lo, FFT_N2, cw_ref, cb_ref, 1) if conv else u_ref[lo:lo + FFT_N2, :]
            xs[r, PITCH * n1:PITCH * n1 + FFT_N2, :] = rows
    ml = ml_ref[...]
    mr = mr_ref[...]

    def body(g, carry):
        n2 = g * _N2_GROUP
        x0 = jnp.concatenate([xs[0, pl.ds(n2 + i, half, stride=PITCH), :] for i in range(_N2_GROUP)], axis=1)
        x1 = jnp.concatenate([xs[1, pl.ds(n2 + i, half, stride=PITCH), :] for i in range(_N2_GROUP)], axis=1)
        y = _bdot(ml, x0) + _bdot(mr, x1)
        for i in range(_N2_GROUP):
            ysc[pl.ds(n2 + i, 2 * FFT_N1, stride=PITCH), :] = y[:, LANES * i:LANES * (i + 1)]
        return carry

    lax.fori_loop(0, FFT_N2 // _N2_GROUP, body, 0)
    for row in range(2 * FFT_N1):
        o_ref[FFT_N2 * row:FFT_N2 * (row + 1), :] = ysc[PITCH * row:PITCH * row + FFT_N2, :].astype(o_ref.dtype)


def _dft_a(ml, mr, srcs, groups, n_slabs, conv_args=None):
    conv = conv_args is not None
    slab = lambda arr_map: pl.BlockSpec((None, SEQ, LANES), arr_map)
    mspec = pl.BlockSpec((2 * FFT_N1, FFT_N1 // 2), lambda g, s: (0, 0))
    in_specs = [slab(srcs[0][1]), slab(srcs[1][1])]
    args = [srcs[0][0], srcs[1][0]]
    if conv:
        cw, cb, col0 = conv_args
        in_specs += [pl.BlockSpec((cw.shape[0], LANES), lambda g, s: (0, col0 + s)),
                     pl.BlockSpec((1, LANES), lambda g, s: (0, col0 + s))]
        args += [cw, cb]
    rows = 2 * FFT_N1 * FFT_N2
    return pl.pallas_call(
        functools.partial(_dft_a_kernel, conv=conv),
        grid=(groups, n_slabs),
        in_specs=in_specs + [mspec, mspec],
        out_specs=pl.BlockSpec((None, rows, LANES), lambda g, s: (g, 0, s)),
        out_shape=jax.ShapeDtypeStruct((groups, rows, n_slabs * LANES), BF16),
        scratch_shapes=[pltpu.VMEM((2, (FFT_N1 // 2) * PITCH, LANES), F32),
                        pltpu.VMEM((2 * FFT_N1 * PITCH, LANES), F32)],
        compiler_params=_cparams(("parallel", "parallel")),
        name="dft_a",
    )(*args, ml, mr)


def _dft_mid_kernel(f_ref, g_ref, ah_ref, ssq_ref, a_ref, o_ref):
    scale = lax.rsqrt(ssq_ref[0:1, :] + EPS)
    for kk in range(f_ref.shape[0]):
        f = f_ref[kk]
        g = g_ref[kk]
        h = _bdot(f, jnp.concatenate([ah_ref[0, kk], ah_ref[1, kk]], axis=0)) * scale
        hr = h[:FFT_N2]
        hi = h[FFT_N2:]
        for p in range(a_ref.shape[0]):
            y = _bdot(f, jnp.concatenate([a_ref[p, 0, kk], a_ref[p, 1, kk]], axis=0))
            yr = y[:FFT_N2]
            yi = y[FFT_N2:]
            z = jnp.concatenate([yr * hr - yi * hi, yr * hi + yi * hr], axis=0)
            w = _bdot(g, z)
            o_ref[p, 0, kk] = w[:FFT_N2].astype(o_ref.dtype)
            o_ref[p, 1, kk] = w[FFT_N2:].astype(o_ref.dtype)


def _dft_mid(fwd, inv, ah5, ssq, layer, order, a5):
    pairs = a5.shape[0]
    kb = 2
    blk = pl.BlockSpec((pairs, 2, kb, FFT_N2, HY_W), lambda k: (0, 0, k, 0, 0))
    mat = pl.BlockSpec((kb, 2 * FFT_N2, 2 * FFT_N2), lambda k: (k, 0, 0))
    return pl.pallas_call(
        _dft_mid_kernel,
        grid=(FFT_N1 // kb,),
        in_specs=[mat, mat,
                  pl.BlockSpec((None, 2, kb, FFT_N2, HY_W), lambda k: (layer, 0, k, 0, order)),
                  pl.BlockSpec((None, 8, HY_W), lambda k: (layer, 0, order)),
                  blk],
        out_specs=blk,
        out_shape=jax.ShapeDtypeStruct(a5.shape, BF16),
        compiler_params=_cparams(("parallel",)),
        name="dft_mid",
    )(fwd, inv, ah5, ssq, a5)


def _dft_c_kernel(*refs, u_conv):
    if u_conv:
        (b_ref, u0_ref, u1_ref, g0_ref, g1_ref, ucw_ref, ucb_ref, gcw_ref, gcb_ref, ml_ref, mr_ref, skip_ref,
         o_ref, bs, ys) = refs
    else:
        b_ref, u0_ref, u1_ref, g0_ref, g1_ref, gcw_ref, gcb_ref, ml_ref, mr_ref, skip_ref, o_ref, bs, ys = refs
    half = FFT_N1 // 2
    ml = ml_ref[...]
    mr = mr_ref[...]
    im0 = FFT_N1 * PITCH
    for row in range(2 * FFT_N1):
        bs[PITCH * row:PITCH * row + FFT_N2, :] = b_ref[FFT_N2 * row:FFT_N2 * (row + 1), :].astype(F32)

    def body(g, carry):
        n2 = g * _N2_GROUP
        br = jnp.concatenate([bs[pl.ds(n2 + i, FFT_N1, stride=PITCH), :] for i in range(_N2_GROUP)], axis=1)
        bi = jnp.concatenate([bs[pl.ds(im0 + n2 + i, FFT_N1, stride=PITCH), :] for i in range(_N2_GROUP)], axis=1)
        y = _bdot(ml, br) + _bdot(mr, bi)
        for i in range(_N2_GROUP):
            ys[0, pl.ds(n2 + i, half, stride=PITCH), :] = y[:half, LANES * i:LANES * (i + 1)]
            ys[1, pl.ds(n2 + i, half, stride=PITCH), :] = y[half:, LANES * i:LANES * (i + 1)]
        return carry

    lax.fori_loop(0, FFT_N2 // _N2_GROUP, body, 0)
    skip = skip_ref[...]
    for r, (u_ref, g_ref) in enumerate(((u0_ref, g0_ref), (u1_ref, g1_ref))):
        for n1 in range(half):
            lo = FFT_N2 * n1
            u = _dwconv_rows(u_ref, lo, FFT_N2, ucw_ref, ucb_ref, 1) if u_conv else u_ref[lo:lo + FFT_N2, :]
            gate = _dwconv_rows(g_ref, lo, FFT_N2, gcw_ref, gcb_ref, 1)
            conv = ys[r, PITCH * n1:PITCH * n1 + FFT_N2, :]
            o_ref[r, lo:lo + FFT_N2, :] = (gate * (conv + u * skip)).astype(o_ref.dtype)


def _dft_c(ml, mr, b3, u_src, gate_src, u_conv_args, gate_conv_args, skip, out_dtype):
    pairs = b3.shape[0]
    n_slabs = HY_W // LANES
    u_conv = u_conv_args is not None
    slab = lambda m: pl.BlockSpec((None, SEQ, LANES), m)
    wspecs = lambda cw, col0: [pl.BlockSpec((cw.shape[0], LANES), lambda p, s: (0, col0 + s)),
                               pl.BlockSpec((1, LANES), lambda p, s: (0, col0 + s))]
    in_specs = [pl.BlockSpec((None, 2 * FFT_N1 * FFT_N2, LANES), lambda p, s: (p, 0, s)),
                slab(u_src[1]), slab(u_src[2]), slab(gate_src[1]), slab(gate_src[2])]
    args = [b3, u_src[0], u_src[0], gate_src[0], gate_src[0]]
    if u_conv:
        in_specs += wspecs(u_conv_args[0], u_conv_args[2])
        args += [u_conv_args[0], u_conv_args[1]]
    in_specs += wspecs(gate_conv_args[0], gate_conv_args[2])
    args += [gate_conv_args[0], gate_conv_args[1]]
    mspec = pl.BlockSpec((FFT_N1, FFT_N1), lambda p, s: (0, 0))
    in_specs += [mspec, mspec, pl.BlockSpec((1, LANES), lambda p, s: (0, s))]
    args += [ml, mr, skip.reshape(1, HY_W)]
    return pl.pallas_call(
        functools.partial(_dft_c_kernel, u_conv=u_conv),
        grid=(pairs, n_slabs),
        in_specs=in_specs,
        out_specs=pl.BlockSpec((2, SEQ, LANES), lambda p, s: (p, 0, s)),
        out_shape=jax.ShapeDtypeStruct((2 * pairs, SEQ, HY_W), out_dtype),
        scratch_shapes=[pltpu.VMEM((2 * FFT_N1 * PITCH, LANES), F32),
                        pltpu.VMEM((2, (FFT_N1 // 2) * PITCH, LANES), F32)],
        compiler_params=_cparams(("parallel", "parallel")),
        name="dft_c",
    )(*args)


def _mixer_hyena(proj3, conv_w, conv_b, skip, ah5, ssq, layer, consts):
    bsz = proj3.shape[0]
    pairs = bsz // 2
    n_slabs = HY_W // LANES
    cb = conv_b.reshape(1, 3 * HY_W)
    col = lambda which: (COL_CU + which * HY_W) // LANES
    proj_map = lambda which, odd: (lambda p, s: (2 * p + odd, 0, col(which) + s))
    plain_map = lambda odd: (lambda p, s: (2 * p + odd, 0, s))
    conv_args = lambda which: (conv_w, cb, which * n_slabs)
    a5_shape = (pairs, 2, FFT_N1, FFT_N2, HY_W)

    a = _dft_a(consts['sig_l'], consts['sig_r'], [(proj3, proj_map(0, 0)), (proj3, proj_map(0, 1))],
               pairs, n_slabs, conv_args(0))
    b = _dft_mid(consts['fwd'], consts['inv'], ah5, ssq, layer, 0, a.reshape(a5_shape))
    z1 = _dft_c(consts['out_l'], consts['out_r'], b.reshape(a.shape),
                (proj3, proj_map(0, 0), proj_map(0, 1)), (proj3, proj_map(1, 0), proj_map(1, 1)),
                conv_args(0), conv_args(1), skip[0], F32)
    a = _dft_a(consts['sig_l'], consts['sig_r'], [(z1, plain_map(0)), (z1, plain_map(1))], pairs, n_slabs)
    b = _dft_mid(consts['fwd'], consts['inv'], ah5, ssq, layer, 1, a.reshape(a5_shape))
    return _dft_c(consts['out_l'], consts['out_r'], b.reshape(a.shape),
                  (z1, plain_map(0), plain_map(1)), (proj3, proj_map(2, 0), proj_map(2, 1)),
                  None, conv_args(2), skip[1], BF16)


def _hyena_filter_stage(consts, hy_w1, hy_b1, hy_w2, hy_b2, hy_w3, hy_sin_freq, hy_decay):
    hf, hb, ssq = _hy_filters(hy_w1, hy_b1, hy_w2, hy_b2, hy_w3, hy_sin_freq, hy_decay)
    cw = HY_ORDER * HY_W
    fmap = lambda l, s: (l, 0, s)
    ha = _dft_a(consts['sig_l'], consts['fil_r'], [(hf, fmap), (hb, fmap)], DEPTH, cw // LANES)
    return ha.reshape(DEPTH, 2, FFT_N1, FFT_N2, cw), ssq


def _mlstm_chunk(q_ref, k_ref, v_ref, gc_ref, c_scr, m_scr, reverse, i_off, f_off, state_off):
    ch = ML_CHUNK
    ri = lax.broadcasted_iota(jnp.int32, (ch, ch), 0)
    ci = lax.broadcasted_iota(jnp.int32, (ch, ch), 1)
    tri = (ci >= ri) if reverse else (ci <= ri)
    tri_f = tri.astype(F32)
    gc = gc_ref[...]
    gr = gc.T
    b_col = jnp.dot(tri_f, _log_sigmoid(gc), precision=HIGHEST, preferred_element_type=F32)
    b_row = lax.dot_general(_log_sigmoid(gr[:4 * ML_HEADS, :]), tri_f, (((1,), (1,)), ((), ())), precision=HIGHEST,
                            preferred_element_type=F32)
    last = 0 if reverse else ch - 1
    lane = lax.broadcasted_iota(jnp.int32, (ch, ML_HEAD_DIM), 1)
    ones_col = jnp.where(lane == 0, 1.0, 0.0).astype(BF16)

    outs = []
    for h in range(ML_HEADS):
        sl = slice(h * ML_HEAD_DIM, (h + 1) * ML_HEAD_DIM)
        st = state_off + h
        q = (q_ref[:, sl] * (ML_HEAD_DIM ** -0.5)).astype(BF16)
        k = k_ref[:, sl]
        v_aug = jnp.concatenate([v_ref[:, sl].astype(BF16), ones_col], axis=1)
        bc = b_col[:, f_off + h:f_off + h + 1]
        lic = gc[:, i_off + h:i_off + h + 1]
        br = b_row[f_off + h:f_off + h + 1, :]
        lir = gr[i_off + h:i_off + h + 1, :]
        b_tot = bc[last:last + 1, :]
        d = jnp.where(tri, bc - br + lir, -jnp.inf)
        w_end = b_tot - bc + lic
        m_loc = jnp.max(w_end, axis=0, keepdims=True)
        e_end = jnp.exp(w_end - m_loc)
        m_prev = m_scr[st:st + 1, 0:1]
        c_prev = c_scr[st]
        m_inter = bc + m_prev
        m_t = jnp.maximum(m_inter, jnp.max(d, axis=-1, keepdims=True))
        e_inter = jnp.exp(m_inter - m_t)
        qk = lax.dot_general(q, k.astype(BF16), (((1,), (1,)), ((), ())), preferred_element_type=F32)
        s = qk * jnp.exp(d - m_t)
        nd = _bdot(s, v_aug) + e_inter * _bdot(q, c_prev)
        num = nd[:, :ML_HEAD_DIM]
        den = nd[:, ML_HEAD_DIM:ML_HEAD_DIM + 1]
        outs.append(num / jnp.maximum(jnp.abs(den), jnp.exp(-m_t)))
        m_new = jnp.maximum(b_tot + m_prev, m_loc)
        decay = jnp.exp(b_tot + m_prev - m_new)
        gain = jnp.exp(m_loc - m_new)
        dc = lax.dot_general((k * e_end).astype(BF16), v_aug, (((0,), (0,)), ((), ())),
                             preferred_element_type=F32)
        c_scr[st] = decay * c_prev + gain * dc
        m_scr[st:st + 1, :] = jnp.broadcast_to(m_new, (1, LANES))
    return outs


ML_BATCH_ROWS = 1


def _mlstm_kernel(*refs):
    n_in = 8 * ML_BATCH_ROWS
    hf_ref, hb_ref, c_scr, m_scr = refs[n_in:]

    @pl.when(pl.program_id(1) == 0)
    def _():
        c_scr[...] = jnp.zeros_like(c_scr)
        m_scr[...] = jnp.zeros_like(m_scr)

    for r in range(ML_BATCH_ROWS):
        qf_ref, kf_ref, vf_ref, gf_ref, qb_ref, kb_ref, vb_ref, gb_ref = refs[8 * r:8 * r + 8]
        st = 2 * ML_HEADS * r
        outs_f = _mlstm_chunk(qf_ref, kf_ref, vf_ref, gf_ref, c_scr, m_scr, False, 0, ML_HEADS, st)
        outs_b = _mlstm_chunk(qb_ref, kb_ref, vb_ref, gb_ref, c_scr, m_scr, True, 2 * ML_HEADS, 3 * ML_HEADS,
                              st + ML_HEADS)
        hf_ref[r] = jnp.concatenate(outs_f, axis=1)
        hb_ref[r] = jnp.concatenate(outs_b, axis=1)


def _mixer_mlstm(proj2, bsz):
    nc = SEQ // ML_CHUNK
    rows = ML_BATCH_ROWS

    def specs(r, chunk_of):
        def at(col, width):
            def index(i, j):
                return pl.multiple_of((i * rows + r) * SEQ + chunk_of(j) * ML_CHUNK, ML_CHUNK), col
            return pl.BlockSpec((pl.Element(ML_CHUNK), pl.Element(width)), index)
        return [at(COL_DQ, GROUP_W), at(COL_DK, GROUP_W), at(COL_DV, GROUP_W), at(COL_GATES, LANES)]

    fwd_of = lambda j: j
    bwd_of = lambda j: nc - 1 - j
    in_specs = []
    for r in range(rows):
        in_specs += specs(r, fwd_of) + specs(r, bwd_of)
    out = lambda chunk_of: pl.BlockSpec((rows, ML_CHUNK, GROUP_W), lambda i, j: (i, chunk_of(j), 0))
    n_state = 2 * ML_HEADS * rows
    return pl.pallas_call(
        _mlstm_kernel,
        grid=(bsz // rows, nc),
        in_specs=in_specs,
        out_specs=[out(fwd_of), out(bwd_of)],
        out_shape=[jax.ShapeDtypeStruct((bsz, SEQ, GROUP_W), F32)] * 2,
        scratch_shapes=[pltpu.VMEM((n_state, ML_HEAD_DIM, 2 * ML_HEAD_DIM), F32), pltpu.VMEM((n_state, LANES), F32)],
        compiler_params=_cparams(("parallel", "arbitrary")),
        name="mlstm",
    )(*([proj2] * (8 * rows)))


def _outproj_kernel(lf_ref, lb_ref, ga_ref, yb_ref, yc_ref, mf_ref, mb_ref, o_ref, mg_ref, w_ref, x_ref, m_ref,
                    out_ref):
    ya = (jax.nn.gelu(ga_ref[...]) * (lf_ref[...] + lb_ref[...])).astype(BF16)
    normed = []
    for h in range(ML_HEADS):
        sl = slice(h * ML_HEAD_DIM, (h + 1) * ML_HEAD_DIM)
        hh = mf_ref[:, sl] + mb_ref[:, sl]
        ms = jnp.mean(hh * hh, axis=-1, keepdims=True)
        normed.append(hh * lax.rsqrt(ms + EPS) * mg_ref[:, sl])
    yd = (_sigmoid(o_ref[...]) * jnp.concatenate(normed, axis=1)).astype(BF16)
    acc = jnp.dot(ya, w_ref[0:GROUP_W, :], preferred_element_type=F32)
    acc += jnp.dot(yb_ref[...], w_ref[GROUP_W:2 * GROUP_W, :], preferred_element_type=F32)
    acc += jnp.dot(yc_ref[...], w_ref[2 * GROUP_W:3 * GROUP_W, :], preferred_element_type=F32)
    acc += jnp.dot(yd, w_ref[3 * GROUP_W:, :], preferred_element_type=F32)
    out_ref[...] = x_ref[...] + m_ref[2:3, :] * acc


def _outproj(lru_f, lru_b, proj2, y_b, y_c, ml_f, ml_b, ml_g, w_all, layer, x2, mod_l):
    n = x2.shape[0]
    tm = 512
    per_b = SEQ // tm
    grp = pl.BlockSpec((tm, GROUP_W), lambda i: (i, 0))
    return pl.pallas_call(
        _outproj_kernel,
        grid=(n // tm,),
        in_specs=[grp, grp, pl.BlockSpec((tm, GROUP_W), lambda i: (i, COL_AG // GROUP_W)), grp, grp, grp, grp,
                  pl.BlockSpec((pl.Element(tm), pl.Element(GROUP_W)), lambda i: (pl.multiple_of(i * tm, tm), COL_DO)),
                  pl.BlockSpec((1, GROUP_W), lambda i: (0, 0)),
                  pl.BlockSpec((None, D_MODEL, D_MODEL), lambda i: (layer, 0, 0)),
                  pl.BlockSpec((tm, D_MODEL), lambda i: (i, 0)),
                  pl.BlockSpec((None, 6, D_MODEL), lambda i: (i // per_b, 0, 0))],
        out_specs=pl.BlockSpec((tm, D_MODEL), lambda i: (i, 0)),
        out_shape=jax.ShapeDtypeStruct((n, D_MODEL), F32),
        compiler_params=_cparams(("parallel",)),
        name="out_proj",
    )(lru_f, lru_b, proj2, y_b, y_c, ml_f, ml_b, proj2, ml_g.reshape(1, GROUP_W), w_all, x2, mod_l)


def _ffn_kernel(x_ref, xn_ref, m_ref, mn_ref, g_ref, w1_ref, w3_ref, w2_ref, fg_ref, o_ref, h_scr, hn_scr, *, final):
    i = pl.program_id(0)
    j = pl.program_id(1)

    @pl.when(j == 0)
    def _():
        @pl.when(i == 0)
        def _():
            h_scr[...] = _rms_mod(x_ref[...], g_ref[...], m_ref[4:5, :], m_ref[3:4, :]).astype(BF16)

        @pl.when(i > 0)
        def _():
            h_scr[...] = hn_scr[...]

        o_ref[...] = jnp.zeros_like(o_ref)

    n_slices = hn_scr.shape[0] // FFN_NORM_ROWS
    r0 = pl.multiple_of(jnp.minimum(j, n_slices - 1) * FFN_NORM_ROWS, FFN_NORM_ROWS)
    hn_scr[pl.ds(r0, FFN_NORM_ROWS), :] = _rms_mod(xn_ref[pl.ds(r0, FFN_NORM_ROWS), :], g_ref[...],
                                                   mn_ref[4:5, :], mn_ref[3:4, :]).astype(BF16)

    h = h_scr[...]
    a = jnp.dot(h, w1_ref[...], preferred_element_type=F32)
    b = jnp.dot(h, w3_ref[...], preferred_element_type=F32)
    act = (a * _sigmoid(a)) * b
    o_ref[...] += jnp.dot(act.astype(BF16), w2_ref[...], preferred_element_type=F32)

    @pl.when(j == pl.num_programs(1) - 1)
    def _():
        y = x_ref[...] + m_ref[5:6, :] * o_ref[...]
        if final:
            ms = jnp.mean(y * y, axis=-1, keepdims=True)
            y = y * lax.rsqrt(ms + EPS) * fg_ref[...]
        o_ref[...] = y


def _ffn(x2, mod_l, g, w1_all, w3_all, w2_all, layer, final_g, final):
    n = x2.shape[0]
    tm, tf = 512, 512
    per_b = SEQ // tm
    n_tiles = n // tm
    assert D_FF // tf >= tm // FFN_NORM_ROWS
    nxt = lambda i: jnp.minimum(i + 1, n_tiles - 1)
    row = lambda: pl.BlockSpec((1, D_MODEL), lambda i, j: (0, 0))
    return pl.pallas_call(
        functools.partial(_ffn_kernel, final=final),
        grid=(n_tiles, D_FF // tf),
        in_specs=[pl.BlockSpec((tm, D_MODEL), lambda i, j: (i, 0)),
                  pl.BlockSpec((tm, D_MODEL), lambda i, j: (nxt(i), 0)),
                  pl.BlockSpec((None, 6, D_MODEL), lambda i, j: (i // per_b, 0, 0)),
                  pl.BlockSpec((None, 6, D_MODEL), lambda i, j: (nxt(i) // per_b, 0, 0)),
                  row(),
                  pl.BlockSpec((None, D_MODEL, tf), lambda i, j: (layer, 0, j)),
                  pl.BlockSpec((None, D_MODEL, tf), lambda i, j: (layer, 0, j)),
                  pl.BlockSpec((None, tf, D_MODEL), lambda i, j: (layer, j, 0)),
                  row()],
        out_specs=pl.BlockSpec((tm, D_MODEL), lambda i, j: (i, 0)),
        out_shape=jax.ShapeDtypeStruct((n, D_MODEL), F32),
        scratch_shapes=[pltpu.VMEM((tm, D_MODEL), BF16), pltpu.VMEM((tm, D_MODEL), BF16)],
        compiler_params=_cparams(("arbitrary", "arbitrary")),
        name="ffn",
    )(x2, x2, mod_l, mod_l, g.reshape(1, D_MODEL), w1_all, w3_all, w2_all, final_g.reshape(1, D_MODEL))


def kernel(x, c, w_in, b_in, w_out, norm_mix_g, norm_ffn_g, ada_w, ada_b, lru_conv_w, lru_conv_b, lru_wa, lru_ba, lru_wx, lru_bx, lru_lambda, att_q_norm_g, att_k_norm_g, hy_conv_w, hy_conv_b, hy_w1, hy_b1, hy_w2, hy_b2, hy_w3, hy_sin_freq, hy_decay, hy_skip, ml_norm_g, ffn_w1, ffn_w3, ffn_w2, final_g):
    bsz = x.shape[0]
    assert x.shape == (bsz, SEQ, D_MODEL) and bsz % 2 == 0
    n = bsz * SEQ
    mod = _ada_all(c, ada_w, ada_b)

    consts = {k: v.astype(BF16) for k, v in _dft_constants().items()}
    ah5, ssq = _hyena_filter_stage(consts, hy_w1, hy_b1, hy_w2, hy_b2, hy_w3, hy_sin_freq, hy_decay)

    pad = D_IN_PAD - D_IN
    w_in_b = jnp.pad(w_in.astype(BF16), ((0, 0), (0, 0), (0, pad)))
    b_in_p = jnp.pad(b_in, ((0, 0), (0, pad))).reshape(DEPTH, 1, D_IN_PAD)
    w_out_b = w_out.astype(BF16)
    w1_b, w3_b, w2_b = ffn_w1.astype(BF16), ffn_w3.astype(BF16), ffn_w2.astype(BF16)

    x2 = x.reshape(n, D_MODEL)
    for l in range(DEPTH):
        proj2 = _inproj(x2, mod[l], norm_mix_g[l], w_in_b, b_in_p, l)
        proj3 = proj2.reshape(bsz, SEQ, D_IN_PAD)
        lru_f, lru_b = _mixer_rglru(proj3, lru_conv_w[l], lru_conv_b[l], lru_wa[l], lru_ba[l], lru_wx[l], lru_bx[l],
                                    lru_lambda[l])
        y_b = _mixer_attention(proj3, att_q_norm_g[l], att_k_norm_g[l])
        y_c = _mixer_hyena(proj3, hy_conv_w[l], hy_conv_b[l], hy_skip[l], ah5, ssq, l, consts)
        ml_f, ml_b = _mixer_mlstm(proj2, bsz)
        flat = lambda a: a.reshape(n, GROUP_W)
        x2 = _outproj(flat(lru_f), flat(lru_b), proj2, flat(y_b), flat(y_c), flat(ml_f), flat(ml_b), ml_norm_g[l],
                      w_out_b, l, x2, mod[l])
        x2 = _ffn(x2, mod[l], norm_ffn_g[l], w1_b, w3_b, w2_b, l, final_g, final=(l == DEPTH - 1))
    return x2.reshape(bsz, SEQ, D_MODEL)
```

```python
import functools
import math

import numpy as np
import jax
import jax.numpy as jnp
from jax import lax
from jax.experimental import pallas as pl
from jax.experimental.pallas import tpu as pltpu

F32 = jnp.float32
BF16 = jnp.bfloat16
HIGHEST = lax.Precision.HIGHEST

D_MODEL = 2048
SEQ = 4096
DEPTH = 2
GROUP_W = 512
LRU_BLOCKS = 8
LRU_C = 8.0
ATT_HEADS = 8
ATT_KV_HEADS = 2
ATT_GROUP = ATT_HEADS // ATT_KV_HEADS
ATT_HEAD_DIM = 64
ROPE_AXIS = ATT_HEAD_DIM // 2
ROPE_THETA = 10000.0
GRID_W = 64
ATT_HEAD_GROUP = 2
ATT_V_ROWS = 80
LOG2_E = math.log2(math.e)
HY_W = GROUP_W
HY_ORDER = 2
HY_BANDS = 8
HY_EMB = 2 * HY_BANDS + 1
HY_EMB_PAD = 32
HY_FFN = 64
ML_HEADS = 4
ML_HEAD_DIM = 128
ML_CHUNK = 128
D_FF = 5632
EPS = 1e-6
IN_SIZES = (512, 512, 512, 128, 128, 1536, 512, 512, 512, 512, 16)
D_IN = sum(IN_SIZES)
D_IN_PAD = 5632

COL_AX, COL_AG, COL_BQ, COL_BK, COL_BV, COL_CU = 0, 512, 1024, 1536, 1664, 1792
COL_DQ, COL_DK, COL_DV, COL_DO, COL_GATES = 3328, 3840, 4352, 4864, 5376
LANES = 128

FFT_N = 2 * SEQ
FFT_N1 = 64
FFT_N2 = 128
PITCH = 136

VMEM_LIMIT = 56 * 1024 * 1024


def _cparams(sem, vmem=VMEM_LIMIT):
    return pltpu.CompilerParams(dimension_semantics=sem, vmem_limit_bytes=vmem)


def _bdot(a, b):
    return jnp.dot(a.astype(BF16), b.astype(BF16), preferred_element_type=F32)


def _sigmoid(x):
    return 0.5 * jnp.tanh(0.5 * x) + 0.5


def _log_sigmoid(x):
    return jnp.minimum(x, 0.0) - jnp.log1p(jnp.exp(-jnp.abs(x)))


def _softplus(x):
    return jnp.maximum(x, 0.0) + jnp.log1p(jnp.exp(-jnp.abs(x)))


def _ada_kernel(c_ref, w_ref, b_ref, o_ref):
    c = c_ref[...]
    o_ref[...] = _bdot(c * _sigmoid(c), w_ref[...]) + b_ref[...]


def _ada_all(c, ada_w, ada_b):
    bsz = c.shape[0]
    rows = 8
    cp = jnp.zeros((rows, D_MODEL), F32).at[:bsz].set(c)
    tn = 2048
    out = pl.pallas_call(
        _ada_kernel,
        grid=(DEPTH, 6 * D_MODEL // tn),
        in_specs=[pl.BlockSpec((rows, D_MODEL), lambda l, j: (0, 0)),
                  pl.BlockSpec((None, D_MODEL, tn), lambda l, j: (l, 0, j)),
                  pl.BlockSpec((None, 1, tn), lambda l, j: (l, 0, j))],
        out_specs=pl.BlockSpec((None, rows, tn), lambda l, j: (l, 0, j)),
        out_shape=jax.ShapeDtypeStruct((DEPTH, rows, 6 * D_MODEL), F32),
        compiler_params=_cparams(("parallel", "parallel")),
        name="ada_mod",
    )(cp, ada_w, ada_b.reshape(DEPTH, 1, 6 * D_MODEL))
    return out[:, :bsz].reshape(DEPTH, bsz, 6, D_MODEL)


def _rms_mod(x, g, scale, shift):
    ms = jnp.mean(x * x, axis=-1, keepdims=True)
    return (x * lax.rsqrt(ms + EPS)) * (g * (1.0 + scale)) + shift


def _inproj_kernel(x_ref, m_ref, g_ref, w_ref, b_ref, o_ref, h_scr):
    @pl.when(pl.program_id(1) == 0)
    def _():
        h_scr[...] = _rms_mod(x_ref[...], g_ref[...], m_ref[1:2, :], m_ref[0:1, :]).astype(BF16)

    o_ref[...] = jnp.dot(h_scr[...], w_ref[...], preferred_element_type=F32) + b_ref[...]


def _inproj(x2, mod_l, g, w_all, b_all, layer):
    n = x2.shape[0]
    tm, tn = 1024, 1408
    per_b = SEQ // tm
    return pl.pallas_call(
        _inproj_kernel,
        grid=(n // tm, D_IN_PAD // tn),
        in_specs=[pl.BlockSpec((tm, D_MODEL), lambda i, j: (i, 0)),
                  pl.BlockSpec((None, 6, D_MODEL), lambda i, j: (i // per_b, 0, 0)),
                  pl.BlockSpec((1, D_MODEL), lambda i, j: (0, 0)),
                  pl.BlockSpec((None, D_MODEL, tn), lambda i, j: (layer, 0, j)),
                  pl.BlockSpec((None, 1, tn), lambda i, j: (layer, 0, j))],
        out_specs=pl.BlockSpec((tm, tn), lambda i, j: (i, j)),
        out_shape=jax.ShapeDtypeStruct((n, D_IN_PAD), F32),
        scratch_shapes=[pltpu.VMEM((tm, D_MODEL), BF16)],
        compiler_params=_cparams(("parallel", "arbitrary")),
        name="in_proj",
    )(x2, mod_l, g.reshape(1, D_MODEL), w_all, b_all)


def _shifted(ext, off, rows):
    total = ext.shape[0]
    if off == 0:
        return ext[8:8 + rows]
    return pltpu.roll(ext, (-off) % total, axis=0)[8:8 + rows]


def _dwconv_ext(ext, w_ref, b_ref, left, rows):
    out = b_ref[...]
    for j in range(w_ref.shape[0]):
        out = out + _shifted(ext, j - left, rows) * w_ref[j:j + 1, :]
    return out


def _dwconv_tile(x_ref, p_ref, n_ref, w_ref, b_ref, tile, n_tiles, left):
    prev = jnp.where(tile > 0, p_ref[...], 0.0)
    nxt = jnp.where(tile < n_tiles - 1, n_ref[...], 0.0)
    ext = jnp.concatenate([prev, x_ref[...], nxt], axis=0)
    return _dwconv_ext(ext, w_ref, b_ref, left, x_ref.shape[0])


def _dwconv_rows(ref, lo, rows, w_ref, b_ref, left):
    zeros = jnp.zeros((8, ref.shape[1]), F32)
    prev = ref[lo - 8:lo, :] if lo > 0 else zeros
    nxt = ref[lo + rows:lo + rows + 8, :] if lo + rows < SEQ else zeros
    ext = jnp.concatenate([prev, ref[lo:lo + rows, :], nxt], axis=0)
    return _dwconv_ext(ext, w_ref, b_ref, left, rows)


def _halo_specs(tile_rows, width, col_block, tile_of):
    r8 = tile_rows // 8
    last8 = SEQ // 8 - 1

    def main(b, t):
        return (b, tile_of(t), col_block)

    def prev(b, t):
        return (b, jnp.maximum(tile_of(t) * r8 - 1, 0), col_block)

    def nxt(b, t):
        return (b, jnp.minimum((tile_of(t) + 1) * r8, last8), col_block)

    return [pl.BlockSpec((None, tile_rows, width), main),
            pl.BlockSpec((None, 8, width), prev),
            pl.BlockSpec((None, 8, width), nxt)]


def _lru_gates(xc, wg, bg, lam):
    gates = _bdot(xc, wg) + bg
    r = _sigmoid(gates[:, :GROUP_W])
    i = _sigmoid(gates[:, GROUP_W:])
    log_a = (-LRU_C * _softplus(-lam)) * r
    a = jnp.exp(log_a)
    th = jnp.tanh(log_a)
    u = jnp.sqrt(-2.0 * th / (1.0 - th)) * (i * xc)
    return a, u


def _scan8(a, u, ridx, reverse):
    for k in (1, 2, 4):
        if reverse:
            keep = ridx < 8 - k
            sh = 8 - k
        else:
            keep = ridx >= k
            sh = k
        a_sh = jnp.where(keep, pltpu.roll(a, sh, axis=0), 1.0)
        u_sh = jnp.where(keep, pltpu.roll(u, sh, axis=0), 0.0)
        u = a * u_sh + u
        a = a * a_sh
    return a, u


def _lru_kernel(xf_ref, pf_ref, nf_ref, xb_ref, pb_ref, nb_ref, cw_ref, cb_ref, wg_ref, bg_ref, lam_ref,
                hf_ref, hb_ref, af_scr, uf_scr, ab_scr, ub_scr, cf_scr, cb_scr, c_scr, *, n_tiles):
    t = pl.program_id(1)
    n_chunks = af_scr.shape[0]
    xc = _dwconv_tile(xf_ref, pf_ref, nf_ref, cw_ref, cb_ref, t, n_tiles, 2)
    a, u = _lru_gates(xc, wg_ref[0], bg_ref[0], lam_ref[0])
    af_scr[...] = a.reshape(n_chunks, 8, GROUP_W)
    uf_scr[...] = u.reshape(n_chunks, 8, GROUP_W)
    xc = _dwconv_tile(xb_ref, pb_ref, nb_ref, cw_ref, cb_ref, n_tiles - 1 - t, n_tiles, 2)
    a, u = _lru_gates(xc, wg_ref[1], bg_ref[1], lam_ref[1])
    ab_scr[...] = a.reshape(n_chunks, 8, GROUP_W)
    ub_scr[...] = u.reshape(n_chunks, 8, GROUP_W)

    @pl.when(t == 0)
    def _():
        c_scr[...] = jnp.zeros_like(c_scr)

    ridx = lax.broadcasted_iota(jnp.int32, (8, GROUP_W), 0)

    def local(c, carry):
        a, u = _scan8(af_scr[c], uf_scr[c], ridx, False)
        af_scr[c] = a
        uf_scr[c] = u
        a, u = _scan8(ab_scr[c], ub_scr[c], ridx, True)
        ab_scr[c] = a
        ub_scr[c] = u
        return carry

    lax.fori_loop(0, n_chunks, local, 0, unroll=4)

    def chain(c, carry):
        cf, cb = carry
        cf_scr[c] = jnp.broadcast_to(cf, (8, GROUP_W))
        cf = af_scr[c][7:8, :] * cf + uf_scr[c][7:8, :]
        cr = n_chunks - 1 - c
        cb_scr[cr] = jnp.broadcast_to(cb, (8, GROUP_W))
        cb = ab_scr[cr][0:1, :] * cb + ub_scr[cr][0:1, :]
        return cf, cb

    cf, cb = lax.fori_loop(0, n_chunks, chain, (c_scr[0:1, :], c_scr[1:2, :]), unroll=4)
    c_scr[0:1, :] = cf
    c_scr[1:2, :] = cb

    def apply(c, carry):
        r0 = pl.multiple_of(c * 8, 8)
        hf_ref[pl.ds(r0, 8), :] = uf_scr[c] + af_scr[c] * cf_scr[c]
        hb_ref[pl.ds(r0, 8), :] = ub_scr[c] + ab_scr[c] * cb_scr[c]
        return carry

    lax.fori_loop(0, n_chunks, apply, 0, unroll=4)


def _block_diag(w):
    nb, k, j = w.shape
    eye = jnp.eye(nb, dtype=w.dtype)
    return jnp.einsum('nkj,nm->nkmj', w, eye).reshape(nb * k, nb * j)


def _mixer_rglru(proj3, conv_w, conv_b, wa, ba, wx, bx, lam):
    bsz = proj3.shape[0]
    ts = 512
    n_tiles = SEQ // ts
    wg = jnp.stack([jnp.concatenate([_block_diag(wa[d]), _block_diag(wx[d])], axis=1) for d in range(2)]).astype(BF16)
    bg = jnp.stack([jnp.concatenate([ba[d], bx[d]]).reshape(1, 2 * GROUP_W) for d in range(2)])
    small = lambda *shape: pl.BlockSpec(shape, lambda b, t: (0,) * len(shape))
    fwd_of = lambda t: t
    bwd_of = lambda t: n_tiles - 1 - t
    out = lambda tile_of: pl.BlockSpec((None, ts, GROUP_W), lambda b, t: (b, tile_of(t), 0))
    tile_scr = pltpu.VMEM((ts // 8, 8, GROUP_W), F32)
    return pl.pallas_call(
        functools.partial(_lru_kernel, n_tiles=n_tiles),
        grid=(bsz, n_tiles),
        in_specs=_halo_specs(ts, GROUP_W, COL_AX // GROUP_W, fwd_of) + _halo_specs(ts, GROUP_W, COL_AX // GROUP_W, bwd_of)
        + [small(4, GROUP_W), small(1, GROUP_W), small(2, GROUP_W, 2 * GROUP_W), small(2, 1, 2 * GROUP_W),
           small(2, 1, GROUP_W)],
        out_specs=[out(fwd_of), out(bwd_of)],
        out_shape=[jax.ShapeDtypeStruct((bsz, SEQ, GROUP_W), F32)] * 2,
        scratch_shapes=[tile_scr] * 6 + [pltpu.VMEM((8, GROUP_W), F32)],
        compiler_params=_cparams(("parallel", "arbitrary")),
        name="rglru",
    )(proj3, proj3, proj3, proj3, proj3, proj3, conv_w, conv_b.reshape(1, GROUP_W), wg, bg,
      lam.reshape(2, 1, GROUP_W))


def _split_dot(x, m_ref):
    hi = x.astype(BF16)
    lo = (x - hi.astype(F32)).astype(BF16)
    m = m_ref[...]
    return (jnp.dot(hi, m, preferred_element_type=F32) + jnp.dot(lo, m, preferred_element_type=F32))


def _norm_rope(x, gain, m_ref, cos, sins):
    width = x.shape[1]
    ms = _split_dot(x * x, m_ref)
    xn = x * lax.rsqrt(ms + EPS) * gain
    lane = lax.broadcasted_iota(jnp.int32, xn.shape, 1)
    first = (lane % ROPE_AXIS) < (ROPE_AXIS // 2)
    half = ROPE_AXIS // 2
    partner = jnp.where(first, pltpu.roll(xn, width - half, axis=1), pltpu.roll(xn, half, axis=1))
    return xn * cos + partner * sins


def _attn_prep_kernel(q_ref, kv_ref, gq_ref, gk_ref, mq_ref, mk_ref, cos_ref, sin_ref,
                      qo_ref, kt_ref, vt_ref):
    cos = cos_ref[...]
    sins = sin_ref[...]
    cos_q = jnp.concatenate([cos] * (GROUP_W // LANES), axis=1)
    sin_q = jnp.concatenate([sins] * (GROUP_W // LANES), axis=1)
    q = _norm_rope(q_ref[...], gq_ref[...], mq_ref, cos_q, sin_q)
    qo_ref[...] = (q * (ATT_HEAD_DIM ** -0.5 * LOG2_E)).astype(BF16)
    kv = kv_ref[...]
    kt_ref[...] = _norm_rope(kv[:, :LANES], gk_ref[...], mk_ref, cos, sins).T.astype(BF16)
    v_t = kv[:, LANES:].T
    row = lax.broadcasted_iota(jnp.int32, (ATT_V_ROWS - ATT_HEAD_DIM, v_t.shape[1]), 0)
    ones_rows = jnp.where(row == 0, 1.0, 0.0)
    for g in range(ATT_KV_HEADS):
        vt_ref[g] = jnp.concatenate([v_t[g * ATT_HEAD_DIM:(g + 1) * ATT_HEAD_DIM, :], ones_rows], axis=0).astype(BF16)


def _rope_tables():
    rows = SEQ // GRID_W
    row = jnp.repeat(jnp.arange(rows, dtype=F32), GRID_W)
    col = jnp.tile(jnp.arange(GRID_W, dtype=F32), rows)
    inv = ROPE_THETA ** (-jnp.arange(0, ROPE_AXIS, 2, dtype=F32) / ROPE_AXIS)
    ar = row[:, None] * inv
    ac = col[:, None] * inv
    ang = jnp.concatenate([ar, ar, ac, ac], axis=1)
    sign = jnp.concatenate([-jnp.ones((ROPE_AXIS // 2,), F32), jnp.ones((ROPE_AXIS // 2,), F32)] * 2)
    cos = jnp.tile(jnp.cos(ang), (1, 2))
    sins = jnp.tile(jnp.sin(ang) * sign, (1, 2))
    return cos, sins


def _head_mean_matrix(width):
    idx = np.arange(width) // ATT_HEAD_DIM
    return jnp.asarray((idx[:, None] == idx[None, :]).astype(np.float32) / ATT_HEAD_DIM, dtype=BF16)


def _attn_kernel(q_ref, kt_ref, vt_ref, o_ref):
    def scores(h):
        g = h // ATT_GROUP
        q = q_ref[:, h * ATT_HEAD_DIM:(h + 1) * ATT_HEAD_DIM]
        return jnp.dot(q, kt_ref[g * ATT_HEAD_DIM:(g + 1) * ATT_HEAD_DIM, :], preferred_element_type=F32)

    def probs(s):
        return jnp.exp2(s - jnp.max(s, axis=-1, keepdims=True)).astype(BF16)

    def values(h, p):
        o_aug = lax.dot_general(vt_ref[h // ATT_GROUP], p, (((1,), (1,)), ((), ())), preferred_element_type=F32)
        return (o_aug[:ATT_HEAD_DIM] / o_aug[ATT_HEAD_DIM:ATT_HEAD_DIM + 1]).T

    outs = []
    for h0 in range(0, ATT_HEADS, ATT_HEAD_GROUP):
        heads = range(h0, h0 + ATT_HEAD_GROUP)
        ps = [probs(s) for s in [scores(h) for h in heads]]
        outs += [values(h, p) for h, p in zip(heads, ps)]
    o_ref[...] = jnp.concatenate(outs, axis=1).astype(o_ref.dtype)


def _mixer_attention(proj3, q_g, k_g):
    bsz = proj3.shape[0]
    ts = 512
    cos, sins = _rope_tables()
    gq = jnp.tile(q_g, ATT_HEADS).reshape(1, GROUP_W)
    gk = jnp.tile(k_g, ATT_KV_HEADS).reshape(1, LANES)
    const = lambda shape: pl.BlockSpec(shape, lambda b, t: (0, 0))
    qp, kt, vt = pl.pallas_call(
        _attn_prep_kernel,
        grid=(bsz, SEQ // ts),
        in_specs=[pl.BlockSpec((None, ts, GROUP_W), lambda b, t: (b, t, COL_BQ // GROUP_W)),
                  pl.BlockSpec((None, ts, 2 * LANES), lambda b, t: (b, t, COL_BK // (2 * LANES))),
                  const((1, GROUP_W)), const((1, LANES)), const((GROUP_W, GROUP_W)), const((LANES, LANES)),
                  pl.BlockSpec((ts, LANES), lambda b, t: (t, 0)),
                  pl.BlockSpec((ts, LANES), lambda b, t: (t, 0))],
        out_specs=[pl.BlockSpec((None, ts, GROUP_W), lambda b, t: (b, t, 0)),
                   pl.BlockSpec((None, LANES, ts), lambda b, t: (b, 0, t)),
                   pl.BlockSpec((None, ATT_KV_HEADS, ATT_V_ROWS, ts), lambda b, t: (b, 0, 0, t))],
        out_shape=[jax.ShapeDtypeStruct((bsz, SEQ, GROUP_W), BF16),
                   jax.ShapeDtypeStruct((bsz, LANES, SEQ), BF16),
                   jax.ShapeDtypeStruct((bsz, ATT_KV_HEADS, ATT_V_ROWS, SEQ), BF16)],
        compiler_params=_cparams(("parallel", "parallel")),
        name="attn_prep",
    )(proj3, proj3, gq, gk, _head_mean_matrix(GROUP_W), _head_mean_matrix(LANES), cos, sins)

    tq = 512
    return pl.pallas_call(
        _attn_kernel,
        grid=(bsz, SEQ // tq),
        in_specs=[pl.BlockSpec((None, tq, GROUP_W), lambda b, t: (b, t, 0)),
                  pl.BlockSpec((None, LANES, SEQ), lambda b, t: (b, 0, 0)),
                  pl.BlockSpec((None, ATT_KV_HEADS, ATT_V_ROWS, SEQ), lambda b, t: (b, 0, 0, 0))],
        out_specs=pl.BlockSpec((None, tq, GROUP_W), lambda b, t: (b, t, 0)),
        out_shape=jax.ShapeDtypeStruct((bsz, SEQ, GROUP_W), BF16),
        compiler_params=_cparams(("parallel", "parallel")),
        name="attention",
    )(qp, kt, vt)


def _hy_features():
    L = SEQ
    pos = jnp.arange(L, dtype=F32)
    t = pos / max(L - 1, 1)
    bands = jnp.linspace(1e-4, HY_BANDS - 1, HY_BANDS, dtype=F32)
    ang = (2.0 * math.pi * pos / L)[:, None] * bands
    feat = jnp.concatenate([t[:, None], jnp.cos(ang), -jnp.sin(ang)], axis=-1)
    feat = jnp.pad(feat, ((0, 0), (0, HY_EMB_PAD - HY_EMB)))
    rev_idx = np.concatenate([[0], np.arange(L - 1, 0, -1)])
    return feat, feat[rev_idx]


def _hy_mlp(feat, feat_t, w1t_ref, b1_ref, w2t_ref, b2_ref, w3_ref, sf_ref, dec_ref):
    sf = sf_ref[...]
    h = jnp.sin(sf * (jnp.dot(w1t_ref[...], feat_t, precision=HIGHEST, preferred_element_type=F32) + b1_ref[...]))
    h = jnp.sin(sf * (jnp.dot(w2t_ref[...], h, precision=HIGHEST, preferred_element_type=F32) + b2_ref[...]))
    out = jnp.dot(h.T, w3_ref[...], precision=HIGHEST, preferred_element_type=F32)
    return out * jnp.exp(-feat[:, 0:1] * jnp.abs(dec_ref[...]))


def _hy_filter_kernel(ff_ref, fft_ref, fr_ref, frt_ref, w1t_ref, b1_ref, w2t_ref, b2_ref, w3f_ref, w3b_ref, sf_ref,
                      decf_ref, decb_ref, of_ref, ob_ref, ssq_ref):
    i = pl.program_id(1)
    hf = _hy_mlp(ff_ref[...], fft_ref[...], w1t_ref, b1_ref, w2t_ref, b2_ref, w3f_ref, sf_ref, decf_ref)
    hb = _hy_mlp(fr_ref[...], frt_ref[...], w1t_ref, b1_ref, w2t_ref, b2_ref, w3b_ref, sf_ref, decb_ref)

    @pl.when(i == 0)
    def _():
        ssq_ref[...] = jnp.zeros_like(ssq_ref)

    ssq_ref[0:1, :] += jnp.sum(hf * hf + hb * hb, axis=0, keepdims=True)
    of_ref[...] = hf
    row = lax.broadcasted_iota(jnp.int32, hb.shape, 0)
    ob_ref[...] = jnp.where(jnp.logical_and(i == 0, row == 0), 0.0, hb)


def _hy_filters(hy_w1, hy_b1, hy_w2, hy_b2, hy_w3, hy_sin_freq, hy_decay):
    feat, feat_rev = _hy_features()
    tr = 512
    cw = HY_ORDER * HY_W
    w1t = jnp.transpose(jnp.pad(hy_w1, ((0, 0), (0, HY_EMB_PAD - HY_EMB), (0, 0))), (0, 2, 1))
    w2t = jnp.transpose(hy_w2, (0, 2, 1))
    w3 = hy_w3.reshape(DEPTH, HY_FFN, HY_ORDER, 2, HY_W)
    dec = hy_decay.reshape(DEPTH, HY_ORDER, 2, HY_W)
    w3f = w3[:, :, :, 0].reshape(DEPTH, HY_FFN, cw)
    w3b = w3[:, :, :, 1].reshape(DEPTH, HY_FFN, cw)
    decf = dec[:, :, 0].reshape(DEPTH, 1, cw)
    decb = dec[:, :, 1].reshape(DEPTH, 1, cw)
    col = lambda a: a.reshape(DEPTH, HY_FFN, 1)
    row_spec = pl.BlockSpec((tr, HY_EMB_PAD), lambda l, i: (i, 0))
    rowt_spec = pl.BlockSpec((HY_EMB_PAD, tr), lambda l, i: (0, i))
    per_layer = lambda a, b: pl.BlockSpec((None, a, b), lambda l, i: (l, 0, 0))
    out_spec = pl.BlockSpec((None, tr, cw), lambda l, i: (l, i, 0))
    return pl.pallas_call(
        _hy_filter_kernel,
        grid=(DEPTH, SEQ // tr),
        in_specs=[row_spec, rowt_spec, row_spec, rowt_spec, per_layer(HY_FFN, HY_EMB_PAD), per_layer(HY_FFN, 1),
                  per_layer(HY_FFN, HY_FFN), per_layer(HY_FFN, 1), per_layer(HY_FFN, cw), per_layer(HY_FFN, cw),
                  per_layer(HY_FFN, 1), per_layer(1, cw), per_layer(1, cw)],
        out_specs=[out_spec, out_spec, per_layer(8, cw)],
        out_shape=[jax.ShapeDtypeStruct((DEPTH, SEQ, cw), F32)] * 2 + [jax.ShapeDtypeStruct((DEPTH, 8, cw), F32)],
        compiler_params=_cparams(("arbitrary", "arbitrary")),
        name="hyena_filter",
    )(feat, feat.T, feat_rev, feat_rev.T, w1t, col(hy_b1), w2t, col(hy_b2), w3f, w3b, col(hy_sin_freq), decf, decb)


def _dft_constants():
    n1 = np.arange(FFT_N1)
    n2 = np.arange(FFT_N2)
    f1 = np.exp(-2j * np.pi * np.outer(n1, n1) / FFT_N1)
    stack = lambda m: np.concatenate([m.real, m.imag], axis=0)
    half = FFT_N1 // 2
    sig_l = stack(f1[:, :half])
    sig_r = np.concatenate([-f1[:, :half].imag, f1[:, :half].real], axis=0)
    fil_r = stack(f1[:, half:])
    f2 = np.exp(-2j * np.pi * np.outer(n2, n2) / FFT_N2)
    tw = np.exp(-2j * np.pi * np.outer(n1, n2) / FFT_N)
    fwd = f2[None, :, :] * tw[:, None, :]
    inv = np.conj(np.transpose(fwd, (0, 2, 1))) / FFT_N
    block = lambda m: np.concatenate([np.concatenate([m.real, -m.imag], axis=2),
                                      np.concatenate([m.imag, m.real], axis=2)], axis=1)
    g1 = np.conj(f1[:half, :])
    out_l = stack(g1)
    out_r = np.concatenate([-g1.imag, g1.real], axis=0)
    as32 = lambda a: jnp.asarray(a.astype(np.float32))
    return dict(sig_l=as32(sig_l), sig_r=as32(sig_r), fil_r=as32(fil_r), fwd=as32(block(fwd)),
                inv=as32(block(inv)), out_l=as32(out_l), out_r=as32(out_r))


_N2_GROUP = 8


def _dft_a_kernel(*refs, conv):
    if conv:
        u0_ref, u1_ref, cw_ref, cb_ref, ml_ref, mr_ref, o_ref, xs, ysc = refs
    else:
        u0_ref, u1_ref, ml_ref, mr_ref, o_ref, xs, ysc = refs
    half = FFT_N1 // 2
    for r, u_ref in enumerate((u0_ref, u1_ref)):
        for n1 in range(half):
            lo = FFT_N2 * n1
            rows = _dwconv_rows(u_ref, lo, FFT_N2, cw_ref, cb_ref, 1) if conv else u_ref[lo:lo + FFT_N2, :]
            xs[r, PITCH * n1:PITCH * n1 + FFT_N2, :] = rows
    ml = ml_ref[...]
    mr = mr_ref[...]

    def body(g, carry):
        n2 = g * _N2_GROUP
        x0 = jnp.concatenate([xs[0, pl.ds(n2 + i, half, stride=PITCH), :] for i in range(_N2_GROUP)], axis=1)
        x1 = jnp.concatenate([xs[1, pl.ds(n2 + i, half, stride=PITCH), :] for i in range(_N2_GROUP)], axis=1)
        y = _bdot(ml, x0) + _bdot(mr, x1)
        for i in range(_N2_GROUP):
            ysc[pl.ds(n2 + i, 2 * FFT_N1, stride=PITCH), :] = y[:, LANES * i:LANES * (i + 1)]
        return carry

    lax.fori_loop(0, FFT_N2 // _N2_GROUP, body, 0)
    for row in range(2 * FFT_N1):
        o_ref[FFT_N2 * row:FFT_N2 * (row + 1), :] = ysc[PITCH * row:PITCH * row + FFT_N2, :].astype(o_ref.dtype)


def _dft_a(ml, mr, srcs, groups, n_slabs, conv_args=None):
    conv = conv_args is not None
    slab = lambda arr_map: pl.BlockSpec((None, SEQ, LANES), arr_map)
    mspec = pl.BlockSpec((2 * FFT_N1, FFT_N1 // 2), lambda g, s: (0, 0))
    in_specs = [slab(srcs[0][1]), slab(srcs[1][1])]
    args = [srcs[0][0], srcs[1][0]]
    if conv:
        cw, cb, col0 = conv_args
        in_specs += [pl.BlockSpec((cw.shape[0], LANES), lambda g, s: (0, col0 + s)),
                     pl.BlockSpec((1, LANES), lambda g, s: (0, col0 + s))]
        args += [cw, cb]
    rows = 2 * FFT_N1 * FFT_N2
    return pl.pallas_call(
        functools.partial(_dft_a_kernel, conv=conv),
        grid=(groups, n_slabs),
        in_specs=in_specs + [mspec, mspec],
        out_specs=pl.BlockSpec((None, rows, LANES), lambda g, s: (g, 0, s)),
        out_shape=jax.ShapeDtypeStruct((groups, rows, n_slabs * LANES), BF16),
        scratch_shapes=[pltpu.VMEM((2, (FFT_N1 // 2) * PITCH, LANES), F32),
                        pltpu.VMEM((2 * FFT_N1 * PITCH, LANES), F32)],
        compiler_params=_cparams(("parallel", "parallel")),
        name="dft_a",
    )(*args, ml, mr)


def _dft_mid_kernel(f_ref, g_ref, ah_ref, ssq_ref, a_ref, o_ref):
    scale = lax.rsqrt(ssq_ref[0:1, :] + EPS)
    for kk in range(f_ref.shape[0]):
        f = f_ref[kk]
        g = g_ref[kk]
        h = _bdot(f, jnp.concatenate([ah_ref[0, kk], ah_ref[1, kk]], axis=0)) * scale
        hr = h[:FFT_N2]
        hi = h[FFT_N2:]
        for p in range(a_ref.shape[0]):
            y = _bdot(f, jnp.concatenate([a_ref[p, 0, kk], a_ref[p, 1, kk]], axis=0))
            yr = y[:FFT_N2]
            yi = y[FFT_N2:]
            z = jnp.concatenate([yr * hr - yi * hi, yr * hi + yi * hr], axis=0)
            w = _bdot(g, z)
            o_ref[p, 0, kk] = w[:FFT_N2].astype(o_ref.dtype)
            o_ref[p, 1, kk] = w[FFT_N2:].astype(o_ref.dtype)


def _dft_mid(fwd, inv, ah5, ssq, layer, order, a5):
    pairs = a5.shape[0]
    kb = 2
    blk = pl.BlockSpec((pairs, 2, kb, FFT_N2, HY_W), lambda k: (0, 0, k, 0, 0))
    mat = pl.BlockSpec((kb, 2 * FFT_N2, 2 * FFT_N2), lambda k: (k, 0, 0))
    return pl.pallas_call(
        _dft_mid_kernel,
        grid=(FFT_N1 // kb,),
        in_specs=[mat, mat,
                  pl.BlockSpec((None, 2, kb, FFT_N2, HY_W), lambda k: (layer, 0, k, 0, order)),
                  pl.BlockSpec((None, 8, HY_W), lambda k: (layer, 0, order)),
                  blk],
        out_specs=blk,
        out_shape=jax.ShapeDtypeStruct(a5.shape, BF16),
        compiler_params=_cparams(("parallel",)),
        name="dft_mid",
    )(fwd, inv, ah5, ssq, a5)


def _dft_c_kernel(*refs, u_conv):
    if u_conv:
        (b_ref, u0_ref, u1_ref, g0_ref, g1_ref, ucw_ref, ucb_ref, gcw_ref, gcb_ref, ml_ref, mr_ref, skip_ref,
         o_ref, bs, ys) = refs
    else:
        b_ref, u0_ref, u1_ref, g0_ref, g1_ref, gcw_ref, gcb_ref, ml_ref, mr_ref, skip_ref, o_ref, bs, ys = refs
    half = FFT_N1 // 2
    ml = ml_ref[...]
    mr = mr_ref[...]
    im0 = FFT_N1 * PITCH
    for row in range(2 * FFT_N1):
        bs[PITCH * row:PITCH * row + FFT_N2, :] = b_ref[FFT_N2 * row:FFT_N2 * (row + 1), :].astype(F32)

    def body(g, carry):
        n2 = g * _N2_GROUP
        br = jnp.concatenate([bs[pl.ds(n2 + i, FFT_N1, stride=PITCH), :] for i in range(_N2_GROUP)], axis=1)
        bi = jnp.concatenate([bs[pl.ds(im0 + n2 + i, FFT_N1, stride=PITCH), :] for i in range(_N2_GROUP)], axis=1)
        y = _bdot(ml, br) + _bdot(mr, bi)
        for i in range(_N2_GROUP):
            ys[0, pl.ds(n2 + i, half, stride=PITCH), :] = y[:half, LANES * i:LANES * (i + 1)]
            ys[1, pl.ds(n2 + i, half, stride=PITCH), :] = y[half:, LANES * i:LANES * (i + 1)]
        return carry

    lax.fori_loop(0, FFT_N2 // _N2_GROUP, body, 0)
    skip = skip_ref[...]
    for r, (u_ref, g_ref) in enumerate(((u0_ref, g0_ref), (u1_ref, g1_ref))):
        for n1 in range(half):
            lo = FFT_N2 * n1
            u = _dwconv_rows(u_ref, lo, FFT_N2, ucw_ref, ucb_ref, 1) if u_conv else u_ref[lo:lo + FFT_N2, :]
            gate = _dwconv_rows(g_ref, lo, FFT_N2, gcw_ref, gcb_ref, 1)
            conv = ys[r, PITCH * n1:PITCH * n1 + FFT_N2, :]
            o_ref[r, lo:lo + FFT_N2, :] = (gate * (conv + u * skip)).astype(o_ref.dtype)


def _dft_c(ml, mr, b3, u_src, gate_src, u_conv_args, gate_conv_args, skip, out_dtype):
    pairs = b3.shape[0]
    n_slabs = HY_W // LANES
    u_conv = u_conv_args is not None
    slab = lambda m: pl.BlockSpec((None, SEQ, LANES), m)
    wspecs = lambda cw, col0: [pl.BlockSpec((cw.shape[0], LANES), lambda p, s: (0, col0 + s)),
                               pl.BlockSpec((1, LANES), lambda p, s: (0, col0 + s))]
    in_specs = [pl.BlockSpec((None, 2 * FFT_N1 * FFT_N2, LANES), lambda p, s: (p, 0, s)),
                slab(u_src[1]), slab(u_src[2]), slab(gate_src[1]), slab(gate_src[2])]
    args = [b3, u_src[0], u_src[0], gate_src[0], gate_src[0]]
    if u_conv:
        in_specs += wspecs(u_conv_args[0], u_conv_args[2])
        args += [u_conv_args[0], u_conv_args[1]]
    in_specs += wspecs(gate_conv_args[0], gate_conv_args[2])
    args += [gate_conv_args[0], gate_conv_args[1]]
    mspec = pl.BlockSpec((FFT_N1, FFT_N1), lambda p, s: (0, 0))
    in_specs += [mspec, mspec, pl.BlockSpec((1, LANES), lambda p, s: (0, s))]
    args += [ml, mr, skip.reshape(1, HY_W)]
    return pl.pallas_call(
        functools.partial(_dft_c_kernel, u_conv=u_conv),
        grid=(pairs, n_slabs),
        in_specs=in_specs,
        out_specs=pl.BlockSpec((2, SEQ, LANES), lambda p, s: (p, 0, s)),
        out_shape=jax.ShapeDtypeStruct((2 * pairs, SEQ, HY_W), out_dtype),
        scratch_shapes=[pltpu.VMEM((2 * FFT_N1 * PITCH, LANES), F32),
                        pltpu.VMEM((2, (FFT_N1 // 2) * PITCH, LANES), F32)],
        compiler_params=_cparams(("parallel", "parallel")),
        name="dft_c",
    )(*args)


def _mixer_hyena(proj3, conv_w, conv_b, skip, ah5, ssq, layer, consts):
    bsz = proj3.shape[0]
    pairs = bsz // 2
    n_slabs = HY_W // LANES
    cb = conv_b.reshape(1, 3 * HY_W)
    col = lambda which: (COL_CU + which * HY_W) // LANES
    proj_map = lambda which, odd: (lambda p, s: (2 * p + odd, 0, col(which) + s))
    plain_map = lambda odd: (lambda p, s: (2 * p + odd, 0, s))
    conv_args = lambda which: (conv_w, cb, which * n_slabs)
    a5_shape = (pairs, 2, FFT_N1, FFT_N2, HY_W)

    a = _dft_a(consts['sig_l'], consts['sig_r'], [(proj3, proj_map(0, 0)), (proj3, proj_map(0, 1))],
               pairs, n_slabs, conv_args(0))
    b = _dft_mid(consts['fwd'], consts['inv'], ah5, ssq, layer, 0, a.reshape(a5_shape))
    z1 = _dft_c(consts['out_l'], consts['out_r'], b.reshape(a.shape),
                (proj3, proj_map(0, 0), proj_map(0, 1)), (proj3, proj_map(1, 0), proj_map(1, 1)),
                conv_args(0), conv_args(1), skip[0], F32)
    a = _dft_a(consts['sig_l'], consts['sig_r'], [(z1, plain_map(0)), (z1, plain_map(1))], pairs, n_slabs)
    b = _dft_mid(consts['fwd'], consts['inv'], ah5, ssq, layer, 1, a.reshape(a5_shape))
    return _dft_c(consts['out_l'], consts['out_r'], b.reshape(a.shape),
                  (z1, plain_map(0), plain_map(1)), (proj3, proj_map(2, 0), proj_map(2, 1)),
                  None, conv_args(2), skip[1], BF16)


def _hyena_filter_stage(consts, hy_w1, hy_b1, hy_w2, hy_b2, hy_w3, hy_sin_freq, hy_decay):
    hf, hb, ssq = _hy_filters(hy_w1, hy_b1, hy_w2, hy_b2, hy_w3, hy_sin_freq, hy_decay)
    cw = HY_ORDER * HY_W
    fmap = lambda l, s: (l, 0, s)
    ha = _dft_a(consts['sig_l'], consts['fil_r'], [(hf, fmap), (hb, fmap)], DEPTH, cw // LANES)
    return ha.reshape(DEPTH, 2, FFT_N1, FFT_N2, cw), ssq


def _mlstm_chunk(q_ref, k_ref, v_ref, gc_ref, c_scr, m_scr, reverse, i_off, f_off, state_off):
    ch = ML_CHUNK
    ri = lax.broadcasted_iota(jnp.int32, (ch, ch), 0)
    ci = lax.broadcasted_iota(jnp.int32, (ch, ch), 1)
    tri = (ci >= ri) if reverse else (ci <= ri)
    tri_f = tri.astype(F32)
    gc = gc_ref[...]
    gr = gc.T
    b_col = jnp.dot(tri_f, _log_sigmoid(gc), precision=HIGHEST, preferred_element_type=F32)
    b_row = lax.dot_general(_log_sigmoid(gr[:4 * ML_HEADS, :]), tri_f, (((1,), (1,)), ((), ())), precision=HIGHEST,
                            preferred_element_type=F32)
    last = 0 if reverse else ch - 1
    lane = lax.broadcasted_iota(jnp.int32, (ch, ML_HEAD_DIM), 1)
    ones_col = jnp.where(lane == 0, 1.0, 0.0).astype(BF16)

    outs = []
    for h in range(ML_HEADS):
        sl = slice(h * ML_HEAD_DIM, (h + 1) * ML_HEAD_DIM)
        st = state_off + h
        q = (q_ref[:, sl] * (ML_HEAD_DIM ** -0.5)).astype(BF16)
        k = k_ref[:, sl]
        v_aug = jnp.concatenate([v_ref[:, sl].astype(BF16), ones_col], axis=1)
        bc = b_col[:, f_off + h:f_off + h + 1]
        lic = gc[:, i_off + h:i_off + h + 1]
        br = b_row[f_off + h:f_off + h + 1, :]
        lir = gr[i_off + h:i_off + h + 1, :]
        b_tot = bc[last:last + 1, :]
        d = jnp.where(tri, bc - br + lir, -jnp.inf)
        w_end = b_tot - bc + lic
        m_loc = jnp.max(w_end, axis=0, keepdims=True)
        e_end = jnp.exp(w_end - m_loc)
        m_prev = m_scr[st:st + 1, 0:1]
        c_prev = c_scr[st]
        m_inter = bc + m_prev
        m_t = jnp.maximum(m_inter, jnp.max(d, axis=-1, keepdims=True))
        e_inter = jnp.exp(m_inter - m_t)
        qk = lax.dot_general(q, k.astype(BF16), (((1,), (1,)), ((), ())), preferred_element_type=F32)
        s = qk * jnp.exp(d - m_t)
        nd = _bdot(s, v_aug) + e_inter * _bdot(q, c_prev)
        num = nd[:, :ML_HEAD_DIM]
        den = nd[:, ML_HEAD_DIM:ML_HEAD_DIM + 1]
        outs.append(num / jnp.maximum(jnp.abs(den), jnp.exp(-m_t)))
        m_new = jnp.maximum(b_tot + m_prev, m_loc)
        decay = jnp.exp(b_tot + m_prev - m_new)
        gain = jnp.exp(m_loc - m_new)
        dc = lax.dot_general((k * e_end).astype(BF16), v_aug, (((0,), (0,)), ((), ())),
                             preferred_element_type=F32)
        c_scr[st] = decay * c_prev + gain * dc
        m_scr[st:st + 1, :] = jnp.broadcast_to(m_new, (1, LANES))
    return outs


ML_BATCH_ROWS = 1


def _mlstm_kernel(*refs):
    n_in = 8 * ML_BATCH_ROWS
    hf_ref, hb_ref, c_scr, m_scr = refs[n_in:]

    @pl.when(pl.program_id(1) == 0)
    def _():
        c_scr[...] = jnp.zeros_like(c_scr)
        m_scr[...] = jnp.zeros_like(m_scr)

    for r in range(ML_BATCH_ROWS):
        qf_ref, kf_ref, vf_ref, gf_ref, qb_ref, kb_ref, vb_ref, gb_ref = refs[8 * r:8 * r + 8]
        st = 2 * ML_HEADS * r
        outs_f = _mlstm_chunk(qf_ref, kf_ref, vf_ref, gf_ref, c_scr, m_scr, False, 0, ML_HEADS, st)
        outs_b = _mlstm_chunk(qb_ref, kb_ref, vb_ref, gb_ref, c_scr, m_scr, True, 2 * ML_HEADS, 3 * ML_HEADS,
                              st + ML_HEADS)
        hf_ref[r] = jnp.concatenate(outs_f, axis=1)
        hb_ref[r] = jnp.concatenate(outs_b, axis=1)


def _mixer_mlstm(proj2, bsz):
    nc = SEQ // ML_CHUNK
    rows = ML_BATCH_ROWS

    def specs(r, chunk_of):
        def at(col, width):
            def index(i, j):
                return pl.multiple_of((i * rows + r) * SEQ + chunk_of(j) * ML_CHUNK, ML_CHUNK), col
            return pl.BlockSpec((pl.Element(ML_CHUNK), pl.Element(width)), index)
        return [at(COL_DQ, GROUP_W), at(COL_DK, GROUP_W), at(COL_DV, GROUP_W), at(COL_GATES, LANES)]

    fwd_of = lambda j: j
    bwd_of = lambda j: nc - 1 - j
    in_specs = []
    for r in range(rows):
        in_specs += specs(r, fwd_of) + specs(r, bwd_of)
    out = lambda chunk_of: pl.BlockSpec((rows, ML_CHUNK, GROUP_W), lambda i, j: (i, chunk_of(j), 0))
    n_state = 2 * ML_HEADS * rows
    return pl.pallas_call(
        _mlstm_kernel,
        grid=(bsz // rows, nc),
        in_specs=in_specs,
        out_specs=[out(fwd_of), out(bwd_of)],
        out_shape=[jax.ShapeDtypeStruct((bsz, SEQ, GROUP_W), F32)] * 2,
        scratch_shapes=[pltpu.VMEM((n_state, ML_HEAD_DIM, 2 * ML_HEAD_DIM), F32), pltpu.VMEM((n_state, LANES), F32)],
        compiler_params=_cparams(("parallel", "arbitrary")),
        name="mlstm",
    )(*([proj2] * (8 * rows)))


def _outproj_kernel(lf_ref, lb_ref, ga_ref, yb_ref, yc_ref, mf_ref, mb_ref, o_ref, mg_ref, w_ref, x_ref, m_ref,
                    out_ref):
    ya = (jax.nn.gelu(ga_ref[...]) * (lf_ref[...] + lb_ref[...])).astype(BF16)
    normed = []
    for h in range(ML_HEADS):
        sl = slice(h * ML_HEAD_DIM, (h + 1) * ML_HEAD_DIM)
        hh = mf_ref[:, sl] + mb_ref[:, sl]
        ms = jnp.mean(hh * hh, axis=-1, keepdims=True)
        normed.append(hh * lax.rsqrt(ms + EPS) * mg_ref[:, sl])
    yd = (_sigmoid(o_ref[...]) * jnp.concatenate(normed, axis=1)).astype(BF16)
    acc = jnp.dot(ya, w_ref[0:GROUP_W, :], preferred_element_type=F32)
    acc += jnp.dot(yb_ref[...], w_ref[GROUP_W:2 * GROUP_W, :], preferred_element_type=F32)
    acc += jnp.dot(yc_ref[...], w_ref[2 * GROUP_W:3 * GROUP_W, :], preferred_element_type=F32)
    acc += jnp.dot(yd, w_ref[3 * GROUP_W:, :], preferred_element_type=F32)
    out_ref[...] = x_ref[...] + m_ref[2:3, :] * acc


def _outproj(lru_f, lru_b, proj2, y_b, y_c, ml_f, ml_b, ml_g, w_all, layer, x2, mod_l):
    n = x2.shape[0]
    tm = 512
    per_b = SEQ // tm
    grp = pl.BlockSpec((tm, GROUP_W), lambda i: (i, 0))
    return pl.pallas_call(
        _outproj_kernel,
        grid=(n // tm,),
        in_specs=[grp, grp, pl.BlockSpec((tm, GROUP_W), lambda i: (i, COL_AG // GROUP_W)), grp, grp, grp, grp,
                  pl.BlockSpec((pl.Element(tm), pl.Element(GROUP_W)), lambda i: (pl.multiple_of(i * tm, tm), COL_DO)),
                  pl.BlockSpec((1, GROUP_W), lambda i: (0, 0)),
                  pl.BlockSpec((None, D_MODEL, D_MODEL), lambda i: (layer, 0, 0)),
                  pl.BlockSpec((tm, D_MODEL), lambda i: (i, 0)),
                  pl.BlockSpec((None, 6, D_MODEL), lambda i: (i // per_b, 0, 0))],
        out_specs=pl.BlockSpec((tm, D_MODEL), lambda i: (i, 0)),
        out_shape=jax.ShapeDtypeStruct((n, D_MODEL), F32),
        compiler_params=_cparams(("parallel",)),
        name="out_proj",
    )(lru_f, lru_b, proj2, y_b, y_c, ml_f, ml_b, proj2, ml_g.reshape(1, GROUP_W), w_all, x2, mod_l)


def _ffn_kernel(x_ref, m_ref, g_ref, w1_ref, w3_ref, w2_ref, fg_ref, o_ref, h_scr, *, final):
    j = pl.program_id(1)

    @pl.when(j == 0)
    def _():
        h_scr[...] = _rms_mod(x_ref[...], g_ref[...], m_ref[4:5, :], m_ref[3:4, :]).astype(BF16)
        o_ref[...] = jnp.zeros_like(o_ref)

    h = h_scr[...]
    a = jnp.dot(h, w1_ref[...], preferred_element_type=F32)
    b = jnp.dot(h, w3_ref[...], preferred_element_type=F32)
    act = (a * _sigmoid(a)) * b
    o_ref[...] += jnp.dot(act.astype(BF16), w2_ref[...], preferred_element_type=F32)

    @pl.when(j == pl.num_programs(1) - 1)
    def _():
        y = x_ref[...] + m_ref[5:6, :] * o_ref[...]
        if final:
            ms = jnp.mean(y * y, axis=-1, keepdims=True)
            y = y * lax.rsqrt(ms + EPS) * fg_ref[...]
        o_ref[...] = y


def _ffn(x2, mod_l, g, w1_all, w3_all, w2_all, layer, final_g, final):
    n = x2.shape[0]
    tm, tf = 512, 512
    per_b = SEQ // tm
    row = lambda: pl.BlockSpec((1, D_MODEL), lambda i, j: (0, 0))
    return pl.pallas_call(
        functools.partial(_ffn_kernel, final=final),
        grid=(n // tm, D_FF // tf),
        in_specs=[pl.BlockSpec((tm, D_MODEL), lambda i, j: (i, 0)),
                  pl.BlockSpec((None, 6, D_MODEL), lambda i, j: (i // per_b, 0, 0)),
                  row(),
                  pl.BlockSpec((None, D_MODEL, tf), lambda i, j: (layer, 0, j)),
                  pl.BlockSpec((None, D_MODEL, tf), lambda i, j: (layer, 0, j)),
                  pl.BlockSpec((None, tf, D_MODEL), lambda i, j: (layer, j, 0)),
                  row()],
        out_specs=pl.BlockSpec((tm, D_MODEL), lambda i, j: (i, 0)),
        out_shape=jax.ShapeDtypeStruct((n, D_MODEL), F32),
        scratch_shapes=[pltpu.VMEM((tm, D_MODEL), BF16)],
        compiler_params=_cparams(("parallel", "arbitrary")),
        name="ffn",
    )(x2, mod_l, g.reshape(1, D_MODEL), w1_all, w3_all, w2_all, final_g.reshape(1, D_MODEL))


def kernel(x, c, w_in, b_in, w_out, norm_mix_g, norm_ffn_g, ada_w, ada_b, lru_conv_w, lru_conv_b, lru_wa, lru_ba, lru_wx, lru_bx, lru_lambda, att_q_norm_g, att_k_norm_g, hy_conv_w, hy_conv_b, hy_w1, hy_b1, hy_w2, hy_b2, hy_w3, hy_sin_freq, hy_decay, hy_skip, ml_norm_g, ffn_w1, ffn_w3, ffn_w2, final_g):
    bsz = x.shape[0]
    assert x.shape == (bsz, SEQ, D_MODEL) and bsz % 2 == 0
    n = bsz * SEQ
    mod = _ada_all(c, ada_w, ada_b)

    consts = {k: v.astype(BF16) for k, v in _dft_constants().items()}
    ah5, ssq = _hyena_filter_stage(consts, hy_w1, hy_b1, hy_w2, hy_b2, hy_w3, hy_sin_freq, hy_decay)

    pad = D_IN_PAD - D_IN
    w_in_b = jnp.pad(w_in.astype(BF16), ((0, 0), (0, 0), (0, pad)))
    b_in_p = jnp.pad(b_in, ((0, 0), (0, pad))).reshape(DEPTH, 1, D_IN_PAD)
    w_out_b = w_out.astype(BF16)
    w1_b, w3_b, w2_b = ffn_w1.astype(BF16), ffn_w3.astype(BF16), ffn_w2.astype(BF16)

    x2 = x.reshape(n, D_MODEL)
    for l in range(DEPTH):
        proj2 = _inproj(x2, mod[l], norm_mix_g[l], w_in_b, b_in_p, l)
        proj3 = proj2.reshape(bsz, SEQ, D_IN_PAD)
        lru_f, lru_b = _mixer_rglru(proj3, lru_conv_w[l], lru_conv_b[l], lru_wa[l], lru_ba[l], lru_wx[l], lru_bx[l],
                                    lru_lambda[l])
        y_b = _mixer_attention(proj3, att_q_norm_g[l], att_k_norm_g[l])
        y_c = _mixer_hyena(proj3, hy_conv_w[l], hy_conv_b[l], hy_skip[l], ah5, ssq, l, consts)
        ml_f, ml_b = _mixer_mlstm(proj2, bsz)
        flat = lambda a: a.reshape(n, GROUP_W)
        x2 = _outproj(flat(lru_f), flat(lru_b), proj2, flat(y_b), flat(y_c), flat(ml_f), flat(ml_b), ml_norm_g[l],
                      w_out_b, l, x2, mod[l])
        x2 = _ffn(x2, mod[l], norm_ffn_g[l], w1_b, w3_b, w2_b, l, final_g, final=(l == DEPTH - 1))
    return x2.reshape(bsz, SEQ, D_MODEL)
```

```python
import functools
import math

import numpy as np
import jax
import jax.numpy as jnp
from jax import lax
from jax.experimental import pallas as pl
from jax.experimental.pallas import tpu as pltpu

F32 = jnp.float32
BF16 = jnp.bfloat16
HIGHEST = lax.Precision.HIGHEST

D_MODEL = 2048
SEQ = 4096
DEPTH = 2
GROUP_W = 512
LRU_BLOCKS = 8
LRU_C = 8.0
ATT_HEADS = 8
ATT_KV_HEADS = 2
ATT_GROUP = ATT_HEADS // ATT_KV_HEADS
ATT_HEAD_DIM = 64
ROPE_AXIS = ATT_HEAD_DIM // 2
ROPE_THETA = 10000.0
GRID_W = 64
ATT_HEAD_GROUP = 2
ATT_V_ROWS = 80
LOG2_E = math.log2(math.e)
HY_W = GROUP_W
HY_ORDER = 2
HY_BANDS = 8
HY_EMB = 2 * HY_BANDS + 1
HY_EMB_PAD = 32
HY_FFN = 64
ML_HEADS = 4
ML_HEAD_DIM = 128
ML_CHUNK = 128
D_FF = 5632
EPS = 1e-6
IN_SIZES = (512, 512, 512, 128, 128, 1536, 512, 512, 512, 512, 16)
D_IN = sum(IN_SIZES)
D_IN_PAD = 5632

COL_AX, COL_AG, COL_BQ, COL_BK, COL_BV, COL_CU = 0, 512, 1024, 1536, 1664, 1792
COL_DQ, COL_DK, COL_DV, COL_DO, COL_GATES = 3328, 3840, 4352, 4864, 5376
LANES = 128

FFT_N = 2 * SEQ
FFT_N1 = 64
FFT_N2 = 128
PITCH = 136

VMEM_LIMIT = 56 * 1024 * 1024


def _cparams(sem, vmem=VMEM_LIMIT):
    return pltpu.CompilerParams(dimension_semantics=sem, vmem_limit_bytes=vmem)


def _bdot(a, b):
    return jnp.dot(a.astype(BF16), b.astype(BF16), preferred_element_type=F32)


def _sigmoid(x):
    return 0.5 * jnp.tanh(0.5 * x) + 0.5


def _log_sigmoid(x):
    return jnp.minimum(x, 0.0) - jnp.log1p(jnp.exp(-jnp.abs(x)))


def _softplus(x):
    return jnp.maximum(x, 0.0) + jnp.log1p(jnp.exp(-jnp.abs(x)))


def _ada_kernel(c_ref, w_ref, b_ref, o_ref):
    c = c_ref[...]
    o_ref[...] = _bdot(c * _sigmoid(c), w_ref[...]) + b_ref[...]


def _ada_all(c, ada_w, ada_b):
    bsz = c.shape[0]
    rows = 8
    cp = jnp.zeros((rows, D_MODEL), F32).at[:bsz].set(c)
    tn = 2048
    out = pl.pallas_call(
        _ada_kernel,
        grid=(DEPTH, 6 * D_MODEL // tn),
        in_specs=[pl.BlockSpec((rows, D_MODEL), lambda l, j: (0, 0)),
                  pl.BlockSpec((None, D_MODEL, tn), lambda l, j: (l, 0, j)),
                  pl.BlockSpec((None, 1, tn), lambda l, j: (l, 0, j))],
        out_specs=pl.BlockSpec((None, rows, tn), lambda l, j: (l, 0, j)),
        out_shape=jax.ShapeDtypeStruct((DEPTH, rows, 6 * D_MODEL), F32),
        compiler_params=_cparams(("parallel", "parallel")),
        name="ada_mod",
    )(cp, ada_w, ada_b.reshape(DEPTH, 1, 6 * D_MODEL))
    return out[:, :bsz].reshape(DEPTH, bsz, 6, D_MODEL)


def _rms_mod(x, g, scale, shift):
    ms = jnp.mean(x * x, axis=-1, keepdims=True)
    return (x * lax.rsqrt(ms + EPS)) * (g * (1.0 + scale)) + shift


def _inproj_kernel(x_ref, m_ref, g_ref, w_ref, b_ref, o_ref, h_scr):
    @pl.when(pl.program_id(1) == 0)
    def _():
        h_scr[...] = _rms_mod(x_ref[...], g_ref[...], m_ref[1:2, :], m_ref[0:1, :]).astype(BF16)

    o_ref[...] = jnp.dot(h_scr[...], w_ref[...], preferred_element_type=F32) + b_ref[...]


def _inproj(x2, mod_l, g, w_all, b_all, layer):
    n = x2.shape[0]
    tm, tn = 1024, 1408
    per_b = SEQ // tm
    return pl.pallas_call(
        _inproj_kernel,
        grid=(n // tm, D_IN_PAD // tn),
        in_specs=[pl.BlockSpec((tm, D_MODEL), lambda i, j: (i, 0)),
                  pl.BlockSpec((None, 6, D_MODEL), lambda i, j: (i // per_b, 0, 0)),
                  pl.BlockSpec((1, D_MODEL), lambda i, j: (0, 0)),
                  pl.BlockSpec((None, D_MODEL, tn), lambda i, j: (layer, 0, j)),
                  pl.BlockSpec((None, 1, tn), lambda i, j: (layer, 0, j))],
        out_specs=pl.BlockSpec((tm, tn), lambda i, j: (i, j)),
        out_shape=jax.ShapeDtypeStruct((n, D_IN_PAD), F32),
        scratch_shapes=[pltpu.VMEM((tm, D_MODEL), BF16)],
        compiler_params=_cparams(("parallel", "arbitrary")),
        name="in_proj",
    )(x2, mod_l, g.reshape(1, D_MODEL), w_all, b_all)


def _shifted(ext, off, rows):
    total = ext.shape[0]
    if off == 0:
        return ext[8:8 + rows]
    return pltpu.roll(ext, (-off) % total, axis=0)[8:8 + rows]


def _dwconv_ext(ext, w_ref, b_ref, left, rows):
    out = b_ref[...]
    for j in range(w_ref.shape[0]):
        out = out + _shifted(ext, j - left, rows) * w_ref[j:j + 1, :]
    return out


def _dwconv_tile(x_ref, p_ref, n_ref, w_ref, b_ref, ext_ref, tile, n_tiles, left):
    rows = x_ref.shape[0]
    prev = jnp.where(tile > 0, p_ref[...], 0.0)
    nxt = jnp.where(tile < n_tiles - 1, n_ref[...], 0.0)
    outs = []
    for s in range(ext_ref.shape[0]):
        ln = slice(s * LANES, (s + 1) * LANES)
        ext_ref[s, 0:8, :] = prev[:, ln]
        ext_ref[s, 8:8 + rows, :] = x_ref[:, ln]
        ext_ref[s, 8 + rows:16 + rows, :] = nxt[:, ln]
        out = b_ref[:, ln]
        for j in range(w_ref.shape[0]):
            start = 8 + j - left
            out = out + ext_ref[s, start:start + rows, :] * w_ref[j:j + 1, ln]
        outs.append(out)
    return jnp.concatenate(outs, axis=1)


def _dwconv_rows(ref, lo, rows, w_ref, b_ref, left):
    taps = w_ref.shape[0]
    if lo - left >= 0 and lo + rows + (taps - 1 - left) <= SEQ:
        out = b_ref[...]
        for j in range(taps):
            start = lo + j - left
            out = out + ref[start:start + rows, :] * w_ref[j:j + 1, :]
        return out
    zeros = jnp.zeros((8, ref.shape[1]), F32)
    prev = ref[lo - 8:lo, :] if lo > 0 else zeros
    nxt = ref[lo + rows:lo + rows + 8, :] if lo + rows < SEQ else zeros
    ext = jnp.concatenate([prev, ref[lo:lo + rows, :], nxt], axis=0)
    return _dwconv_ext(ext, w_ref, b_ref, left, rows)


def _halo_specs(tile_rows, width, col_block, tile_of):
    r8 = tile_rows // 8
    last8 = SEQ // 8 - 1

    def main(b, t):
        return (b, tile_of(t), col_block)

    def prev(b, t):
        return (b, jnp.maximum(tile_of(t) * r8 - 1, 0), col_block)

    def nxt(b, t):
        return (b, jnp.minimum((tile_of(t) + 1) * r8, last8), col_block)

    return [pl.BlockSpec((None, tile_rows, width), main),
            pl.BlockSpec((None, 8, width), prev),
            pl.BlockSpec((None, 8, width), nxt)]


def _lru_gates(xc, wg, bg, lam):
    gates = _bdot(xc, wg) + bg
    r = _sigmoid(gates[:, :GROUP_W])
    i = _sigmoid(gates[:, GROUP_W:])
    log_a = (-LRU_C * _softplus(-lam)) * r
    a = jnp.exp(log_a)
    th = jnp.tanh(log_a)
    u = jnp.sqrt(-2.0 * th / (1.0 - th)) * (i * xc)
    return a, u


def _scan8(a, u, ridx, reverse):
    for k in (1, 2, 4):
        if reverse:
            keep = ridx < 8 - k
            sh = 8 - k
        else:
            keep = ridx >= k
            sh = k
        a_sh = jnp.where(keep, pltpu.roll(a, sh, axis=0), 1.0)
        u_sh = jnp.where(keep, pltpu.roll(u, sh, axis=0), 0.0)
        u = a * u_sh + u
        a = a * a_sh
    return a, u


def _lru_kernel(xf_ref, pf_ref, nf_ref, xb_ref, pb_ref, nb_ref, cw_ref, cb_ref, wg_ref, bg_ref, lam_ref,
                hf_ref, hb_ref, af_scr, uf_scr, ab_scr, ub_scr, cf_scr, cb_scr, ext_scr, c_scr, *, n_tiles):
    t = pl.program_id(1)
    n_chunks = af_scr.shape[0]
    xc = _dwconv_tile(xf_ref, pf_ref, nf_ref, cw_ref, cb_ref, ext_scr.at[0], t, n_tiles, 2)
    a, u = _lru_gates(xc, wg_ref[0], bg_ref[0], lam_ref[0])
    af_scr[...] = a.reshape(n_chunks, 8, GROUP_W)
    uf_scr[...] = u.reshape(n_chunks, 8, GROUP_W)
    xc = _dwconv_tile(xb_ref, pb_ref, nb_ref, cw_ref, cb_ref, ext_scr.at[1], n_tiles - 1 - t, n_tiles, 2)
    a, u = _lru_gates(xc, wg_ref[1], bg_ref[1], lam_ref[1])
    ab_scr[...] = a.reshape(n_chunks, 8, GROUP_W)
    ub_scr[...] = u.reshape(n_chunks, 8, GROUP_W)

    @pl.when(t == 0)
    def _():
        c_scr[...] = jnp.zeros_like(c_scr)

    ridx = lax.broadcasted_iota(jnp.int32, (8, GROUP_W), 0)

    def local(c, carry):
        a, u = _scan8(af_scr[c], uf_scr[c], ridx, False)
        af_scr[c] = a
        uf_scr[c] = u
        a, u = _scan8(ab_scr[c], ub_scr[c], ridx, True)
        ab_scr[c] = a
        ub_scr[c] = u
        return carry

    lax.fori_loop(0, n_chunks, local, 0, unroll=4)

    def chain(c, carry):
        cf, cb = carry
        cf_scr[c] = jnp.broadcast_to(cf, (8, GROUP_W))
        cf = af_scr[c][7:8, :] * cf + uf_scr[c][7:8, :]
        cr = n_chunks - 1 - c
        cb_scr[cr] = jnp.broadcast_to(cb, (8, GROUP_W))
        cb = ab_scr[cr][0:1, :] * cb + ub_scr[cr][0:1, :]
        return cf, cb

    cf, cb = lax.fori_loop(0, n_chunks, chain, (c_scr[0:1, :], c_scr[1:2, :]), unroll=4)
    c_scr[0:1, :] = cf
    c_scr[1:2, :] = cb

    def apply(c, carry):
        r0 = pl.multiple_of(c * 8, 8)
        hf_ref[pl.ds(r0, 8), :] = uf_scr[c] + af_scr[c] * cf_scr[c]
        hb_ref[pl.ds(r0, 8), :] = ub_scr[c] + ab_scr[c] * cb_scr[c]
        return carry

    lax.fori_loop(0, n_chunks, apply, 0, unroll=4)


def _block_diag(w):
    nb, k, j = w.shape
    eye = jnp.eye(nb, dtype=w.dtype)
    return jnp.einsum('nkj,nm->nkmj', w, eye).reshape(nb * k, nb * j)


def _mixer_rglru(proj3, conv_w, conv_b, wa, ba, wx, bx, lam):
    bsz = proj3.shape[0]
    ts = 512
    n_tiles = SEQ // ts
    wg = jnp.stack([jnp.concatenate([_block_diag(wa[d]), _block_diag(wx[d])], axis=1) for d in range(2)]).astype(BF16)
    bg = jnp.stack([jnp.concatenate([ba[d], bx[d]]).reshape(1, 2 * GROUP_W) for d in range(2)])
    small = lambda *shape: pl.BlockSpec(shape, lambda b, t: (0,) * len(shape))
    fwd_of = lambda t: t
    bwd_of = lambda t: n_tiles - 1 - t
    out = lambda tile_of: pl.BlockSpec((None, ts, GROUP_W), lambda b, t: (b, tile_of(t), 0))
    tile_scr = pltpu.VMEM((ts // 8, 8, GROUP_W), F32)
    return pl.pallas_call(
        functools.partial(_lru_kernel, n_tiles=n_tiles),
        grid=(bsz, n_tiles),
        in_specs=_halo_specs(ts, GROUP_W, COL_AX // GROUP_W, fwd_of) + _halo_specs(ts, GROUP_W, COL_AX // GROUP_W, bwd_of)
        + [small(4, GROUP_W), small(1, GROUP_W), small(2, GROUP_W, 2 * GROUP_W), small(2, 1, 2 * GROUP_W),
           small(2, 1, GROUP_W)],
        out_specs=[out(fwd_of), out(bwd_of)],
        out_shape=[jax.ShapeDtypeStruct((bsz, SEQ, GROUP_W), F32)] * 2,
        scratch_shapes=[tile_scr] * 6 + [pltpu.VMEM((2, GROUP_W // LANES, ts + 16, LANES), F32),
                                         pltpu.VMEM((8, GROUP_W), F32)],
        compiler_params=_cparams(("parallel", "arbitrary")),
        name="rglru",
    )(proj3, proj3, proj3, proj3, proj3, proj3, conv_w, conv_b.reshape(1, GROUP_W), wg, bg,
      lam.reshape(2, 1, GROUP_W))


def _split_dot(x, m_ref):
    hi = x.astype(BF16)
    lo = (x - hi.astype(F32)).astype(BF16)
    m = m_ref[...]
    return (jnp.dot(hi, m, preferred_element_type=F32) + jnp.dot(lo, m, preferred_element_type=F32))


def _norm_rope(x, gain, m_ref, cos, sins):
    width = x.shape[1]
    ms = _split_dot(x * x, m_ref)
    xn = x * lax.rsqrt(ms + EPS) * gain
    lane = lax.broadcasted_iota(jnp.int32, xn.shape, 1)
    first = (lane % ROPE_AXIS) < (ROPE_AXIS // 2)
    half = ROPE_AXIS // 2
    partner = jnp.where(first, pltpu.roll(xn, width - half, axis=1), pltpu.roll(xn, half, axis=1))
    return xn * cos + partner * sins


def _attn_prep_kernel(q_ref, kv_ref, gq_ref, gk_ref, mq_ref, mk_ref, cos_ref, sin_ref,
                      qo_ref, kt_ref, vt_ref):
    cos = cos_ref[...]
    sins = sin_ref[...]
    cos_q = jnp.concatenate([cos] * (GROUP_W // LANES), axis=1)
    sin_q = jnp.concatenate([sins] * (GROUP_W // LANES), axis=1)
    q = _norm_rope(q_ref[...], gq_ref[...], mq_ref, cos_q, sin_q)
    qo_ref[...] = (q * (ATT_HEAD_DIM ** -0.5 * LOG2_E)).astype(BF16)
    kv = kv_ref[...]
    kt_ref[...] = _norm_rope(kv[:, :LANES], gk_ref[...], mk_ref, cos, sins).T.astype(BF16)
    v_t = kv[:, LANES:].T
    row = lax.broadcasted_iota(jnp.int32, (ATT_V_ROWS - ATT_HEAD_DIM, v_t.shape[1]), 0)
    ones_rows = jnp.where(row == 0, 1.0, 0.0)
    for g in range(ATT_KV_HEADS):
        vt_ref[g] = jnp.concatenate([v_t[g * ATT_HEAD_DIM:(g + 1) * ATT_HEAD_DIM, :], ones_rows], axis=0).astype(BF16)


def _rope_tables():
    rows = SEQ // GRID_W
    row = jnp.repeat(jnp.arange(rows, dtype=F32), GRID_W)
    col = jnp.tile(jnp.arange(GRID_W, dtype=F32), rows)
    inv = ROPE_THETA ** (-jnp.arange(0, ROPE_AXIS, 2, dtype=F32) / ROPE_AXIS)
    ar = row[:, None] * inv
    ac = col[:, None] * inv
    ang = jnp.concatenate([ar, ar, ac, ac], axis=1)
    sign = jnp.concatenate([-jnp.ones((ROPE_AXIS // 2,), F32), jnp.ones((ROPE_AXIS // 2,), F32)] * 2)
    cos = jnp.tile(jnp.cos(ang), (1, 2))
    sins = jnp.tile(jnp.sin(ang) * sign, (1, 2))
    return cos, sins


def _head_mean_matrix(width):
    idx = np.arange(width) // ATT_HEAD_DIM
    return jnp.asarray((idx[:, None] == idx[None, :]).astype(np.float32) / ATT_HEAD_DIM, dtype=BF16)


def _attn_kernel(q_ref, kt_ref, vt_ref, o_ref):
    def scores(h):
        g = h // ATT_GROUP
        q = q_ref[:, h * ATT_HEAD_DIM:(h + 1) * ATT_HEAD_DIM]
        return jnp.dot(q, kt_ref[g * ATT_HEAD_DIM:(g + 1) * ATT_HEAD_DIM, :], preferred_element_type=F32)

    def probs(s):
        return jnp.exp2(s - jnp.max(s, axis=-1, keepdims=True)).astype(BF16)

    def values(h, p):
        o_aug = lax.dot_general(vt_ref[h // ATT_GROUP], p, (((1,), (1,)), ((), ())), preferred_element_type=F32)
        return (o_aug[:ATT_HEAD_DIM] / o_aug[ATT_HEAD_DIM:ATT_HEAD_DIM + 1]).T

    outs = []
    for h0 in range(0, ATT_HEADS, ATT_HEAD_GROUP):
        heads = range(h0, h0 + ATT_HEAD_GROUP)
        ps = [probs(s) for s in [scores(h) for h in heads]]
        outs += [values(h, p) for h, p in zip(heads, ps)]
    o_ref[...] = jnp.concatenate(outs, axis=1).astype(o_ref.dtype)


def _mixer_attention(proj3, q_g, k_g):
    bsz = proj3.shape[0]
    ts = 512
    cos, sins = _rope_tables()
    gq = jnp.tile(q_g, ATT_HEADS).reshape(1, GROUP_W)
    gk = jnp.tile(k_g, ATT_KV_HEADS).reshape(1, LANES)
    const = lambda shape: pl.BlockSpec(shape, lambda b, t: (0, 0))
    qp, kt, vt = pl.pallas_call(
        _attn_prep_kernel,
        grid=(bsz, SEQ // ts),
        in_specs=[pl.BlockSpec((None, ts, GROUP_W), lambda b, t: (b, t, COL_BQ // GROUP_W)),
                  pl.BlockSpec((None, ts, 2 * LANES), lambda b, t: (b, t, COL_BK // (2 * LANES))),
                  const((1, GROUP_W)), const((1, LANES)), const((GROUP_W, GROUP_W)), const((LANES, LANES)),
                  pl.BlockSpec((ts, LANES), lambda b, t: (t, 0)),
                  pl.BlockSpec((ts, LANES), lambda b, t: (t, 0))],
        out_specs=[pl.BlockSpec((None, ts, GROUP_W), lambda b, t: (b, t, 0)),
                   pl.BlockSpec((None, LANES, ts), lambda b, t: (b, 0, t)),
                   pl.BlockSpec((None, ATT_KV_HEADS, ATT_V_ROWS, ts), lambda b, t: (b, 0, 0, t))],
        out_shape=[jax.ShapeDtypeStruct((bsz, SEQ, GROUP_W), BF16),
                   jax.ShapeDtypeStruct((bsz, LANES, SEQ), BF16),
                   jax.ShapeDtypeStruct((bsz, ATT_KV_HEADS, ATT_V_ROWS, SEQ), BF16)],
        compiler_params=_cparams(("parallel", "parallel")),
        name="attn_prep",
    )(proj3, proj3, gq, gk, _head_mean_matrix(GROUP_W), _head_mean_matrix(LANES), cos, sins)

    tq = 512
    return pl.pallas_call(
        _attn_kernel,
        grid=(bsz, SEQ // tq),
        in_specs=[pl.BlockSpec((None, tq, GROUP_W), lambda b, t: (b, t, 0)),
                  pl.BlockSpec((None, LANES, SEQ), lambda b, t: (b, 0, 0)),
                  pl.BlockSpec((None, ATT_KV_HEADS, ATT_V_ROWS, SEQ), lambda b, t: (b, 0, 0, 0))],
        out_specs=pl.BlockSpec((None, tq, GROUP_W), lambda b, t: (b, t, 0)),
        out_shape=jax.ShapeDtypeStruct((bsz, SEQ, GROUP_W), BF16),
        compiler_params=_cparams(("parallel", "parallel")),
        name="attention",
    )(qp, kt, vt)


def _hy_features():
    L = SEQ
    pos = jnp.arange(L, dtype=F32)
    t = pos / max(L - 1, 1)
    bands = jnp.linspace(1e-4, HY_BANDS - 1, HY_BANDS, dtype=F32)
    ang = (2.0 * math.pi * pos / L)[:, None] * bands
    feat = jnp.concatenate([t[:, None], jnp.cos(ang), -jnp.sin(ang)], axis=-1)
    feat = jnp.pad(feat, ((0, 0), (0, HY_EMB_PAD - HY_EMB)))
    rev_idx = np.concatenate([[0], np.arange(L - 1, 0, -1)])
    return feat, feat[rev_idx]


def _hy_mlp(feat, feat_t, w1t_ref, b1_ref, w2t_ref, b2_ref, w3_ref, sf_ref, dec_ref):
    sf = sf_ref[...]
    h = jnp.sin(sf * (jnp.dot(w1t_ref[...], feat_t, precision=HIGHEST, preferred_element_type=F32) + b1_ref[...]))
    h = jnp.sin(sf * (jnp.dot(w2t_ref[...], h, precision=HIGHEST, preferred_element_type=F32) + b2_ref[...]))
    out = jnp.dot(h.T, w3_ref[...], precision=HIGHEST, preferred_element_type=F32)
    return out * jnp.exp(-feat[:, 0:1] * jnp.abs(dec_ref[...]))


def _hy_filter_kernel(ff_ref, fft_ref, fr_ref, frt_ref, w1t_ref, b1_ref, w2t_ref, b2_ref, w3f_ref, w3b_ref, sf_ref,
                      decf_ref, decb_ref, of_ref, ob_ref, ssq_ref):
    i = pl.program_id(1)
    hf = _hy_mlp(ff_ref[...], fft_ref[...], w1t_ref, b1_ref, w2t_ref, b2_ref, w3f_ref, sf_ref, decf_ref)
    hb = _hy_mlp(fr_ref[...], frt_ref[...], w1t_ref, b1_ref, w2t_ref, b2_ref, w3b_ref, sf_ref, decb_ref)

    @pl.when(i == 0)
    def _():
        ssq_ref[...] = jnp.zeros_like(ssq_ref)

    ssq_ref[0:1, :] += jnp.sum(hf * hf + hb * hb, axis=0, keepdims=True)
    of_ref[...] = hf
    row = lax.broadcasted_iota(jnp.int32, hb.shape, 0)
    ob_ref[...] = jnp.where(jnp.logical_and(i == 0, row == 0), 0.0, hb)


def _hy_filters(hy_w1, hy_b1, hy_w2, hy_b2, hy_w3, hy_sin_freq, hy_decay):
    feat, feat_rev = _hy_features()
    tr = 512
    cw = HY_ORDER * HY_W
    w1t = jnp.transpose(jnp.pad(hy_w1, ((0, 0), (0, HY_EMB_PAD - HY_EMB), (0, 0))), (0, 2, 1))
    w2t = jnp.transpose(hy_w2, (0, 2, 1))
    w3 = hy_w3.reshape(DEPTH, HY_FFN, HY_ORDER, 2, HY_W)
    dec = hy_decay.reshape(DEPTH, HY_ORDER, 2, HY_W)
    w3f = w3[:, :, :, 0].reshape(DEPTH, HY_FFN, cw)
    w3b = w3[:, :, :, 1].reshape(DEPTH, HY_FFN, cw)
    decf = dec[:, :, 0].reshape(DEPTH, 1, cw)
    decb = dec[:, :, 1].reshape(DEPTH, 1, cw)
    col = lambda a: a.reshape(DEPTH, HY_FFN, 1)
    row_spec = pl.BlockSpec((tr, HY_EMB_PAD), lambda l, i: (i, 0))
    rowt_spec = pl.BlockSpec((HY_EMB_PAD, tr), lambda l, i: (0, i))
    per_layer = lambda a, b: pl.BlockSpec((None, a, b), lambda l, i: (l, 0, 0))
    out_spec = pl.BlockSpec((None, tr, cw), lambda l, i: (l, i, 0))
    return pl.pallas_call(
        _hy_filter_kernel,
        grid=(DEPTH, SEQ // tr),
        in_specs=[row_spec, rowt_spec, row_spec, rowt_spec, per_layer(HY_FFN, HY_EMB_PAD), per_layer(HY_FFN, 1),
                  per_layer(HY_FFN, HY_FFN), per_layer(HY_FFN, 1), per_layer(HY_FFN, cw), per_layer(HY_FFN, cw),
                  per_layer(HY_FFN, 1), per_layer(1, cw), per_layer(1, cw)],
        out_specs=[out_spec, out_spec, per_layer(8, cw)],
        out_shape=[jax.ShapeDtypeStruct((DEPTH, SEQ, cw), F32)] * 2 + [jax.ShapeDtypeStruct((DEPTH, 8, cw), F32)],
        compiler_params=_cparams(("arbitrary", "arbitrary")),
        name="hyena_filter",
    )(feat, feat.T, feat_rev, feat_rev.T, w1t, col(hy_b1), w2t, col(hy_b2), w3f, w3b, col(hy_sin_freq), decf, decb)


def _dft_constants():
    n1 = np.arange(FFT_N1)
    n2 = np.arange(FFT_N2)
    f1 = np.exp(-2j * np.pi * np.outer(n1, n1) / FFT_N1)
    stack = lambda m: np.concatenate([m.real, m.imag], axis=0)
    half = FFT_N1 // 2
    sig_l = stack(f1[:, :half])
    sig_r = np.concatenate([-f1[:, :half].imag, f1[:, :half].real], axis=0)
    fil_r = stack(f1[:, half:])
    f2 = np.exp(-2j * np.pi * np.outer(n2, n2) / FFT_N2)
    tw = np.exp(-2j * np.pi * np.outer(n1, n2) / FFT_N)
    fwd = f2[None, :, :] * tw[:, None, :]
    inv = np.conj(np.transpose(fwd, (0, 2, 1))) / FFT_N
    block = lambda m: np.concatenate([np.concatenate([m.real, -m.imag], axis=2),
                                      np.concatenate([m.imag, m.real], axis=2)], axis=1)
    g1 = np.conj(f1[:half, :])
    out_l = stack(g1)
    out_r = np.concatenate([-g1.imag, g1.real], axis=0)
    as32 = lambda a: jnp.asarray(a.astype(np.float32))
    return dict(sig_l=as32(sig_l), sig_r=as32(sig_r), fil_r=as32(fil_r), fwd=as32(block(fwd)),
                inv=as32(block(inv)), out_l=as32(out_l), out_r=as32(out_r))


_N2_GROUP = 8


def _dft_a_kernel(*refs, conv):
    if conv:
        u0_ref, u1_ref, cw_ref, cb_ref, ml_ref, mr_ref, o_ref, xs, ysc = refs
    else:
        u0_ref, u1_ref, ml_ref, mr_ref, o_ref, xs, ysc = refs
    half = FFT_N1 // 2
    for r, u_ref in enumerate((u0_ref, u1_ref)):
        for n1 in range(half):
            lo = FFT_N2 * n1
            rows = _dwconv_rows(u_ref, lo, FFT_N2, cw_ref, cb_ref, 1) if conv else u_ref[lo:lo + FFT_N2, :]
            xs[r, PITCH * n1:PITCH * n1 + FFT_N2, :] = rows
    ml = ml_ref[...]
    mr = mr_ref[...]

    def body(g, carry):
        n2 = g * _N2_GROUP
        x0 = jnp.concatenate([xs[0, pl.ds(n2 + i, half, stride=PITCH), :] for i in range(_N2_GROUP)], axis=1)
        x1 = jnp.concatenate([xs[1, pl.ds(n2 + i, half, stride=PITCH), :] for i in range(_N2_GROUP)], axis=1)
        y = _bdot(ml, x0) + _bdot(mr, x1)
        for i in range(_N2_GROUP):
            ysc[pl.ds(n2 + i, 2 * FFT_N1, stride=PITCH), :] = y[:, LANES * i:LANES * (i + 1)]
        return carry

    lax.fori_loop(0, FFT_N2 // _N2_GROUP, body, 0)
    for row in range(2 * FFT_N1):
        o_ref[FFT_N2 * row:FFT_N2 * (row + 1), :] = ysc[PITCH * row:PITCH * row + FFT_N2, :].astype(o_ref.dtype)


def _dft_a(ml, mr, srcs, groups, n_slabs, conv_args=None):
    conv = conv_args is not None
    slab = lambda arr_map: pl.BlockSpec((None, SEQ, LANES), arr_map)
    mspec = pl.BlockSpec((2 * FFT_N1, FFT_N1 // 2), lambda g, s: (0, 0))
    in_specs = [slab(srcs[0][1]), slab(srcs[1][1])]
    args = [srcs[0][0], srcs[1][0]]
    if conv:
        cw, cb, col0 = conv_args
        in_specs += [pl.BlockSpec((cw.shape[0], LANES), lambda g, s: (0, col0 + s)),
                     pl.BlockSpec((1, LANES), lambda g, s: (0, col0 + s))]
        args += [cw, cb]
    rows = 2 * FFT_N1 * FFT_N2
    return pl.pallas_call(
        functools.partial(_dft_a_kernel, conv=conv),
        grid=(groups, n_slabs),
        in_specs=in_specs + [mspec, mspec],
        out_specs=pl.BlockSpec((None, rows, LANES), lambda g, s: (g, 0, s)),
        out_shape=jax.ShapeDtypeStruct((groups, rows, n_slabs * LANES), BF16),
        scratch_shapes=[pltpu.VMEM((2, (FFT_N1 // 2) * PITCH, LANES), F32),
                        pltpu.VMEM((2 * FFT_N1 * PITCH, LANES), F32)],
        compiler_params=_cparams(("parallel", "parallel")),
        name="dft_a",
    )(*args, ml, mr)


def _dft_mid_kernel(f_ref, g_ref, ah_ref, ssq_ref, a_ref, o_ref):
    scale = lax.rsqrt(ssq_ref[0:1, :] + EPS)
    for kk in range(f_ref.shape[0]):
        f = f_ref[kk]
        g = g_ref[kk]
        h = _bdot(f, jnp.concatenate([ah_ref[0, kk], ah_ref[1, kk]], axis=0)) * scale
        hr = h[:FFT_N2]
        hi = h[FFT_N2:]
        for p in range(a_ref.shape[0]):
            y = _bdot(f, jnp.concatenate([a_ref[p, 0, kk], a_ref[p, 1, kk]], axis=0))
            yr = y[:FFT_N2]
            yi = y[FFT_N2:]
            z = jnp.concatenate([yr * hr - yi * hi, yr * hi + yi * hr], axis=0)
            w = _bdot(g, z)
            o_ref[p, 0, kk] = w[:FFT_N2].astype(o_ref.dtype)
            o_ref[p, 1, kk] = w[FFT_N2:].astype(o_ref.dtype)


def _dft_mid(fwd, inv, ah5, ssq, layer, order, a5):
    pairs = a5.shape[0]
    kb = 2
    blk = pl.BlockSpec((pairs, 2, kb, FFT_N2, HY_W), lambda k: (0, 0, k, 0, 0))
    mat = pl.BlockSpec((kb, 2 * FFT_N2, 2 * FFT_N2), lambda k: (k, 0, 0))
    return pl.pallas_call(
        _dft_mid_kernel,
        grid=(FFT_N1 // kb,),
        in_specs=[mat, mat,
                  pl.BlockSpec((None, 2, kb, FFT_N2, HY_W), lambda k: (layer, 0, k, 0, order)),
                  pl.BlockSpec((None, 8, HY_W), lambda k: (layer, 0, order)),
                  blk],
        out_specs=blk,
        out_shape=jax.ShapeDtypeStruct(a5.shape, BF16),
        compiler_params=_cparams(("parallel",)),
        name="dft_mid",
    )(fwd, inv, ah5, ssq, a5)


def _dft_c_kernel(*refs, u_conv):
    if u_conv:
        (b_ref, u0_ref, u1_ref, g0_ref, g1_ref, ucw_ref, ucb_ref, gcw_ref, gcb_ref, ml_ref, mr_ref, skip_ref,
         o_ref, bs, ys) = refs
    else:
        b_ref, u0_ref, u1_ref, g0_ref, g1_ref, gcw_ref, gcb_ref, ml_ref, mr_ref, skip_ref, o_ref, bs, ys = refs
    half = FFT_N1 // 2
    ml = ml_ref[...]
    mr = mr_ref[...]
    im0 = FFT_N1 * PITCH
    for row in range(2 * FFT_N1):
        bs[PITCH * row:PITCH * row + FFT_N2, :] = b_ref[FFT_N2 * row:FFT_N2 * (row + 1), :].astype(F32)

    def body(g, carry):
        n2 = g * _N2_GROUP
        br = jnp.concatenate([bs[pl.ds(n2 + i, FFT_N1, stride=PITCH), :] for i in range(_N2_GROUP)], axis=1)
        bi = jnp.concatenate([bs[pl.ds(im0 + n2 + i, FFT_N1, stride=PITCH), :] for i in range(_N2_GROUP)], axis=1)
        y = _bdot(ml, br) + _bdot(mr, bi)
        for i in range(_N2_GROUP):
            ys[0, pl.ds(n2 + i, half, stride=PITCH), :] = y[:half, LANES * i:LANES * (i + 1)]
            ys[1, pl.ds(n2 + i, half, stride=PITCH), :] = y[half:, LANES * i:LANES * (i + 1)]
        return carry

    lax.fori_loop(0, FFT_N2 // _N2_GROUP, body, 0)
    skip = skip_ref[...]
    for r, (u_ref, g_ref) in enumerate(((u0_ref, g0_ref), (u1_ref, g1_ref))):
        for n1 in range(half):
            lo = FFT_N2 * n1
            u = _dwconv_rows(u_ref, lo, FFT_N2, ucw_ref, ucb_ref, 1) if u_conv else u_ref[lo:lo + FFT_N2, :]
            gate = _dwconv_rows(g_ref, lo, FFT_N2, gcw_ref, gcb_ref, 1)
            conv = ys[r, PITCH * n1:PITCH * n1 + FFT_N2, :]
            o_ref[r, lo:lo + FFT_N2, :] = (gate * (conv + u * skip)).astype(o_ref.dtype)


def _dft_c(ml, mr, b3, u_src, gate_src, u_conv_args, gate_conv_args, skip, out_dtype):
    pairs = b3.shape[0]
    n_slabs = HY_W // LANES
    u_conv = u_conv_args is not None
    slab = lambda m: pl.BlockSpec((None, SEQ, LANES), m)
    wspecs = lambda cw, col0: [pl.BlockSpec((cw.shape[0], LANES), lambda p, s: (0, col0 + s)),
                               pl.BlockSpec((1, LANES), lambda p, s: (0, col0 + s))]
    in_specs = [pl.BlockSpec((None, 2 * FFT_N1 * FFT_N2, LANES), lambda p, s: (p, 0, s)),
                slab(u_src[1]), slab(u_src[2]), slab(gate_src[1]), slab(gate_src[2])]
    args = [b3, u_src[0], u_src[0], gate_src[0], gate_src[0]]
    if u_conv:
        in_specs += wspecs(u_conv_args[0], u_conv_args[2])
        args += [u_conv_args[0], u_conv_args[1]]
    in_specs += wspecs(gate_conv_args[0], gate_conv_args[2])
    args += [gate_conv_args[0], gate_conv_args[1]]
    mspec = pl.BlockSpec((FFT_N1, FFT_N1), lambda p, s: (0, 0))
    in_specs += [mspec, mspec, pl.BlockSpec((1, LANES), lambda p, s: (0, s))]
    args += [ml, mr, skip.reshape(1, HY_W)]
    return pl.pallas_call(
        functools.partial(_dft_c_kernel, u_conv=u_conv),
        grid=(pairs, n_slabs),
        in_specs=in_specs,
        out_specs=pl.BlockSpec((2, SEQ, LANES), lambda p, s: (p, 0, s)),
        out_shape=jax.ShapeDtypeStruct((2 * pairs, SEQ, HY_W), out_dtype),
        scratch_shapes=[pltpu.VMEM((2 * FFT_N1 * PITCH, LANES), F32),
                        pltpu.VMEM((2, (FFT_N1 // 2) * PITCH, LANES), F32)],
        compiler_params=_cparams(("parallel", "parallel")),
        name="dft_c",
    )(*args)


def _mixer_hyena(proj3, conv_w, conv_b, skip, ah5, ssq, layer, consts):
    bsz = proj3.shape[0]
    pairs = bsz // 2
    n_slabs = HY_W // LANES
    cb = conv_b.reshape(1, 3 * HY_W)
    col = lambda which: (COL_CU + which * HY_W) // LANES
    proj_map = lambda which, odd: (lambda p, s: (2 * p + odd, 0, col(which) + s))
    plain_map = lambda odd: (lambda p, s: (2 * p + odd, 0, s))
    conv_args = lambda which: (conv_w, cb, which * n_slabs)
    a5_shape = (pairs, 2, FFT_N1, FFT_N2, HY_W)

    a = _dft_a(consts['sig_l'], consts['sig_r'], [(proj3, proj_map(0, 0)), (proj3, proj_map(0, 1))],
               pairs, n_slabs, conv_args(0))
    b = _dft_mid(consts['fwd'], consts['inv'], ah5, ssq, layer, 0, a.reshape(a5_shape))
    z1 = _dft_c(consts['out_l'], consts['out_r'], b.reshape(a.shape),
                (proj3, proj_map(0, 0), proj_map(0, 1)), (proj3, proj_map(1, 0), proj_map(1, 1)),
                conv_args(0), conv_args(1), skip[0], F32)
    a = _dft_a(consts['sig_l'], consts['sig_r'], [(z1, plain_map(0)), (z1, plain_map(1))], pairs, n_slabs)
    b = _dft_mid(consts['fwd'], consts['inv'], ah5, ssq, layer, 1, a.reshape(a5_shape))
    return _dft_c(consts['out_l'], consts['out_r'], b.reshape(a.shape),
                  (z1, plain_map(0), plain_map(1)), (proj3, proj_map(2, 0), proj_map(2, 1)),
                  None, conv_args(2), skip[1], BF16)


def _hyena_filter_stage(consts, hy_w1, hy_b1, hy_w2, hy_b2, hy_w3, hy_sin_freq, hy_decay):
    hf, hb, ssq = _hy_filters(hy_w1, hy_b1, hy_w2, hy_b2, hy_w3, hy_sin_freq, hy_decay)
    cw = HY_ORDER * HY_W
    fmap = lambda l, s: (l, 0, s)
    ha = _dft_a(consts['sig_l'], consts['fil_r'], [(hf, fmap), (hb, fmap)], DEPTH, cw // LANES)
    return ha.reshape(DEPTH, 2, FFT_N1, FFT_N2, cw), ssq


def _mlstm_chunk(q_ref, k_ref, v_ref, gc_ref, c_scr, m_scr, reverse, i_off, f_off, state_off):
    ch = ML_CHUNK
    ri = lax.broadcasted_iota(jnp.int32, (ch, ch), 0)
    ci = lax.broadcasted_iota(jnp.int32, (ch, ch), 1)
    tri = (ci >= ri) if reverse else (ci <= ri)
    tri_f = tri.astype(F32)
    gc = gc_ref[...]
    gr = gc.T
    b_col = jnp.dot(tri_f, _log_sigmoid(gc), precision=HIGHEST, preferred_element_type=F32)
    b_row = lax.dot_general(_log_sigmoid(gr[:4 * ML_HEADS, :]), tri_f, (((1,), (1,)), ((), ())), precision=HIGHEST,
                            preferred_element_type=F32)
    last = 0 if reverse else ch - 1
    lane = lax.broadcasted_iota(jnp.int32, (ch, ML_HEAD_DIM), 1)
    ones_col = jnp.where(lane == 0, 1.0, 0.0).astype(BF16)

    outs = []
    for h in range(ML_HEADS):
        sl = slice(h * ML_HEAD_DIM, (h + 1) * ML_HEAD_DIM)
        st = state_off + h
        q = (q_ref[:, sl] * (ML_HEAD_DIM ** -0.5)).astype(BF16)
        k = k_ref[:, sl]
        v_aug = jnp.concatenate([v_ref[:, sl].astype(BF16), ones_col], axis=1)
        bc = b_col[:, f_off + h:f_off + h + 1]
        lic = gc[:, i_off + h:i_off + h + 1]
        br = b_row[f_off + h:f_off + h + 1, :]
        lir = gr[i_off + h:i_off + h + 1, :]
        b_tot = bc[last:last + 1, :]
        d = jnp.where(tri, bc - br + lir, -jnp.inf)
        w_end = b_tot - bc + lic
        m_loc = jnp.max(w_end, axis=0, keepdims=True)
        e_end = jnp.exp(w_end - m_loc)
        m_prev = m_scr[st:st + 1, 0:1]
        c_prev = c_scr[st]
        m_inter = bc + m_prev
        m_t = jnp.maximum(m_inter, jnp.max(d, axis=-1, keepdims=True))
        e_inter = jnp.exp(m_inter - m_t)
        qk = lax.dot_general(q, k.astype(BF16), (((1,), (1,)), ((), ())), preferred_element_type=F32)
        s = qk * jnp.exp(d - m_t)
        nd = _bdot(s, v_aug) + e_inter * _bdot(q, c_prev)
        num = nd[:, :ML_HEAD_DIM]
        den = nd[:, ML_HEAD_DIM:ML_HEAD_DIM + 1]
        outs.append(num / jnp.maximum(jnp.abs(den), jnp.exp(-m_t)))
        m_new = jnp.maximum(b_tot + m_prev, m_loc)
        decay = jnp.exp(b_tot + m_prev - m_new)
        gain = jnp.exp(m_loc - m_new)
        dc = lax.dot_general((k * e_end).astype(BF16), v_aug, (((0,), (0,)), ((), ())),
                             preferred_element_type=F32)
        c_scr[st] = decay * c_prev + gain * dc
        m_scr[st:st + 1, :] = jnp.broadcast_to(m_new, (1, LANES))
    return outs


ML_BATCH_ROWS = 1


def _mlstm_kernel(*refs):
    n_in = 8 * ML_BATCH_ROWS
    hf_ref, hb_ref, c_scr, m_scr = refs[n_in:]

    @pl.when(pl.program_id(1) == 0)
    def _():
        c_scr[...] = jnp.zeros_like(c_scr)
        m_scr[...] = jnp.zeros_like(m_scr)

    for r in range(ML_BATCH_ROWS):
        qf_ref, kf_ref, vf_ref, gf_ref, qb_ref, kb_ref, vb_ref, gb_ref = refs[8 * r:8 * r + 8]
        st = 2 * ML_HEADS * r
        outs_f = _mlstm_chunk(qf_ref, kf_ref, vf_ref, gf_ref, c_scr, m_scr, False, 0, ML_HEADS, st)
        outs_b = _mlstm_chunk(qb_ref, kb_ref, vb_ref, gb_ref, c_scr, m_scr, True, 2 * ML_HEADS, 3 * ML_HEADS,
                              st + ML_HEADS)
        hf_ref[r] = jnp.concatenate(outs_f, axis=1)
        hb_ref[r] = jnp.concatenate(outs_b, axis=1)


def _mixer_mlstm(proj2, bsz):
    nc = SEQ // ML_CHUNK
    rows = ML_BATCH_ROWS

    def specs(r, chunk_of):
        def at(col, width):
            def index(i, j):
                return pl.multiple_of((i * rows + r) * SEQ + chunk_of(j) * ML_CHUNK, ML_CHUNK), col
            return pl.BlockSpec((pl.Element(ML_CHUNK), pl.Element(width)), index)
        return [at(COL_DQ, GROUP_W), at(COL_DK, GROUP_W), at(COL_DV, GROUP_W), at(COL_GATES, LANES)]

    fwd_of = lambda j: j
    bwd_of = lambda j: nc - 1 - j
    in_specs = []
    for r in range(rows):
        in_specs += specs(r, fwd_of) + specs(r, bwd_of)
    out = lambda chunk_of: pl.BlockSpec((rows, ML_CHUNK, GROUP_W), lambda i, j: (i, chunk_of(j), 0))
    n_state = 2 * ML_HEADS * rows
    return pl.pallas_call(
        _mlstm_kernel,
        grid=(bsz // rows, nc),
        in_specs=in_specs,
        out_specs=[out(fwd_of), out(bwd_of)],
        out_shape=[jax.ShapeDtypeStruct((bsz, SEQ, GROUP_W), F32)] * 2,
        scratch_shapes=[pltpu.VMEM((n_state, ML_HEAD_DIM, 2 * ML_HEAD_DIM), F32), pltpu.VMEM((n_state, LANES), F32)],
        compiler_params=_cparams(("parallel", "arbitrary")),
        name="mlstm",
    )(*([proj2] * (8 * rows)))


def _outproj_kernel(lf_ref, lb_ref, ga_ref, yb_ref, yc_ref, mf_ref, mb_ref, o_ref, mg_ref, w_ref, x_ref, m_ref,
                    out_ref):
    ya = (jax.nn.gelu(ga_ref[...]) * (lf_ref[...] + lb_ref[...])).astype(BF16)
    normed = []
    for h in range(ML_HEADS):
        sl = slice(h * ML_HEAD_DIM, (h + 1) * ML_HEAD_DIM)
        hh = mf_ref[:, sl] + mb_ref[:, sl]
        ms = jnp.mean(hh * hh, axis=-1, keepdims=True)
        normed.append(hh * lax.rsqrt(ms + EPS) * mg_ref[:, sl])
    yd = (_sigmoid(o_ref[...]) * jnp.concatenate(normed, axis=1)).astype(BF16)
    acc = jnp.dot(ya, w_ref[0:GROUP_W, :], preferred_element_type=F32)
    acc += jnp.dot(yb_ref[...], w_ref[GROUP_W:2 * GROUP_W, :], preferred_element_type=F32)
    acc += jnp.dot(yc_ref[...], w_ref[2 * GROUP_W:3 * GROUP_W, :], preferred_element_type=F32)
    acc += jnp.dot(yd, w_ref[3 * GROUP_W:, :], preferred_element_type=F32)
    out_ref[...] = x_ref[...] + m_ref[2:3, :] * acc


def _outproj(lru_f, lru_b, proj2, y_b, y_c, ml_f, ml_b, ml_g, w_all, layer, x2, mod_l):
    n = x2.shape[0]
    tm = 512
    per_b = SEQ // tm
    grp = pl.BlockSpec((tm, GROUP_W), lambda i: (i, 0))
    return pl.pallas_call(
        _outproj_kernel,
        grid=(n // tm,),
        in_specs=[grp, grp, pl.BlockSpec((tm, GROUP_W), lambda i: (i, COL_AG // GROUP_W)), grp, grp, grp, grp,
                  pl.BlockSpec((pl.Element(tm), pl.Element(GROUP_W)), lambda i: (pl.multiple_of(i * tm, tm), COL_DO)),
                  pl.BlockSpec((1, GROUP_W), lambda i: (0, 0)),
                  pl.BlockSpec((None, D_MODEL, D_MODEL), lambda i: (layer, 0, 0)),
                  pl.BlockSpec((tm, D_MODEL), lambda i: (i, 0)),
                  pl.BlockSpec((None, 6, D_MODEL), lambda i: (i // per_b, 0, 0))],
        out_specs=pl.BlockSpec((tm, D_MODEL), lambda i: (i, 0)),
        out_shape=jax.ShapeDtypeStruct((n, D_MODEL), F32),
        compiler_params=_cparams(("parallel",)),
        name="out_proj",
    )(lru_f, lru_b, proj2, y_b, y_c, ml_f, ml_b, proj2, ml_g.reshape(1, GROUP_W), w_all, x2, mod_l)


def _ffn_kernel(x_ref, m_ref, g_ref, w1_ref, w3_ref, w2_ref, fg_ref, o_ref, h_scr, *, final):
    j = pl.program_id(1)

    @pl.when(j == 0)
    def _():
        h_scr[...] = _rms_mod(x_ref[...], g_ref[...], m_ref[4:5, :], m_ref[3:4, :]).astype(BF16)
        o_ref[...] = jnp.zeros_like(o_ref)

    h = h_scr[...]
    a = jnp.dot(h, w1_ref[...], preferred_element_type=F32)
    b = jnp.dot(h, w3_ref[...], preferred_element_type=F32)
    act = (a * _sigmoid(a)) * b
    o_ref[...] += jnp.dot(act.astype(BF16), w2_ref[...], preferred_element_type=F32)

    @pl.when(j == pl.num_programs(1) - 1)
    def _():
        y = x_ref[...] + m_ref[5:6, :] * o_ref[...]
        if final:
            ms = jnp.mean(y * y, axis=-1, keepdims=True)
            y = y * lax.rsqrt(ms + EPS) * fg_ref[...]
        o_ref[...] = y


def _ffn(x2, mod_l, g, w1_all, w3_all, w2_all, layer, final_g, final):
    n = x2.shape[0]
    tm, tf = 512, 512
    per_b = SEQ // tm
    row = lambda: pl.BlockSpec((1, D_MODEL), lambda i, j: (0, 0))
    return pl.pallas_call(
        functools.partial(_ffn_kernel, final=final),
        grid=(n // tm, D_FF // tf),
        in_specs=[pl.BlockSpec((tm, D_MODEL), lambda i, j: (i, 0)),
                  pl.BlockSpec((None, 6, D_MODEL), lambda i, j: (i // per_b, 0, 0)),
                  row(),
                  pl.BlockSpec((None, D_MODEL, tf), lambda i, j: (layer, 0, j)),
                  pl.BlockSpec((None, D_MODEL, tf), lambda i, j: (layer, 0, j)),
                  pl.BlockSpec((None, tf, D_MODEL), lambda i, j: (layer, j, 0)),
                  row()],
        out_specs=pl.BlockSpec((tm, D_MODEL), lambda i, j: (i, 0)),
        out_shape=jax.ShapeDtypeStruct((n, D_MODEL), F32),
        scratch_shapes=[pltpu.VMEM((tm, D_MODEL), BF16)],
        compiler_params=_cparams(("parallel", "arbitrary")),
        name="ffn",
    )(x2, mod_l, g.reshape(1, D_MODEL), w1_all, w3_all, w2_all, final_g.reshape(1, D_MODEL))


def kernel(x, c, w_in, b_in, w_out, norm_mix_g, norm_ffn_g, ada_w, ada_b, lru_conv_w, lru_conv_b, lru_wa, lru_ba, lru_wx, lru_bx, lru_lambda, att_q_norm_g, att_k_norm_g, hy_conv_w, hy_conv_b, hy_w1, hy_b1, hy_w2, hy_b2, hy_w3, hy_sin_freq, hy_decay, hy_skip, ml_norm_g, ffn_w1, ffn_w3, ffn_w2, final_g):
    bsz = x.shape[0]
    assert x.shape == (bsz, SEQ, D_MODEL) and bsz % 2 == 0
    n = bsz * SEQ
    mod = _ada_all(c, ada_w, ada_b)

    consts = {k: v.astype(BF16) for k, v in _dft_constants().items()}
    ah5, ssq = _hyena_filter_stage(consts, hy_w1, hy_b1, hy_w2, hy_b2, hy_w3, hy_sin_freq, hy_decay)

    pad = D_IN_PAD - D_IN
    w_in_b = jnp.pad(w_in.astype(BF16), ((0, 0), (0, 0), (0, pad)))
    b_in_p = jnp.pad(b_in, ((0, 0), (0, pad))).reshape(DEPTH, 1, D_IN_PAD)
    w_out_b = w_out.astype(BF16)
    w1_b, w3_b, w2_b = ffn_w1.astype(BF16), ffn_w3.astype(BF16), ffn_w2.astype(BF16)

    x2 = x.reshape(n, D_MODEL)
    for l in range(DEPTH):
        proj2 = _inproj(x2, mod[l], norm_mix_g[l], w_in_b, b_in_p, l)
        proj3 = proj2.reshape(bsz, SEQ, D_IN_PAD)
        lru_f, lru_b = _mixer_rglru(proj3, lru_conv_w[l], lru_conv_b[l], lru_wa[l], lru_ba[l], lru_wx[l], lru_bx[l],
                                    lru_lambda[l])
        y_b = _mixer_attention(proj3, att_q_norm_g[l], att_k_norm_g[l])
        y_c = _mixer_hyena(proj3, hy_conv_w[l], hy_conv_b[l], hy_skip[l], ah5, ssq, l, consts)
        ml_f, ml_b = _mixer_mlstm(proj2, bsz)
        flat = lambda a: a.reshape(n, GROUP_W)
        x2 = _outproj(flat(lru_f), flat(lru_b), proj2, flat(y_b), flat(y_c), flat(ml_f), flat(ml_b), ml_norm_g[l],
                      w_out_b, l, x2, mod[l])
        x2 = _ffn(x2, mod[l], norm_ffn_g[l], w1_b, w3_b, w2_b, l, final_g, final=(l == DEPTH - 1))
    return x2.reshape(bsz, SEQ, D_MODEL)
```

```python
import functools
import math

import numpy as np
import jax
import jax.numpy as jnp
from jax import lax
from jax.experimental import pallas as pl
from jax.experimental.pallas import tpu as pltpu

F32 = jnp.float32
BF16 = jnp.bfloat16
HIGHEST = lax.Precision.HIGHEST

D_MODEL = 2048
SEQ = 4096
DEPTH = 2
GROUP_W = 512
LRU_BLOCKS = 8
LRU_C = 8.0
ATT_HEADS = 8
ATT_KV_HEADS = 2
ATT_GROUP = ATT_HEADS // ATT_KV_HEADS
ATT_HEAD_DIM = 64
ROPE_AXIS = ATT_HEAD_DIM // 2
ROPE_THETA = 10000.0
GRID_W = 64
ATT_HEAD_GROUP = 2
ATT_V_ROWS = 80
LOG2_E = math.log2(math.e)
HY_W = GROUP_W
HY_ORDER = 2
HY_BANDS = 8
HY_EMB = 2 * HY_BANDS + 1
HY_EMB_PAD = 32
HY_FFN = 64
ML_HEADS = 4
ML_HEAD_DIM = 128
ML_CHUNK = 128
D_FF = 5632
EPS = 1e-6
IN_SIZES = (512, 512, 512, 128, 128, 1536, 512, 512, 512, 512, 16)
D_IN = sum(IN_SIZES)
D_IN_PAD = 5632

COL_AX, COL_AG, COL_BQ, COL_BK, COL_BV, COL_CU = 0, 512, 1024, 1536, 1664, 1792
COL_DQ, COL_DK, COL_DV, COL_DO, COL_GATES = 3328, 3840, 4352, 4864, 5376
LANES = 128

FFT_N = 2 * SEQ
FFT_N1 = 64
FFT_N2 = 128
PITCH = 136

VMEM_LIMIT = 56 * 1024 * 1024


def _cparams(sem, vmem=VMEM_LIMIT):
    return pltpu.CompilerParams(dimension_semantics=sem, vmem_limit_bytes=vmem)


def _bdot(a, b):
    return jnp.dot(a.astype(BF16), b.astype(BF16), preferred_element_type=F32)


def _sigmoid(x):
    return 0.5 * jnp.tanh(0.5 * x) + 0.5


def _log_sigmoid(x):
    return jnp.minimum(x, 0.0) - jnp.log1p(jnp.exp(-jnp.abs(x)))


def _softplus(x):
    return jnp.maximum(x, 0.0) + jnp.log1p(jnp.exp(-jnp.abs(x)))


def _ada_kernel(c_ref, w_ref, b_ref, o_ref):
    c = c_ref[...]
    o_ref[...] = _bdot(c * _sigmoid(c), w_ref[...]) + b_ref[...]


def _ada_all(c, ada_w, ada_b):
    bsz = c.shape[0]
    rows = 8
    cp = jnp.zeros((rows, D_MODEL), F32).at[:bsz].set(c)
    tn = 2048
    out = pl.pallas_call(
        _ada_kernel,
        grid=(DEPTH, 6 * D_MODEL // tn),
        in_specs=[pl.BlockSpec((rows, D_MODEL), lambda l, j: (0, 0)),
                  pl.BlockSpec((None, D_MODEL, tn), lambda l, j: (l, 0, j)),
                  pl.BlockSpec((None, 1, tn), lambda l, j: (l, 0, j))],
        out_specs=pl.BlockSpec((None, rows, tn), lambda l, j: (l, 0, j)),
        out_shape=jax.ShapeDtypeStruct((DEPTH, rows, 6 * D_MODEL), F32),
        compiler_params=_cparams(("parallel", "parallel")),
        name="ada_mod",
    )(cp, ada_w, ada_b.reshape(DEPTH, 1, 6 * D_MODEL))
    return out[:, :bsz].reshape(DEPTH, bsz, 6, D_MODEL)


def _rms_mod(x, g, scale, shift):
    ms = jnp.mean(x * x, axis=-1, keepdims=True)
    return (x * lax.rsqrt(ms + EPS)) * (g * (1.0 + scale)) + shift


def _inproj_kernel(x_ref, m_ref, g_ref, w_ref, b_ref, o_ref, h_scr):
    @pl.when(pl.program_id(1) == 0)
    def _():
        h_scr[...] = _rms_mod(x_ref[...], g_ref[...], m_ref[1:2, :], m_ref[0:1, :]).astype(BF16)

    o_ref[...] = jnp.dot(h_scr[...], w_ref[...], preferred_element_type=F32) + b_ref[...]


def _inproj(x2, mod_l, g, w_all, b_all, layer):
    n = x2.shape[0]
    tm, tn = 1024, 1408
    per_b = SEQ // tm
    return pl.pallas_call(
        _inproj_kernel,
        grid=(n // tm, D_IN_PAD // tn),
        in_specs=[pl.BlockSpec((tm, D_MODEL), lambda i, j: (i, 0)),
                  pl.BlockSpec((None, 6, D_MODEL), lambda i, j: (i // per_b, 0, 0)),
                  pl.BlockSpec((1, D_MODEL), lambda i, j: (0, 0)),
                  pl.BlockSpec((None, D_MODEL, tn), lambda i, j: (layer, 0, j)),
                  pl.BlockSpec((None, 1, tn), lambda i, j: (layer, 0, j))],
        out_specs=pl.BlockSpec((tm, tn), lambda i, j: (i, j)),
        out_shape=jax.ShapeDtypeStruct((n, D_IN_PAD), F32),
        scratch_shapes=[pltpu.VMEM((tm, D_MODEL), BF16)],
        compiler_params=_cparams(("parallel", "arbitrary")),
        name="in_proj",
    )(x2, mod_l, g.reshape(1, D_MODEL), w_all, b_all)


def _shifted(ext, off, rows):
    total = ext.shape[0]
    if off == 0:
        return ext[8:8 + rows]
    return pltpu.roll(ext, (-off) % total, axis=0)[8:8 + rows]


def _dwconv_ext(ext, w_ref, b_ref, left, rows):
    out = b_ref[...]
    for j in range(w_ref.shape[0]):
        out = out + _shifted(ext, j - left, rows) * w_ref[j:j + 1, :]
    return out


def _dwconv_tile(x_ref, p_ref, n_ref, w_ref, b_ref, ext_ref, tile, n_tiles, left):
    rows = x_ref.shape[0]
    prev = jnp.where(tile > 0, p_ref[...], 0.0)
    nxt = jnp.where(tile < n_tiles - 1, n_ref[...], 0.0)
    outs = []
    for s in range(ext_ref.shape[0]):
        ln = slice(s * LANES, (s + 1) * LANES)
        ext_ref[s, 0:8, :] = prev[:, ln]
        ext_ref[s, 8:8 + rows, :] = x_ref[:, ln]
        ext_ref[s, 8 + rows:16 + rows, :] = nxt[:, ln]
        out = b_ref[:, ln]
        for j in range(w_ref.shape[0]):
            start = 8 + j - left
            out = out + ext_ref[s, start:start + rows, :] * w_ref[j:j + 1, ln]
        outs.append(out)
    return jnp.concatenate(outs, axis=1)


def _dwconv_rows(ref, lo, rows, w_ref, b_ref, left):
    taps = w_ref.shape[0]
    if lo - left >= 0 and lo + rows + (taps - 1 - left) <= SEQ:
        out = b_ref[...]
        for j in range(taps):
            start = lo + j - left
            out = out + ref[start:start + rows, :] * w_ref[j:j + 1, :]
        return out
    zeros = jnp.zeros((8, ref.shape[1]), F32)
    prev = ref[lo - 8:lo, :] if lo > 0 else zeros
    nxt = ref[lo + rows:lo + rows + 8, :] if lo + rows < SEQ else zeros
    ext = jnp.concatenate([prev, ref[lo:lo + rows, :], nxt], axis=0)
    return _dwconv_ext(ext, w_ref, b_ref, left, rows)


def _halo_specs(tile_rows, width, col_block, tile_of):
    r8 = tile_rows // 8
    last8 = SEQ // 8 - 1

    def main(b, t):
        return (b, tile_of(t), col_block)

    def prev(b, t):
        return (b, jnp.maximum(tile_of(t) * r8 - 1, 0), col_block)

    def nxt(b, t):
        return (b, jnp.minimum((tile_of(t) + 1) * r8, last8), col_block)

    return [pl.BlockSpec((None, tile_rows, width), main),
            pl.BlockSpec((None, 8, width), prev),
            pl.BlockSpec((None, 8, width), nxt)]


def _lru_gates(xc, wg, bg, lam):
    gates = _bdot(xc, wg) + bg
    r = _sigmoid(gates[:, :GROUP_W])
    i = _sigmoid(gates[:, GROUP_W:])
    log_a = (-LRU_C * _softplus(-lam)) * r
    a = jnp.exp(log_a)
    th = jnp.tanh(log_a)
    u = jnp.sqrt(-2.0 * th / (1.0 - th)) * (i * xc)
    return a, u


def _scan8(a, u, ridx, reverse):
    for k in (1, 2, 4):
        if reverse:
            keep = ridx < 8 - k
            sh = 8 - k
        else:
            keep = ridx >= k
            sh = k
        a_sh = jnp.where(keep, pltpu.roll(a, sh, axis=0), 1.0)
        u_sh = jnp.where(keep, pltpu.roll(u, sh, axis=0), 0.0)
        u = a * u_sh + u
        a = a * a_sh
    return a, u


def _lru_kernel(xf_ref, pf_ref, nf_ref, xb_ref, pb_ref, nb_ref, cw_ref, cb_ref, wg_ref, bg_ref, lam_ref,
                hf_ref, hb_ref, af_scr, uf_scr, ab_scr, ub_scr, cf_scr, cb_scr, ext_scr, c_scr, *, n_tiles):
    t = pl.program_id(1)
    n_chunks = af_scr.shape[0]
    xc = _dwconv_tile(xf_ref, pf_ref, nf_ref, cw_ref, cb_ref, ext_scr.at[0], t, n_tiles, 2)
    a, u = _lru_gates(xc, wg_ref[0], bg_ref[0], lam_ref[0])
    af_scr[...] = a.reshape(n_chunks, 8, GROUP_W)
    uf_scr[...] = u.reshape(n_chunks, 8, GROUP_W)
    xc = _dwconv_tile(xb_ref, pb_ref, nb_ref, cw_ref, cb_ref, ext_scr.at[1], n_tiles - 1 - t, n_tiles, 2)
    a, u = _lru_gates(xc, wg_ref[1], bg_ref[1], lam_ref[1])
    ab_scr[...] = a.reshape(n_chunks, 8, GROUP_W)
    ub_scr[...] = u.reshape(n_chunks, 8, GROUP_W)

    @pl.when(t == 0)
    def _():
        c_scr[...] = jnp.zeros_like(c_scr)

    ridx = lax.broadcasted_iota(jnp.int32, (8, GROUP_W), 0)

    def local(c, carry):
        a, u = _scan8(af_scr[c], uf_scr[c], ridx, False)
        af_scr[c] = a
        uf_scr[c] = u
        a, u = _scan8(ab_scr[c], ub_scr[c], ridx, True)
        ab_scr[c] = a
        ub_scr[c] = u
        return carry

    lax.fori_loop(0, n_chunks, local, 0, unroll=4)

    def chain(c, carry):
        cf, cb = carry
        cf_scr[c] = jnp.broadcast_to(cf, (8, GROUP_W))
        cf = af_scr[c][7:8, :] * cf + uf_scr[c][7:8, :]
        cr = n_chunks - 1 - c
        cb_scr[cr] = jnp.broadcast_to(cb, (8, GROUP_W))
        cb = ab_scr[cr][0:1, :] * cb + ub_scr[cr][0:1, :]
        return cf, cb

    cf, cb = lax.fori_loop(0, n_chunks, chain, (c_scr[0:1, :], c_scr[1:2, :]), unroll=4)
    c_scr[0:1, :] = cf
    c_scr[1:2, :] = cb

    def apply(c, carry):
        r0 = pl.multiple_of(c * 8, 8)
        hf_ref[pl.ds(r0, 8), :] = uf_scr[c] + af_scr[c] * cf_scr[c]
        hb_ref[pl.ds(r0, 8), :] = ub_scr[c] + ab_scr[c] * cb_scr[c]
        return carry

    lax.fori_loop(0, n_chunks, apply, 0, unroll=4)


def _block_diag(w):
    nb, k, j = w.shape
    eye = jnp.eye(nb, dtype=w.dtype)
    return jnp.einsum('nkj,nm->nkmj', w, eye).reshape(nb * k, nb * j)


def _mixer_rglru(proj3, conv_w, conv_b, wa, ba, wx, bx, lam):
    bsz = proj3.shape[0]
    ts = 512
    n_tiles = SEQ // ts
    wg = jnp.stack([jnp.concatenate([_block_diag(wa[d]), _block_diag(wx[d])], axis=1) for d in range(2)]).astype(BF16)
    bg = jnp.stack([jnp.concatenate([ba[d], bx[d]]).reshape(1, 2 * GROUP_W) for d in range(2)])
    small = lambda *shape: pl.BlockSpec(shape, lambda b, t: (0,) * len(shape))
    fwd_of = lambda t: t
    bwd_of = lambda t: n_tiles - 1 - t
    out = lambda tile_of: pl.BlockSpec((None, ts, GROUP_W), lambda b, t: (b, tile_of(t), 0))
    tile_scr = pltpu.VMEM((ts // 8, 8, GROUP_W), F32)
    return pl.pallas_call(
        functools.partial(_lru_kernel, n_tiles=n_tiles),
        grid=(bsz, n_tiles),
        in_specs=_halo_specs(ts, GROUP_W, COL_AX // GROUP_W, fwd_of) + _halo_specs(ts, GROUP_W, COL_AX // GROUP_W, bwd_of)
        + [small(4, GROUP_W), small(1, GROUP_W), small(2, GROUP_W, 2 * GROUP_W), small(2, 1, 2 * GROUP_W),
           small(2, 1, GROUP_W)],
        out_specs=[out(fwd_of), out(bwd_of)],
        out_shape=[jax.ShapeDtypeStruct((bsz, SEQ, GROUP_W), F32)] * 2,
        scratch_shapes=[tile_scr] * 6 + [pltpu.VMEM((2, GROUP_W // LANES, ts + 16, LANES), F32),
                                         pltpu.VMEM((8, GROUP_W), F32)],
        compiler_params=_cparams(("parallel", "arbitrary")),
        name="rglru",
    )(proj3, proj3, proj3, proj3, proj3, proj3, conv_w, conv_b.reshape(1, GROUP_W), wg, bg,
      lam.reshape(2, 1, GROUP_W))


def _split_dot(x, m_ref):
    hi = x.astype(BF16)
    lo = (x - hi.astype(F32)).astype(BF16)
    m = m_ref[...]
    return (jnp.dot(hi, m, preferred_element_type=F32) + jnp.dot(lo, m, preferred_element_type=F32))


def _norm_rope(x, gain, m_ref, cos, sins):
    width = x.shape[1]
    ms = _split_dot(x * x, m_ref)
    xn = x * lax.rsqrt(ms + EPS) * gain
    lane = lax.broadcasted_iota(jnp.int32, xn.shape, 1)
    first = (lane % ROPE_AXIS) < (ROPE_AXIS // 2)
    half = ROPE_AXIS // 2
    partner = jnp.where(first, pltpu.roll(xn, width - half, axis=1), pltpu.roll(xn, half, axis=1))
    return xn * cos + partner * sins


def _attn_prep_kernel(q_ref, kv_ref, gq_ref, gk_ref, mq_ref, mk_ref, cos_ref, sin_ref,
                      qo_ref, kt_ref, vt_ref):
    cos = cos_ref[...]
    sins = sin_ref[...]
    cos_q = jnp.concatenate([cos] * (GROUP_W // LANES), axis=1)
    sin_q = jnp.concatenate([sins] * (GROUP_W // LANES), axis=1)
    q = _norm_rope(q_ref[...], gq_ref[...], mq_ref, cos_q, sin_q)
    qo_ref[...] = (q * (ATT_HEAD_DIM ** -0.5 * LOG2_E)).astype(BF16)
    kv = kv_ref[...]
    kt_ref[...] = _norm_rope(kv[:, :LANES], gk_ref[...], mk_ref, cos, sins).T.astype(BF16)
    v_t = kv[:, LANES:].T
    row = lax.broadcasted_iota(jnp.int32, (ATT_V_ROWS - ATT_HEAD_DIM, v_t.shape[1]), 0)
    ones_rows = jnp.where(row == 0, 1.0, 0.0)
    for g in range(ATT_KV_HEADS):
        vt_ref[g] = jnp.concatenate([v_t[g * ATT_HEAD_DIM:(g + 1) * ATT_HEAD_DIM, :], ones_rows], axis=0).astype(BF16)


def _rope_tables():
    rows = SEQ // GRID_W
    row = jnp.repeat(jnp.arange(rows, dtype=F32), GRID_W)
    col = jnp.tile(jnp.arange(GRID_W, dtype=F32), rows)
    inv = ROPE_THETA ** (-jnp.arange(0, ROPE_AXIS, 2, dtype=F32) / ROPE_AXIS)
    ar = row[:, None] * inv
    ac = col[:, None] * inv
    ang = jnp.concatenate([ar, ar, ac, ac], axis=1)
    sign = jnp.concatenate([-jnp.ones((ROPE_AXIS // 2,), F32), jnp.ones((ROPE_AXIS // 2,), F32)] * 2)
    cos = jnp.tile(jnp.cos(ang), (1, 2))
    sins = jnp.tile(jnp.sin(ang) * sign, (1, 2))
    return cos, sins


def _head_mean_matrix(width):
    idx = np.arange(width) // ATT_HEAD_DIM
    return jnp.asarray((idx[:, None] == idx[None, :]).astype(np.float32) / ATT_HEAD_DIM, dtype=BF16)


def _attn_kernel(q_ref, kt_ref, vt_ref, o_ref):
    def scores(h):
        g = h // ATT_GROUP
        q = q_ref[:, h * ATT_HEAD_DIM:(h + 1) * ATT_HEAD_DIM]
        return jnp.dot(q, kt_ref[g * ATT_HEAD_DIM:(g + 1) * ATT_HEAD_DIM, :], preferred_element_type=F32)

    def probs(s):
        return jnp.exp2(s - jnp.max(s, axis=-1, keepdims=True)).astype(BF16)

    def values(h, p):
        o_aug = lax.dot_general(vt_ref[h // ATT_GROUP], p, (((1,), (1,)), ((), ())), preferred_element_type=F32)
        return (o_aug[:ATT_HEAD_DIM] / o_aug[ATT_HEAD_DIM:ATT_HEAD_DIM + 1]).T

    outs = []
    for h0 in range(0, ATT_HEADS, ATT_HEAD_GROUP):
        heads = range(h0, h0 + ATT_HEAD_GROUP)
        ps = [probs(s) for s in [scores(h) for h in heads]]
        outs += [values(h, p) for h, p in zip(heads, ps)]
    o_ref[...] = jnp.concatenate(outs, axis=1).astype(o_ref.dtype)


def _mixer_attention(proj3, q_g, k_g):
    bsz = proj3.shape[0]
    ts = 512
    cos, sins = _rope_tables()
    gq = jnp.tile(q_g, ATT_HEADS).reshape(1, GROUP_W)
    gk = jnp.tile(k_g, ATT_KV_HEADS).reshape(1, LANES)
    const = lambda shape: pl.BlockSpec(shape, lambda b, t: (0, 0))
    qp, kt, vt = pl.pallas_call(
        _attn_prep_kernel,
        grid=(bsz, SEQ // ts),
        in_specs=[pl.BlockSpec((None, ts, GROUP_W), lambda b, t: (b, t, COL_BQ // GROUP_W)),
                  pl.BlockSpec((None, ts, 2 * LANES), lambda b, t: (b, t, COL_BK // (2 * LANES))),
                  const((1, GROUP_W)), const((1, LANES)), const((GROUP_W, GROUP_W)), const((LANES, LANES)),
                  pl.BlockSpec((ts, LANES), lambda b, t: (t, 0)),
                  pl.BlockSpec((ts, LANES), lambda b, t: (t, 0))],
        out_specs=[pl.BlockSpec((None, ts, GROUP_W), lambda b, t: (b, t, 0)),
                   pl.BlockSpec((None, LANES, ts), lambda b, t: (b, 0, t)),
                   pl.BlockSpec((None, ATT_KV_HEADS, ATT_V_ROWS, ts), lambda b, t: (b, 0, 0, t))],
        out_shape=[jax.ShapeDtypeStruct((bsz, SEQ, GROUP_W), BF16),
                   jax.ShapeDtypeStruct((bsz, LANES, SEQ), BF16),
                   jax.ShapeDtypeStruct((bsz, ATT_KV_HEADS, ATT_V_ROWS, SEQ), BF16)],
        compiler_params=_cparams(("parallel", "parallel")),
        name="attn_prep",
    )(proj3, proj3, gq, gk, _head_mean_matrix(GROUP_W), _head_mean_matrix(LANES), cos, sins)

    tq = 512
    return pl.pallas_call(
        _attn_kernel,
        grid=(bsz, SEQ // tq),
        in_specs=[pl.BlockSpec((None, tq, GROUP_W), lambda b, t: (b, t, 0)),
                  pl.BlockSpec((None, LANES, SEQ), lambda b, t: (b, 0, 0)),
                  pl.BlockSpec((None, ATT_KV_HEADS, ATT_V_ROWS, SEQ), lambda b, t: (b, 0, 0, 0))],
        out_specs=pl.BlockSpec((None, tq, GROUP_W), lambda b, t: (b, t, 0)),
        out_shape=jax.ShapeDtypeStruct((bsz, SEQ, GROUP_W), BF16),
        compiler_params=_cparams(("parallel", "parallel")),
        name="attention",
    )(qp, kt, vt)


def _hy_features():
    L = SEQ
    pos = jnp.arange(L, dtype=F32)
    t = pos / max(L - 1, 1)
    bands = jnp.linspace(1e-4, HY_BANDS - 1, HY_BANDS, dtype=F32)
    ang = (2.0 * math.pi * pos / L)[:, None] * bands
    feat = jnp.concatenate([t[:, None], jnp.cos(ang), -jnp.sin(ang)], axis=-1)
    feat = jnp.pad(feat, ((0, 0), (0, HY_EMB_PAD - HY_EMB)))
    rev_idx = np.concatenate([[0], np.arange(L - 1, 0, -1)])
    return feat, feat[rev_idx]


def _hy_mlp(feat, feat_t, w1t_ref, b1_ref, w2t_ref, b2_ref, w3_ref, sf_ref, dec_ref):
    sf = sf_ref[...]
    h = jnp.sin(sf * (jnp.dot(w1t_ref[...], feat_t, precision=HIGHEST, preferred_element_type=F32) + b1_ref[...]))
    h = jnp.sin(sf * (jnp.dot(w2t_ref[...], h, precision=HIGHEST, preferred_element_type=F32) + b2_ref[...]))
    out = jnp.dot(h.T, w3_ref[...], precision=HIGHEST, preferred_element_type=F32)
    return out * jnp.exp(-feat[:, 0:1] * jnp.abs(dec_ref[...]))


def _hy_filter_kernel(ff_ref, fft_ref, fr_ref, frt_ref, w1t_ref, b1_ref, w2t_ref, b2_ref, w3f_ref, w3b_ref, sf_ref,
                      decf_ref, decb_ref, of_ref, ob_ref, ssq_ref):
    i = pl.program_id(1)
    hf = _hy_mlp(ff_ref[...], fft_ref[...], w1t_ref, b1_ref, w2t_ref, b2_ref, w3f_ref, sf_ref, decf_ref)
    hb = _hy_mlp(fr_ref[...], frt_ref[...], w1t_ref, b1_ref, w2t_ref, b2_ref, w3b_ref, sf_ref, decb_ref)

    @pl.when(i == 0)
    def _():
        ssq_ref[...] = jnp.zeros_like(ssq_ref)

    ssq_ref[0:1, :] += jnp.sum(hf * hf + hb * hb, axis=0, keepdims=True)
    of_ref[...] = hf
    row = lax.broadcasted_iota(jnp.int32, hb.shape, 0)
    ob_ref[...] = jnp.where(jnp.logical_and(i == 0, row == 0), 0.0, hb)


def _hy_filters(hy_w1, hy_b1, hy_w2, hy_b2, hy_w3, hy_sin_freq, hy_decay):
    feat, feat_rev = _hy_features()
    tr = 512
    cw = HY_ORDER * HY_W
    w1t = jnp.transpose(jnp.pad(hy_w1, ((0, 0), (0, HY_EMB_PAD - HY_EMB), (0, 0))), (0, 2, 1))
    w2t = jnp.transpose(hy_w2, (0, 2, 1))
    w3 = hy_w3.reshape(DEPTH, HY_FFN, HY_ORDER, 2, HY_W)
    dec = hy_decay.reshape(DEPTH, HY_ORDER, 2, HY_W)
    w3f = w3[:, :, :, 0].reshape(DEPTH, HY_FFN, cw)
    w3b = w3[:, :, :, 1].reshape(DEPTH, HY_FFN, cw)
    decf = dec[:, :, 0].reshape(DEPTH, 1, cw)
    decb = dec[:, :, 1].reshape(DEPTH, 1, cw)
    col = lambda a: a.reshape(DEPTH, HY_FFN, 1)
    row_spec = pl.BlockSpec((tr, HY_EMB_PAD), lambda l, i: (i, 0))
    rowt_spec = pl.BlockSpec((HY_EMB_PAD, tr), lambda l, i: (0, i))
    per_layer = lambda a, b: pl.BlockSpec((None, a, b), lambda l, i: (l, 0, 0))
    out_spec = pl.BlockSpec((None, tr, cw), lambda l, i: (l, i, 0))
    return pl.pallas_call(
        _hy_filter_kernel,
        grid=(DEPTH, SEQ // tr),
        in_specs=[row_spec, rowt_spec, row_spec, rowt_spec, per_layer(HY_FFN, HY_EMB_PAD), per_layer(HY_FFN, 1),
                  per_layer(HY_FFN, HY_FFN), per_layer(HY_FFN, 1), per_layer(HY_FFN, cw), per_layer(HY_FFN, cw),
                  per_layer(HY_FFN, 1), per_layer(1, cw), per_layer(1, cw)],
        out_specs=[out_spec, out_spec, per_layer(8, cw)],
        out_shape=[jax.ShapeDtypeStruct((DEPTH, SEQ, cw), F32)] * 2 + [jax.ShapeDtypeStruct((DEPTH, 8, cw), F32)],
        compiler_params=_cparams(("arbitrary", "arbitrary")),
        name="hyena_filter",
    )(feat, feat.T, feat_rev, feat_rev.T, w1t, col(hy_b1), w2t, col(hy_b2), w3f, w3b, col(hy_sin_freq), decf, decb)


def _dft_constants():
    n1 = np.arange(FFT_N1)
    n2 = np.arange(FFT_N2)
    f1 = np.exp(-2j * np.pi * np.outer(n1, n1) / FFT_N1)
    stack = lambda m: np.concatenate([m.real, m.imag], axis=0)
    half = FFT_N1 // 2
    sig_l = stack(f1[:, :half])
    sig_r = np.concatenate([-f1[:, :half].imag, f1[:, :half].real], axis=0)
    fil_r = stack(f1[:, half:])
    f2 = np.exp(-2j * np.pi * np.outer(n2, n2) / FFT_N2)
    tw = np.exp(-2j * np.pi * np.outer(n1, n2) / FFT_N)
    fwd = f2[None, :, :] * tw[:, None, :]
    inv = np.conj(np.transpose(fwd, (0, 2, 1))) / FFT_N
    block = lambda m: np.concatenate([np.concatenate([m.real, -m.imag], axis=2),
                                      np.concatenate([m.imag, m.real], axis=2)], axis=1)
    g1 = np.conj(f1[:half, :])
    out_l = stack(g1)
    out_r = np.concatenate([-g1.imag, g1.real], axis=0)
    as32 = lambda a: jnp.asarray(a.astype(np.float32))
    return dict(sig_l=as32(sig_l), sig_r=as32(sig_r), fil_r=as32(fil_r), fwd=as32(block(fwd)),
                inv=as32(block(inv)), out_l=as32(out_l), out_r=as32(out_r))


_N2_GROUP = 8


def _dft_a_kernel(*refs, conv):
    if conv:
        u0_ref, u1_ref, cw_ref, cb_ref, ml_ref, mr_ref, o_ref, xs, ysc = refs
    else:
        u0_ref, u1_ref, ml_ref, mr_ref, o_ref, xs, ysc = refs
    half = FFT_N1 // 2
    for r, u_ref in enumerate((u0_ref, u1_ref)):
        for n1 in range(half):
            lo = FFT_N2 * n1
            rows = _dwconv_rows(u_ref, lo, FFT_N2, cw_ref, cb_ref, 1) if conv else u_ref[lo:lo + FFT_N2, :]
            xs[r, PITCH * n1:PITCH * n1 + FFT_N2, :] = rows
    ml = ml_ref[...]
    mr = mr_ref[...]

    def body(g, carry):
        n2 = g * _N2_GROUP
        x0 = jnp.concatenate([xs[0, pl.ds(n2 + i, half, stride=PITCH), :] for i in range(_N2_GROUP)], axis=1)
        x1 = jnp.concatenate([xs[1, pl.ds(n2 + i, half, stride=PITCH), :] for i in range(_N2_GROUP)], axis=1)
        y = _bdot(ml, x0) + _bdot(mr, x1)
        for i in range(_N2_GROUP):
            ysc[pl.ds(n2 + i, 2 * FFT_N1, stride=PITCH), :] = y[:, LANES * i:LANES * (i + 1)]
        return carry

    lax.fori_loop(0, FFT_N2 // _N2_GROUP, body, 0)
    for row in range(2 * FFT_N1):
        o_ref[FFT_N2 * row:FFT_N2 * (row + 1), :] = ysc[PITCH * row:PITCH * row + FFT_N2, :].astype(o_ref.dtype)


def _dft_a(ml, mr, srcs, groups, n_slabs, conv_args=None):
    conv = conv_args is not None
    slab = lambda arr_map: pl.BlockSpec((None, SEQ, LANES), arr_map)
    mspec = pl.BlockSpec((2 * FFT_N1, FFT_N1 // 2), lambda g, s: (0, 0))
    in_specs = [slab(srcs[0][1]), slab(srcs[1][1])]
    args = [srcs[0][0], srcs[1][0]]
    if conv:
        cw, cb, col0 = conv_args
        in_specs += [pl.BlockSpec((cw.shape[0], LANES), lambda g, s: (0, col0 + s)),
                     pl.BlockSpec((1, LANES), lambda g, s: (0, col0 + s))]
        args += [cw, cb]
    rows = 2 * FFT_N1 * FFT_N2
    return pl.pallas_call(
        functools.partial(_dft_a_kernel, conv=conv),
        grid=(groups, n_slabs),
        in_specs=in_specs + [mspec, mspec],
        out_specs=pl.BlockSpec((None, rows, LANES), lambda g, s: (g, 0, s)),
        out_shape=jax.ShapeDtypeStruct((groups, rows, n_slabs * LANES), BF16),
        scratch_shapes=[pltpu.VMEM((2, (FFT_N1 // 2) * PITCH, LANES), F32),
                        pltpu.VMEM((2 * FFT_N1 * PITCH, LANES), F32)],
        compiler_params=_cparams(("parallel", "parallel")),
        name="dft_a",
    )(*args, ml, mr)


def _dft_mid_kernel(f_ref, g_ref, ah_ref, ssq_ref, a_ref, o_ref):
    scale = lax.rsqrt(ssq_ref[0:1, :] + EPS)
    for kk in range(f_ref.shape[0]):
        f = f_ref[kk]
        g = g_ref[kk]
        h = _bdot(f, jnp.concatenate([ah_ref[0, kk], ah_ref[1, kk]], axis=0)) * scale
        hr = h[:FFT_N2]
        hi = h[FFT_N2:]
        for p in range(a_ref.shape[0]):
            y = _bdot(f, jnp.concatenate([a_ref[p, 0, kk], a_ref[p, 1, kk]], axis=0))
            yr = y[:FFT_N2]
            yi = y[FFT_N2:]
            z = jnp.concatenate([yr * hr - yi * hi, yr * hi + yi * hr], axis=0)
            w = _bdot(g, z)
            o_ref[p, 0, kk] = w[:FFT_N2].astype(o_ref.dtype)
            o_ref[p, 1, kk] = w[FFT_N2:].astype(o_ref.dtype)


def _dft_mid(fwd, inv, ah5, ssq, layer, order, a5):
    pairs = a5.shape[0]
    kb = 4
    blk = pl.BlockSpec((pairs, 2, kb, FFT_N2, HY_W), lambda k: (0, 0, k, 0, 0))
    mat = pl.BlockSpec((kb, 2 * FFT_N2, 2 * FFT_N2), lambda k: (k, 0, 0))
    return pl.pallas_call(
        _dft_mid_kernel,
        grid=(FFT_N1 // kb,),
        in_specs=[mat, mat,
                  pl.BlockSpec((None, 2, kb, FFT_N2, HY_W), lambda k: (layer, 0, k, 0, order)),
                  pl.BlockSpec((None, 8, HY_W), lambda k: (layer, 0, order)),
                  blk],
        out_specs=blk,
        out_shape=jax.ShapeDtypeStruct(a5.shape, BF16),
        compiler_params=_cparams(("parallel",)),
        name="dft_mid",
    )(fwd, inv, ah5, ssq, a5)


def _dft_c_kernel(*refs, u_conv):
    if u_conv:
        (b_ref, u0_ref, u1_ref, g0_ref, g1_ref, ucw_ref, ucb_ref, gcw_ref, gcb_ref, ml_ref, mr_ref, skip_ref,
         o_ref, bs, ys) = refs
    else:
        b_ref, u0_ref, u1_ref, g0_ref, g1_ref, gcw_ref, gcb_ref, ml_ref, mr_ref, skip_ref, o_ref, bs, ys = refs
    half = FFT_N1 // 2
    ml = ml_ref[...]
    mr = mr_ref[...]
    im0 = FFT_N1 * PITCH
    for row in range(2 * FFT_N1):
        bs[PITCH * row:PITCH * row + FFT_N2, :] = b_ref[FFT_N2 * row:FFT_N2 * (row + 1), :].astype(F32)

    def body(g, carry):
        n2 = g * _N2_GROUP
        br = jnp.concatenate([bs[pl.ds(n2 + i, FFT_N1, stride=PITCH), :] for i in range(_N2_GROUP)], axis=1)
        bi = jnp.concatenate([bs[pl.ds(im0 + n2 + i, FFT_N1, stride=PITCH), :] for i in range(_N2_GROUP)], axis=1)
        y = _bdot(ml, br) + _bdot(mr, bi)
        for i in range(_N2_GROUP):
            ys[0, pl.ds(n2 + i, half, stride=PITCH), :] = y[:half, LANES * i:LANES * (i + 1)]
            ys[1, pl.ds(n2 + i, half, stride=PITCH), :] = y[half:, LANES * i:LANES * (i + 1)]
        return carry

    lax.fori_loop(0, FFT_N2 // _N2_GROUP, body, 0)
    skip = skip_ref[...]
    for r, (u_ref, g_ref) in enumerate(((u0_ref, g0_ref), (u1_ref, g1_ref))):
        for n1 in range(half):
            lo = FFT_N2 * n1
            u = _dwconv_rows(u_ref, lo, FFT_N2, ucw_ref, ucb_ref, 1) if u_conv else u_ref[lo:lo + FFT_N2, :]
            gate = _dwconv_rows(g_ref, lo, FFT_N2, gcw_ref, gcb_ref, 1)
            conv = ys[r, PITCH * n1:PITCH * n1 + FFT_N2, :]
            o_ref[r, lo:lo + FFT_N2, :] = (gate * (conv + u * skip)).astype(o_ref.dtype)


def _dft_c(ml, mr, b3, u_src, gate_src, u_conv_args, gate_conv_args, skip, out_dtype):
    pairs = b3.shape[0]
    n_slabs = HY_W // LANES
    u_conv = u_conv_args is not None
    slab = lambda m: pl.BlockSpec((None, SEQ, LANES), m)
    wspecs = lambda cw, col0: [pl.BlockSpec((cw.shape[0], LANES), lambda p, s: (0, col0 + s)),
                               pl.BlockSpec((1, LANES), lambda p, s: (0, col0 + s))]
    in_specs = [pl.BlockSpec((None, 2 * FFT_N1 * FFT_N2, LANES), lambda p, s: (p, 0, s)),
                slab(u_src[1]), slab(u_src[2]), slab(gate_src[1]), slab(gate_src[2])]
    args = [b3, u_src[0], u_src[0], gate_src[0], gate_src[0]]
    if u_conv:
        in_specs += wspecs(u_conv_args[0], u_conv_args[2])
        args += [u_conv_args[0], u_conv_args[1]]
    in_specs += wspecs(gate_conv_args[0], gate_conv_args[2])
    args += [gate_conv_args[0], gate_conv_args[1]]
    mspec = pl.BlockSpec((FFT_N1, FFT_N1), lambda p, s: (0, 0))
    in_specs += [mspec, mspec, pl.BlockSpec((1, LANES), lambda p, s: (0, s))]
    args += [ml, mr, skip.reshape(1, HY_W)]
    return pl.pallas_call(
        functools.partial(_dft_c_kernel, u_conv=u_conv),
        grid=(pairs, n_slabs),
        in_specs=in_specs,
        out_specs=pl.BlockSpec((2, SEQ, LANES), lambda p, s: (p, 0, s)),
        out_shape=jax.ShapeDtypeStruct((2 * pairs, SEQ, HY_W), out_dtype),
        scratch_shapes=[pltpu.VMEM((2 * FFT_N1 * PITCH, LANES), F32),
                        pltpu.VMEM((2, (FFT_N1 // 2) * PITCH, LANES), F32)],
        compiler_params=_cparams(("parallel", "parallel")),
        name="dft_c",
    )(*args)


def _mixer_hyena(proj3, conv_w, conv_b, skip, ah5, ssq, layer, consts):
    bsz = proj3.shape[0]
    pairs = bsz // 2
    n_slabs = HY_W // LANES
    cb = conv_b.reshape(1, 3 * HY_W)
    col = lambda which: (COL_CU + which * HY_W) // LANES
    proj_map = lambda which, odd: (lambda p, s: (2 * p + odd, 0, col(which) + s))
    plain_map = lambda odd: (lambda p, s: (2 * p + odd, 0, s))
    conv_args = lambda which: (conv_w, cb, which * n_slabs)
    a5_shape = (pairs, 2, FFT_N1, FFT_N2, HY_W)

    a = _dft_a(consts['sig_l'], consts['sig_r'], [(proj3, proj_map(0, 0)), (proj3, proj_map(0, 1))],
               pairs, n_slabs, conv_args(0))
    b = _dft_mid(consts['fwd'], consts['inv'], ah5, ssq, layer, 0, a.reshape(a5_shape))
    z1 = _dft_c(consts['out_l'], consts['out_r'], b.reshape(a.shape),
                (proj3, proj_map(0, 0), proj_map(0, 1)), (proj3, proj_map(1, 0), proj_map(1, 1)),
                conv_args(0), conv_args(1), skip[0], F32)
    a = _dft_a(consts['sig_l'], consts['sig_r'], [(z1, plain_map(0)), (z1, plain_map(1))], pairs, n_slabs)
    b = _dft_mid(consts['fwd'], consts['inv'], ah5, ssq, layer, 1, a.reshape(a5_shape))
    return _dft_c(consts['out_l'], consts['out_r'], b.reshape(a.shape),
                  (z1, plain_map(0), plain_map(1)), (proj3, proj_map(2, 0), proj_map(2, 1)),
                  None, conv_args(2), skip[1], BF16)


def _hyena_filter_stage(consts, hy_w1, hy_b1, hy_w2, hy_b2, hy_w3, hy_sin_freq, hy_decay):
    hf, hb, ssq = _hy_filters(hy_w1, hy_b1, hy_w2, hy_b2, hy_w3, hy_sin_freq, hy_decay)
    cw = HY_ORDER * HY_W
    fmap = lambda l, s: (l, 0, s)
    ha = _dft_a(consts['sig_l'], consts['fil_r'], [(hf, fmap), (hb, fmap)], DEPTH, cw // LANES)
    return ha.reshape(DEPTH, 2, FFT_N1, FFT_N2, cw), ssq


def _mlstm_chunk(q_ref, k_ref, v_ref, gc_ref, c_scr, m_scr, reverse, i_off, f_off, state_off):
    ch = ML_CHUNK
    ri = lax.broadcasted_iota(jnp.int32, (ch, ch), 0)
    ci = lax.broadcasted_iota(jnp.int32, (ch, ch), 1)
    tri = (ci >= ri) if reverse else (ci <= ri)
    tri_f = tri.astype(F32)
    gc = gc_ref[...]
    gr = gc.T
    b_col = jnp.dot(tri_f, _log_sigmoid(gc), precision=HIGHEST, preferred_element_type=F32)
    b_row = lax.dot_general(_log_sigmoid(gr[:4 * ML_HEADS, :]), tri_f, (((1,), (1,)), ((), ())), precision=HIGHEST,
                            preferred_element_type=F32)
    last = 0 if reverse else ch - 1
    lane = lax.broadcasted_iota(jnp.int32, (ch, ML_HEAD_DIM), 1)
    ones_col = jnp.where(lane == 0, 1.0, 0.0).astype(BF16)

    outs = []
    for h in range(ML_HEADS):
        sl = slice(h * ML_HEAD_DIM, (h + 1) * ML_HEAD_DIM)
        st = state_off + h
        q = (q_ref[:, sl] * (ML_HEAD_DIM ** -0.5)).astype(BF16)
        k = k_ref[:, sl]
        v_aug = jnp.concatenate([v_ref[:, sl].astype(BF16), ones_col], axis=1)
        bc = b_col[:, f_off + h:f_off + h + 1]
        lic = gc[:, i_off + h:i_off + h + 1]
        br = b_row[f_off + h:f_off + h + 1, :]
        lir = gr[i_off + h:i_off + h + 1, :]
        b_tot = bc[last:last + 1, :]
        d = jnp.where(tri, bc - br + lir, -jnp.inf)
        w_end = b_tot - bc + lic
        m_loc = jnp.max(w_end, axis=0, keepdims=True)
        e_end = jnp.exp(w_end - m_loc)
        m_prev = m_scr[st:st + 1, 0:1]
        c_prev = c_scr[st]
        m_inter = bc + m_prev
        m_t = jnp.maximum(m_inter, jnp.max(d, axis=-1, keepdims=True))
        e_inter = jnp.exp(m_inter - m_t)
        qk = lax.dot_general(q, k.astype(BF16), (((1,), (1,)), ((), ())), preferred_element_type=F32)
        s = qk * jnp.exp(d - m_t)
        nd = _bdot(s, v_aug) + e_inter * _bdot(q, c_prev)
        num = nd[:, :ML_HEAD_DIM]
        den = nd[:, ML_HEAD_DIM:ML_HEAD_DIM + 1]
        outs.append(num / jnp.maximum(jnp.abs(den), jnp.exp(-m_t)))
        m_new = jnp.maximum(b_tot + m_prev, m_loc)
        decay = jnp.exp(b_tot + m_prev - m_new)
        gain = jnp.exp(m_loc - m_new)
        dc = lax.dot_general((k * e_end).astype(BF16), v_aug, (((0,), (0,)), ((), ())),
                             preferred_element_type=F32)
        c_scr[st] = decay * c_prev + gain * dc
        m_scr[st:st + 1, :] = jnp.broadcast_to(m_new, (1, LANES))
    return outs


ML_BATCH_ROWS = 1


def _mlstm_kernel(*refs):
    n_in = 8 * ML_BATCH_ROWS
    hf_ref, hb_ref, c_scr, m_scr = refs[n_in:]

    @pl.when(pl.program_id(1) == 0)
    def _():
        c_scr[...] = jnp.zeros_like(c_scr)
        m_scr[...] = jnp.zeros_like(m_scr)

    for r in range(ML_BATCH_ROWS):
        qf_ref, kf_ref, vf_ref, gf_ref, qb_ref, kb_ref, vb_ref, gb_ref = refs[8 * r:8 * r + 8]
        st = 2 * ML_HEADS * r
        outs_f = _mlstm_chunk(qf_ref, kf_ref, vf_ref, gf_ref, c_scr, m_scr, False, 0, ML_HEADS, st)
        outs_b = _mlstm_chunk(qb_ref, kb_ref, vb_ref, gb_ref, c_scr, m_scr, True, 2 * ML_HEADS, 3 * ML_HEADS,
                              st + ML_HEADS)
        hf_ref[r] = jnp.concatenate(outs_f, axis=1)
        hb_ref[r] = jnp.concatenate(outs_b, axis=1)


def _mixer_mlstm(proj2, bsz):
    nc = SEQ // ML_CHUNK
    rows = ML_BATCH_ROWS

    def specs(r, chunk_of):
        def at(col, width):
            def index(i, j):
                return pl.multiple_of((i * rows + r) * SEQ + chunk_of(j) * ML_CHUNK, ML_CHUNK), col
            return pl.BlockSpec((pl.Element(ML_CHUNK), pl.Element(width)), index)
        return [at(COL_DQ, GROUP_W), at(COL_DK, GROUP_W), at(COL_DV, GROUP_W), at(COL_GATES, LANES)]

    fwd_of = lambda j: j
    bwd_of = lambda j: nc - 1 - j
    in_specs = []
    for r in range(rows):
        in_specs += specs(r, fwd_of) + specs(r, bwd_of)
    out = lambda chunk_of: pl.BlockSpec((rows, ML_CHUNK, GROUP_W), lambda i, j: (i, chunk_of(j), 0))
    n_state = 2 * ML_HEADS * rows
    return pl.pallas_call(
        _mlstm_kernel,
        grid=(bsz // rows, nc),
        in_specs=in_specs,
        out_specs=[out(fwd_of), out(bwd_of)],
        out_shape=[jax.ShapeDtypeStruct((bsz, SEQ, GROUP_W), F32)] * 2,
        scratch_shapes=[pltpu.VMEM((n_state, ML_HEAD_DIM, 2 * ML_HEAD_DIM), F32), pltpu.VMEM((n_state, LANES), F32)],
        compiler_params=_cparams(("parallel", "arbitrary")),
        name="mlstm",
    )(*([proj2] * (8 * rows)))


def _outproj_kernel(lf_ref, lb_ref, ga_ref, yb_ref, yc_ref, mf_ref, mb_ref, o_ref, mg_ref, w_ref, x_ref, m_ref,
                    out_ref):
    ya = (jax.nn.gelu(ga_ref[...]) * (lf_ref[...] + lb_ref[...])).astype(BF16)
    normed = []
    for h in range(ML_HEADS):
        sl = slice(h * ML_HEAD_DIM, (h + 1) * ML_HEAD_DIM)
        hh = mf_ref[:, sl] + mb_ref[:, sl]
        ms = jnp.mean(hh * hh, axis=-1, keepdims=True)
        normed.append(hh * lax.rsqrt(ms + EPS) * mg_ref[:, sl])
    yd = (_sigmoid(o_ref[...]) * jnp.concatenate(normed, axis=1)).astype(BF16)
    acc = jnp.dot(ya, w_ref[0:GROUP_W, :], preferred_element_type=F32)
    acc += jnp.dot(yb_ref[...], w_ref[GROUP_W:2 * GROUP_W, :], preferred_element_type=F32)
    acc += jnp.dot(yc_ref[...], w_ref[2 * GROUP_W:3 * GROUP_W, :], preferred_element_type=F32)
    acc += jnp.dot(yd, w_ref[3 * GROUP_W:, :], preferred_element_type=F32)
    out_ref[...] = x_ref[...] + m_ref[2:3, :] * acc


def _outproj(lru_f, lru_b, proj2, y_b, y_c, ml_f, ml_b, ml_g, w_all, layer, x2, mod_l):
    n = x2.shape[0]
    tm = 512
    per_b = SEQ // tm
    grp = pl.BlockSpec((tm, GROUP_W), lambda i: (i, 0))
    return pl.pallas_call(
        _outproj_kernel,
        grid=(n // tm,),
        in_specs=[grp, grp, pl.BlockSpec((tm, GROUP_W), lambda i: (i, COL_AG // GROUP_W)), grp, grp, grp, grp,
                  pl.BlockSpec((pl.Element(tm), pl.Element(GROUP_W)), lambda i: (pl.multiple_of(i * tm, tm), COL_DO)),
                  pl.BlockSpec((1, GROUP_W), lambda i: (0, 0)),
                  pl.BlockSpec((None, D_MODEL, D_MODEL), lambda i: (layer, 0, 0)),
                  pl.BlockSpec((tm, D_MODEL), lambda i: (i, 0)),
                  pl.BlockSpec((None, 6, D_MODEL), lambda i: (i // per_b, 0, 0))],
        out_specs=pl.BlockSpec((tm, D_MODEL), lambda i: (i, 0)),
        out_shape=jax.ShapeDtypeStruct((n, D_MODEL), F32),
        compiler_params=_cparams(("parallel",)),
        name="out_proj",
    )(lru_f, lru_b, proj2, y_b, y_c, ml_f, ml_b, proj2, ml_g.reshape(1, GROUP_W), w_all, x2, mod_l)


def _ffn_kernel(x_ref, m_ref, g_ref, w1_ref, w3_ref, w2_ref, fg_ref, o_ref, h_scr, *, final):
    j = pl.program_id(1)

    @pl.when(j == 0)
    def _():
        h_scr[...] = _rms_mod(x_ref[...], g_ref[...], m_ref[4:5, :], m_ref[3:4, :]).astype(BF16)
        o_ref[...] = jnp.zeros_like(o_ref)

    h = h_scr[...]
    a = jnp.dot(h, w1_ref[...], preferred_element_type=F32)
    b = jnp.dot(h, w3_ref[...], preferred_element_type=F32)
    act = (a * _sigmoid(a)) * b
    o_ref[...] += jnp.dot(act.astype(BF16), w2_ref[...], preferred_element_type=F32)

    @pl.when(j == pl.num_programs(1) - 1)
    def _():
        y = x_ref[...] + m_ref[5:6, :] * o_ref[...]
        if final:
            ms = jnp.mean(y * y, axis=-1, keepdims=True)
            y = y * lax.rsqrt(ms + EPS) * fg_ref[...]
        o_ref[...] = y


def _ffn(x2, mod_l, g, w1_all, w3_all, w2_all, layer, final_g, final):
    n = x2.shape[0]
    tm, tf = 512, 512
    per_b = SEQ // tm
    row = lambda: pl.BlockSpec((1, D_MODEL), lambda i, j: (0, 0))
    return pl.pallas_call(
        functools.partial(_ffn_kernel, final=final),
        grid=(n // tm, D_FF // tf),
        in_specs=[pl.BlockSpec((tm, D_MODEL), lambda i, j: (i, 0)),
                  pl.BlockSpec((None, 6, D_MODEL), lambda i, j: (i // per_b, 0, 0)),
                  row(),
                  pl.BlockSpec((None, D_MODEL, tf), lambda i, j: (layer, 0, j)),
                  pl.BlockSpec((None, D_MODEL, tf), lambda i, j: (layer, 0, j)),
                  pl.BlockSpec((None, tf, D_MODEL), lambda i, j: (layer, j, 0)),
                  row()],
        out_specs=pl.BlockSpec((tm, D_MODEL), lambda i, j: (i, 0)),
        out_shape=jax.ShapeDtypeStruct((n, D_MODEL), F32),
        scratch_shapes=[pltpu.VMEM((tm, D_MODEL), BF16)],
        compiler_params=_cparams(("parallel", "arbitrary")),
        name="ffn",
    )(x2, mod_l, g.reshape(1, D_MODEL), w1_all, w3_all, w2_all, final_g.reshape(1, D_MODEL))


def kernel(x, c, w_in, b_in, w_out, norm_mix_g, norm_ffn_g, ada_w, ada_b, lru_conv_w, lru_conv_b, lru_wa, lru_ba, lru_wx, lru_bx, lru_lambda, att_q_norm_g, att_k_norm_g, hy_conv_w, hy_conv_b, hy_w1, hy_b1, hy_w2, hy_b2, hy_w3, hy_sin_freq, hy_decay, hy_skip, ml_norm_g, ffn_w1, ffn_w3, ffn_w2, final_g):
    bsz = x.shape[0]
    assert x.shape == (bsz, SEQ, D_MODEL) and bsz % 2 == 0
    n = bsz * SEQ
    mod = _ada_all(c, ada_w, ada_b)

    consts = {k: v.astype(BF16) for k, v in _dft_constants().items()}
    ah5, ssq = _hyena_filter_stage(consts, hy_w1, hy_b1, hy_w2, hy_b2, hy_w3, hy_sin_freq, hy_decay)

    pad = D_IN_PAD - D_IN
    w_in_b = jnp.pad(w_in.astype(BF16), ((0, 0), (0, 0), (0, pad)))
    b_in_p = jnp.pad(b_in, ((0, 0), (0, pad))).reshape(DEPTH, 1, D_IN_PAD)
    w_out_b = w_out.astype(BF16)
    w1_b, w3_b, w2_b = ffn_w1.astype(BF16), ffn_w3.astype(BF16), ffn_w2.astype(BF16)

    x2 = x.reshape(n, D_MODEL)
    for l in range(DEPTH):
        proj2 = _inproj(x2, mod[l], norm_mix_g[l], w_in_b, b_in_p, l)
        proj3 = proj2.reshape(bsz, SEQ, D_IN_PAD)
        lru_f, lru_b = _mixer_rglru(proj3, lru_conv_w[l], lru_conv_b[l], lru_wa[l], lru_ba[l], lru_wx[l], lru_bx[l],
                                    lru_lambda[l])
        y_b = _mixer_attention(proj3, att_q_norm_g[l], att_k_norm_g[l])
        y_c = _mixer_hyena(proj3, hy_conv_w[l], hy_conv_b[l], hy_skip[l], ah5, ssq, l, consts)
        ml_f, ml_b = _mixer_mlstm(proj2, bsz)
        flat = lambda a: a.reshape(n, GROUP_W)
        x2 = _outproj(flat(lru_f), flat(lru_b), proj2, flat(y_b), flat(y_c), flat(ml_f), flat(ml_b), ml_norm_g[l],
                      w_out_b, l, x2, mod[l])
        x2 = _ffn(x2, mod[l], norm_ffn_g[l], w1_b, w3_b, w2_b, l, final_g, final=(l == DEPTH - 1))
    return x2.reshape(bsz, SEQ, D_MODEL)
```

```python
import functools
import math

import numpy as np
import jax
import jax.numpy as jnp
from jax import lax
from jax.experimental import pallas as pl
from jax.experimental.pallas import tpu as pltpu

F32 = jnp.float32
BF16 = jnp.bfloat16
HIGHEST = lax.Precision.HIGHEST

D_MODEL = 2048
SEQ = 4096
DEPTH = 2
GROUP_W = 512
LRU_BLOCKS = 8
LRU_C = 8.0
ATT_HEADS = 8
ATT_KV_HEADS = 2
ATT_GROUP = ATT_HEADS // ATT_KV_HEADS
ATT_HEAD_DIM = 64
ROPE_AXIS = ATT_HEAD_DIM // 2
ROPE_THETA = 10000.0
GRID_W = 64
ATT_HEAD_GROUP = 2
ATT_V_ROWS = 80
LOG2_E = math.log2(math.e)
HY_W = GROUP_W
HY_ORDER = 2
HY_BANDS = 8
HY_EMB = 2 * HY_BANDS + 1
HY_EMB_PAD = 32
HY_FFN = 64
ML_HEADS = 4
ML_HEAD_DIM = 128
ML_CHUNK = 128
D_FF = 5632
EPS = 1e-6
IN_SIZES = (512, 512, 512, 128, 128, 1536, 512, 512, 512, 512, 16)
D_IN = sum(IN_SIZES)
D_IN_PAD = 5632

COL_AX, COL_AG, COL_BQ, COL_BK, COL_BV, COL_CU = 0, 512, 1024, 1536, 1664, 1792
COL_DQ, COL_DK, COL_DV, COL_DO, COL_GATES = 3328, 3840, 4352, 4864, 5376
LANES = 128

FFT_N = 2 * SEQ
FFT_N1 = 64
FFT_N2 = 128
PITCH = 136

VMEM_LIMIT = 56 * 1024 * 1024


def _cparams(sem, vmem=VMEM_LIMIT):
    return pltpu.CompilerParams(dimension_semantics=sem, vmem_limit_bytes=vmem)


def _bdot(a, b):
    return jnp.dot(a.astype(BF16), b.astype(BF16), preferred_element_type=F32)


def _sigmoid(x):
    return 0.5 * jnp.tanh(0.5 * x) + 0.5


def _log_sigmoid(x):
    return jnp.minimum(x, 0.0) - jnp.log1p(jnp.exp(-jnp.abs(x)))


def _softplus(x):
    return jnp.maximum(x, 0.0) + jnp.log1p(jnp.exp(-jnp.abs(x)))


def _ada_kernel(c_ref, w_ref, b_ref, o_ref):
    c = c_ref[...]
    o_ref[...] = _bdot(c * _sigmoid(c), w_ref[...]) + b_ref[...]


def _ada_all(c, ada_w, ada_b):
    bsz = c.shape[0]
    rows = 8
    cp = jnp.zeros((rows, D_MODEL), F32).at[:bsz].set(c)
    tn = 2048
    out = pl.pallas_call(
        _ada_kernel,
        grid=(DEPTH, 6 * D_MODEL // tn),
        in_specs=[pl.BlockSpec((rows, D_MODEL), lambda l, j: (0, 0)),
                  pl.BlockSpec((None, D_MODEL, tn), lambda l, j: (l, 0, j)),
                  pl.BlockSpec((None, 1, tn), lambda l, j: (l, 0, j))],
        out_specs=pl.BlockSpec((None, rows, tn), lambda l, j: (l, 0, j)),
        out_shape=jax.ShapeDtypeStruct((DEPTH, rows, 6 * D_MODEL), F32),
        compiler_params=_cparams(("parallel", "parallel")),
        name="ada_mod",
    )(cp, ada_w, ada_b.reshape(DEPTH, 1, 6 * D_MODEL))
    return out[:, :bsz].reshape(DEPTH, bsz, 6, D_MODEL)


def _rms_mod(x, g, scale, shift):
    ms = jnp.mean(x * x, axis=-1, keepdims=True)
    return (x * lax.rsqrt(ms + EPS)) * (g * (1.0 + scale)) + shift


def _inproj_kernel(x_ref, m_ref, g_ref, w_ref, b_ref, o_ref, h_scr):
    @pl.when(pl.program_id(1) == 0)
    def _():
        h_scr[...] = _rms_mod(x_ref[...], g_ref[...], m_ref[1:2, :], m_ref[0:1, :]).astype(BF16)

    o_ref[...] = jnp.dot(h_scr[...], w_ref[...], preferred_element_type=F32) + b_ref[...]


def _inproj(x2, mod_l, g, w_all, b_all, layer):
    n = x2.shape[0]
    tm, tn = 1024, 1408
    per_b = SEQ // tm
    return pl.pallas_call(
        _inproj_kernel,
        grid=(n // tm, D_IN_PAD // tn),
        in_specs=[pl.BlockSpec((tm, D_MODEL), lambda i, j: (i, 0)),
                  pl.BlockSpec((None, 6, D_MODEL), lambda i, j: (i // per_b, 0, 0)),
                  pl.BlockSpec((1, D_MODEL), lambda i, j: (0, 0)),
                  pl.BlockSpec((None, D_MODEL, tn), lambda i, j: (layer, 0, j)),
                  pl.BlockSpec((None, 1, tn), lambda i, j: (layer, 0, j))],
        out_specs=pl.BlockSpec((tm, tn), lambda i, j: (i, j)),
        out_shape=jax.ShapeDtypeStruct((n, D_IN_PAD), F32),
        scratch_shapes=[pltpu.VMEM((tm, D_MODEL), BF16)],
        compiler_params=_cparams(("parallel", "arbitrary")),
        name="in_proj",
    )(x2, mod_l, g.reshape(1, D_MODEL), w_all, b_all)


def _shifted(ext, off, rows):
    total = ext.shape[0]
    if off == 0:
        return ext[8:8 + rows]
    return pltpu.roll(ext, (-off) % total, axis=0)[8:8 + rows]


def _dwconv_ext(ext, w_ref, b_ref, left, rows):
    out = b_ref[...]
    for j in range(w_ref.shape[0]):
        out = out + _shifted(ext, j - left, rows) * w_ref[j:j + 1, :]
    return out


def _dwconv_tile(x_ref, p_ref, n_ref, w_ref, b_ref, ext_ref, tile, n_tiles, left):
    rows = x_ref.shape[0]
    prev = jnp.where(tile > 0, p_ref[...], 0.0)
    nxt = jnp.where(tile < n_tiles - 1, n_ref[...], 0.0)
    outs = []
    for s in range(ext_ref.shape[0]):
        ln = slice(s * LANES, (s + 1) * LANES)
        ext_ref[s, 0:8, :] = prev[:, ln]
        ext_ref[s, 8:8 + rows, :] = x_ref[:, ln]
        ext_ref[s, 8 + rows:16 + rows, :] = nxt[:, ln]
        out = b_ref[:, ln]
        for j in range(w_ref.shape[0]):
            start = 8 + j - left
            out = out + ext_ref[s, start:start + rows, :] * w_ref[j:j + 1, ln]
        outs.append(out)
    return jnp.concatenate(outs, axis=1)


def _dwconv_rows(ref, lo, rows, w_ref, b_ref, left):
    taps = w_ref.shape[0]
    if lo - left >= 0 and lo + rows + (taps - 1 - left) <= SEQ:
        out = b_ref[...]
        for j in range(taps):
            start = lo + j - left
            out = out + ref[start:start + rows, :] * w_ref[j:j + 1, :]
        return out
    zeros = jnp.zeros((8, ref.shape[1]), F32)
    prev = ref[lo - 8:lo, :] if lo > 0 else zeros
    nxt = ref[lo + rows:lo + rows + 8, :] if lo + rows < SEQ else zeros
    ext = jnp.concatenate([prev, ref[lo:lo + rows, :], nxt], axis=0)
    return _dwconv_ext(ext, w_ref, b_ref, left, rows)


def _halo_specs(tile_rows, width, col_block, tile_of):
    r8 = tile_rows // 8
    last8 = SEQ // 8 - 1

    def main(b, t):
        return (b, tile_of(t), col_block)

    def prev(b, t):
        return (b, jnp.maximum(tile_of(t) * r8 - 1, 0), col_block)

    def nxt(b, t):
        return (b, jnp.minimum((tile_of(t) + 1) * r8, last8), col_block)

    return [pl.BlockSpec((None, tile_rows, width), main),
            pl.BlockSpec((None, 8, width), prev),
            pl.BlockSpec((None, 8, width), nxt)]


def _lru_gates(xc, wg, bg, lam):
    gates = _bdot(xc, wg) + bg
    r = _sigmoid(gates[:, :GROUP_W])
    i = _sigmoid(gates[:, GROUP_W:])
    log_a = (-LRU_C * _softplus(-lam)) * r
    a = jnp.exp(log_a)
    th = jnp.tanh(log_a)
    u = jnp.sqrt(-2.0 * th / (1.0 - th)) * (i * xc)
    return a, u


def _scan8(a, u, ridx, reverse):
    for k in (1, 2, 4):
        if reverse:
            keep = ridx < 8 - k
            sh = 8 - k
        else:
            keep = ridx >= k
            sh = k
        a_sh = jnp.where(keep, pltpu.roll(a, sh, axis=0), 1.0)
        u_sh = jnp.where(keep, pltpu.roll(u, sh, axis=0), 0.0)
        u = a * u_sh + u
        a = a * a_sh
    return a, u


def _lru_kernel(xf_ref, pf_ref, nf_ref, xb_ref, pb_ref, nb_ref, cw_ref, cb_ref, wg_ref, bg_ref, lam_ref,
                hf_ref, hb_ref, af_scr, uf_scr, ab_scr, ub_scr, cf_scr, cb_scr, ext_scr, c_scr, *, n_tiles):
    t = pl.program_id(1)
    n_chunks = af_scr.shape[0]
    xc = _dwconv_tile(xf_ref, pf_ref, nf_ref, cw_ref, cb_ref, ext_scr.at[0], t, n_tiles, 2)
    a, u = _lru_gates(xc, wg_ref[0], bg_ref[0], lam_ref[0])
    af_scr[...] = a.reshape(n_chunks, 8, GROUP_W)
    uf_scr[...] = u.reshape(n_chunks, 8, GROUP_W)
    xc = _dwconv_tile(xb_ref, pb_ref, nb_ref, cw_ref, cb_ref, ext_scr.at[1], n_tiles - 1 - t, n_tiles, 2)
    a, u = _lru_gates(xc, wg_ref[1], bg_ref[1], lam_ref[1])
    ab_scr[...] = a.reshape(n_chunks, 8, GROUP_W)
    ub_scr[...] = u.reshape(n_chunks, 8, GROUP_W)

    @pl.when(t == 0)
    def _():
        c_scr[...] = jnp.zeros_like(c_scr)

    ridx = lax.broadcasted_iota(jnp.int32, (8, GROUP_W), 0)

    def local(c, carry):
        a, u = _scan8(af_scr[c], uf_scr[c], ridx, False)
        af_scr[c] = a
        uf_scr[c] = u
        a, u = _scan8(ab_scr[c], ub_scr[c], ridx, True)
        ab_scr[c] = a
        ub_scr[c] = u
        return carry

    lax.fori_loop(0, n_chunks, local, 0, unroll=4)

    def chain(c, carry):
        cf, cb = carry
        cf_scr[c] = jnp.broadcast_to(cf, (8, GROUP_W))
        cf = af_scr[c][7:8, :] * cf + uf_scr[c][7:8, :]
        cr = n_chunks - 1 - c
        cb_scr[cr] = jnp.broadcast_to(cb, (8, GROUP_W))
        cb = ab_scr[cr][0:1, :] * cb + ub_scr[cr][0:1, :]
        return cf, cb

    cf, cb = lax.fori_loop(0, n_chunks, chain, (c_scr[0:1, :], c_scr[1:2, :]), unroll=4)
    c_scr[0:1, :] = cf
    c_scr[1:2, :] = cb

    def apply(c, carry):
        r0 = pl.multiple_of(c * 8, 8)
        hf_ref[pl.ds(r0, 8), :] = uf_scr[c] + af_scr[c] * cf_scr[c]
        hb_ref[pl.ds(r0, 8), :] = ub_scr[c] + ab_scr[c] * cb_scr[c]
        return carry

    lax.fori_loop(0, n_chunks, apply, 0, unroll=4)


def _block_diag(w):
    nb, k, j = w.shape
    eye = jnp.eye(nb, dtype=w.dtype)
    return jnp.einsum('nkj,nm->nkmj', w, eye).reshape(nb * k, nb * j)


def _mixer_rglru(proj3, conv_w, conv_b, wa, ba, wx, bx, lam):
    bsz = proj3.shape[0]
    ts = 512
    n_tiles = SEQ // ts
    wg = jnp.stack([jnp.concatenate([_block_diag(wa[d]), _block_diag(wx[d])], axis=1) for d in range(2)]).astype(BF16)
    bg = jnp.stack([jnp.concatenate([ba[d], bx[d]]).reshape(1, 2 * GROUP_W) for d in range(2)])
    small = lambda *shape: pl.BlockSpec(shape, lambda b, t: (0,) * len(shape))
    fwd_of = lambda t: t
    bwd_of = lambda t: n_tiles - 1 - t
    out = lambda tile_of: pl.BlockSpec((None, ts, GROUP_W), lambda b, t: (b, tile_of(t), 0))
    tile_scr = pltpu.VMEM((ts // 8, 8, GROUP_W), F32)
    return pl.pallas_call(
        functools.partial(_lru_kernel, n_tiles=n_tiles),
        grid=(bsz, n_tiles),
        in_specs=_halo_specs(ts, GROUP_W, COL_AX // GROUP_W, fwd_of) + _halo_specs(ts, GROUP_W, COL_AX // GROUP_W, bwd_of)
        + [small(4, GROUP_W), small(1, GROUP_W), small(2, GROUP_W, 2 * GROUP_W), small(2, 1, 2 * GROUP_W),
           small(2, 1, GROUP_W)],
        out_specs=[out(fwd_of), out(bwd_of)],
        out_shape=[jax.ShapeDtypeStruct((bsz, SEQ, GROUP_W), F32)] * 2,
        scratch_shapes=[tile_scr] * 6 + [pltpu.VMEM((2, GROUP_W // LANES, ts + 16, LANES), F32),
                                         pltpu.VMEM((8, GROUP_W), F32)],
        compiler_params=_cparams(("parallel", "arbitrary")),
        name="rglru",
    )(proj3, proj3, proj3, proj3, proj3, proj3, conv_w, conv_b.reshape(1, GROUP_W), wg, bg,
      lam.reshape(2, 1, GROUP_W))


def _split_dot(x, m_ref):
    hi = x.astype(BF16)
    lo = (x - hi.astype(F32)).astype(BF16)
    m = m_ref[...]
    return (jnp.dot(hi, m, preferred_element_type=F32) + jnp.dot(lo, m, preferred_element_type=F32))


def _norm_rope(x, gain, m_ref, cos, sins):
    width = x.shape[1]
    ms = _split_dot(x * x, m_ref)
    xn = x * lax.rsqrt(ms + EPS) * gain
    lane = lax.broadcasted_iota(jnp.int32, xn.shape, 1)
    first = (lane % ROPE_AXIS) < (ROPE_AXIS // 2)
    half = ROPE_AXIS // 2
    partner = jnp.where(first, pltpu.roll(xn, width - half, axis=1), pltpu.roll(xn, half, axis=1))
    return xn * cos + partner * sins


def _attn_prep_kernel(q_ref, kv_ref, gq_ref, gk_ref, mq_ref, mk_ref, cos_ref, sin_ref,
                      qo_ref, kt_ref, vt_ref):
    cos = cos_ref[...]
    sins = sin_ref[...]
    cos_q = jnp.concatenate([cos] * (GROUP_W // LANES), axis=1)
    sin_q = jnp.concatenate([sins] * (GROUP_W // LANES), axis=1)
    q = _norm_rope(q_ref[...], gq_ref[...], mq_ref, cos_q, sin_q)
    qo_ref[...] = (q * (ATT_HEAD_DIM ** -0.5 * LOG2_E)).astype(BF16)
    kv = kv_ref[...]
    kt_ref[...] = _norm_rope(kv[:, :LANES], gk_ref[...], mk_ref, cos, sins).T.astype(BF16)
    v_t = kv[:, LANES:].T
    row = lax.broadcasted_iota(jnp.int32, (ATT_V_ROWS - ATT_HEAD_DIM, v_t.shape[1]), 0)
    ones_rows = jnp.where(row == 0, 1.0, 0.0)
    for g in range(ATT_KV_HEADS):
        vt_ref[g] = jnp.concatenate([v_t[g * ATT_HEAD_DIM:(g + 1) * ATT_HEAD_DIM, :], ones_rows], axis=0).astype(BF16)


def _rope_tables():
    rows = SEQ // GRID_W
    row = jnp.repeat(jnp.arange(rows, dtype=F32), GRID_W)
    col = jnp.tile(jnp.arange(GRID_W, dtype=F32), rows)
    inv = ROPE_THETA ** (-jnp.arange(0, ROPE_AXIS, 2, dtype=F32) / ROPE_AXIS)
    ar = row[:, None] * inv
    ac = col[:, None] * inv
    ang = jnp.concatenate([ar, ar, ac, ac], axis=1)
    sign = jnp.concatenate([-jnp.ones((ROPE_AXIS // 2,), F32), jnp.ones((ROPE_AXIS // 2,), F32)] * 2)
    cos = jnp.tile(jnp.cos(ang), (1, 2))
    sins = jnp.tile(jnp.sin(ang) * sign, (1, 2))
    return cos, sins


def _head_mean_matrix(width):
    idx = np.arange(width) // ATT_HEAD_DIM
    return jnp.asarray((idx[:, None] == idx[None, :]).astype(np.float32) / ATT_HEAD_DIM, dtype=BF16)


def _attn_kernel(q_ref, kt_ref, vt_ref, o_ref):
    def scores(h):
        g = h // ATT_GROUP
        q = q_ref[:, h * ATT_HEAD_DIM:(h + 1) * ATT_HEAD_DIM]
        return jnp.dot(q, kt_ref[g * ATT_HEAD_DIM:(g + 1) * ATT_HEAD_DIM, :], preferred_element_type=F32)

    def probs(s):
        return jnp.exp2(s - jnp.max(s, axis=-1, keepdims=True)).astype(BF16)

    def values(h, p):
        o_aug = lax.dot_general(vt_ref[h // ATT_GROUP], p, (((1,), (1,)), ((), ())), preferred_element_type=F32)
        return (o_aug[:ATT_HEAD_DIM] / o_aug[ATT_HEAD_DIM:ATT_HEAD_DIM + 1]).T

    outs = []
    for h0 in range(0, ATT_HEADS, ATT_HEAD_GROUP):
        heads = range(h0, h0 + ATT_HEAD_GROUP)
        ps = [probs(s) for s in [scores(h) for h in heads]]
        outs += [values(h, p) for h, p in zip(heads, ps)]
    o_ref[...] = jnp.concatenate(outs, axis=1).astype(o_ref.dtype)


def _mixer_attention(proj3, q_g, k_g):
    bsz = proj3.shape[0]
    ts = 512
    cos, sins = _rope_tables()
    gq = jnp.tile(q_g, ATT_HEADS).reshape(1, GROUP_W)
    gk = jnp.tile(k_g, ATT_KV_HEADS).reshape(1, LANES)
    const = lambda shape: pl.BlockSpec(shape, lambda b, t: (0, 0))
    qp, kt, vt = pl.pallas_call(
        _attn_prep_kernel,
        grid=(bsz, SEQ // ts),
        in_specs=[pl.BlockSpec((None, ts, GROUP_W), lambda b, t: (b, t, COL_BQ // GROUP_W)),
                  pl.BlockSpec((None, ts, 2 * LANES), lambda b, t: (b, t, COL_BK // (2 * LANES))),
                  const((1, GROUP_W)), const((1, LANES)), const((GROUP_W, GROUP_W)), const((LANES, LANES)),
                  pl.BlockSpec((ts, LANES), lambda b, t: (t, 0)),
                  pl.BlockSpec((ts, LANES), lambda b, t: (t, 0))],
        out_specs=[pl.BlockSpec((None, ts, GROUP_W), lambda b, t: (b, t, 0)),
                   pl.BlockSpec((None, LANES, ts), lambda b, t: (b, 0, t)),
                   pl.BlockSpec((None, ATT_KV_HEADS, ATT_V_ROWS, ts), lambda b, t: (b, 0, 0, t))],
        out_shape=[jax.ShapeDtypeStruct((bsz, SEQ, GROUP_W), BF16),
                   jax.ShapeDtypeStruct((bsz, LANES, SEQ), BF16),
                   jax.ShapeDtypeStruct((bsz, ATT_KV_HEADS, ATT_V_ROWS, SEQ), BF16)],
        compiler_params=_cparams(("parallel", "parallel")),
        name="attn_prep",
    )(proj3, proj3, gq, gk, _head_mean_matrix(GROUP_W), _head_mean_matrix(LANES), cos, sins)

    tq = 512
    return pl.pallas_call(
        _attn_kernel,
        grid=(bsz, SEQ // tq),
        in_specs=[pl.BlockSpec((None, tq, GROUP_W), lambda b, t: (b, t, 0)),
                  pl.BlockSpec((None, LANES, SEQ), lambda b, t: (b, 0, 0)),
                  pl.BlockSpec((None, ATT_KV_HEADS, ATT_V_ROWS, SEQ), lambda b, t: (b, 0, 0, 0))],
        out_specs=pl.BlockSpec((None, tq, GROUP_W), lambda b, t: (b, t, 0)),
        out_shape=jax.ShapeDtypeStruct((bsz, SEQ, GROUP_W), BF16),
        compiler_params=_cparams(("parallel", "parallel")),
        name="attention",
    )(qp, kt, vt)


def _hy_features():
    L = SEQ
    pos = jnp.arange(L, dtype=F32)
    t = pos / max(L - 1, 1)
    bands = jnp.linspace(1e-4, HY_BANDS - 1, HY_BANDS, dtype=F32)
    ang = (2.0 * math.pi * pos / L)[:, None] * bands
    feat = jnp.concatenate([t[:, None], jnp.cos(ang), -jnp.sin(ang)], axis=-1)
    feat = jnp.pad(feat, ((0, 0), (0, HY_EMB_PAD - HY_EMB)))
    rev_idx = np.concatenate([[0], np.arange(L - 1, 0, -1)])
    return feat, feat[rev_idx]


def _hy_mlp(feat, feat_t, w1t_ref, b1_ref, w2t_ref, b2_ref, w3_ref, sf_ref, dec_ref):
    sf = sf_ref[...]
    h = jnp.sin(sf * (jnp.dot(w1t_ref[...], feat_t, precision=HIGHEST, preferred_element_type=F32) + b1_ref[...]))
    h = jnp.sin(sf * (jnp.dot(w2t_ref[...], h, precision=HIGHEST, preferred_element_type=F32) + b2_ref[...]))
    out = jnp.dot(h.T, w3_ref[...], precision=HIGHEST, preferred_element_type=F32)
    return out * jnp.exp(-feat[:, 0:1] * jnp.abs(dec_ref[...]))


def _hy_filter_kernel(ff_ref, fft_ref, fr_ref, frt_ref, w1t_ref, b1_ref, w2t_ref, b2_ref, w3f_ref, w3b_ref, sf_ref,
                      decf_ref, decb_ref, of_ref, ob_ref, ssq_ref):
    i = pl.program_id(1)
    hf = _hy_mlp(ff_ref[...], fft_ref[...], w1t_ref, b1_ref, w2t_ref, b2_ref, w3f_ref, sf_ref, decf_ref)
    hb = _hy_mlp(fr_ref[...], frt_ref[...], w1t_ref, b1_ref, w2t_ref, b2_ref, w3b_ref, sf_ref, decb_ref)

    @pl.when(i == 0)
    def _():
        ssq_ref[...] = jnp.zeros_like(ssq_ref)

    ssq_ref[0:1, :] += jnp.sum(hf * hf + hb * hb, axis=0, keepdims=True)
    of_ref[...] = hf
    row = lax.broadcasted_iota(jnp.int32, hb.shape, 0)
    ob_ref[...] = jnp.where(jnp.logical_and(i == 0, row == 0), 0.0, hb)


def _hy_filters(hy_w1, hy_b1, hy_w2, hy_b2, hy_w3, hy_sin_freq, hy_decay):
    feat, feat_rev = _hy_features()
    tr = 1024
    cw = HY_ORDER * HY_W
    w1t = jnp.transpose(jnp.pad(hy_w1, ((0, 0), (0, HY_EMB_PAD - HY_EMB), (0, 0))), (0, 2, 1))
    w2t = jnp.transpose(hy_w2, (0, 2, 1))
    w3 = hy_w3.reshape(DEPTH, HY_FFN, HY_ORDER, 2, HY_W)
    dec = hy_decay.reshape(DEPTH, HY_ORDER, 2, HY_W)
    w3f = w3[:, :, :, 0].reshape(DEPTH, HY_FFN, cw)
    w3b = w3[:, :, :, 1].reshape(DEPTH, HY_FFN, cw)
    decf = dec[:, :, 0].reshape(DEPTH, 1, cw)
    decb = dec[:, :, 1].reshape(DEPTH, 1, cw)
    col = lambda a: a.reshape(DEPTH, HY_FFN, 1)
    row_spec = pl.BlockSpec((tr, HY_EMB_PAD), lambda l, i: (i, 0))
    rowt_spec = pl.BlockSpec((HY_EMB_PAD, tr), lambda l, i: (0, i))
    per_layer = lambda a, b: pl.BlockSpec((None, a, b), lambda l, i: (l, 0, 0))
    out_spec = pl.BlockSpec((None, tr, cw), lambda l, i: (l, i, 0))
    return pl.pallas_call(
        _hy_filter_kernel,
        grid=(DEPTH, SEQ // tr),
        in_specs=[row_spec, rowt_spec, row_spec, rowt_spec, per_layer(HY_FFN, HY_EMB_PAD), per_layer(HY_FFN, 1),
                  per_layer(HY_FFN, HY_FFN), per_layer(HY_FFN, 1), per_layer(HY_FFN, cw), per_layer(HY_FFN, cw),
                  per_layer(HY_FFN, 1), per_layer(1, cw), per_layer(1, cw)],
        out_specs=[out_spec, out_spec, per_layer(8, cw)],
        out_shape=[jax.ShapeDtypeStruct((DEPTH, SEQ, cw), F32)] * 2 + [jax.ShapeDtypeStruct((DEPTH, 8, cw), F32)],
        compiler_params=_cparams(("arbitrary", "arbitrary")),
        name="hyena_filter",
    )(feat, feat.T, feat_rev, feat_rev.T, w1t, col(hy_b1), w2t, col(hy_b2), w3f, w3b, col(hy_sin_freq), decf, decb)


def _dft_constants():
    n1 = np.arange(FFT_N1)
    n2 = np.arange(FFT_N2)
    f1 = np.exp(-2j * np.pi * np.outer(n1, n1) / FFT_N1)
    stack = lambda m: np.concatenate([m.real, m.imag], axis=0)
    half = FFT_N1 // 2
    sig_l = stack(f1[:, :half])
    sig_r = np.concatenate([-f1[:, :half].imag, f1[:, :half].real], axis=0)
    fil_r = stack(f1[:, half:])
    f2 = np.exp(-2j * np.pi * np.outer(n2, n2) / FFT_N2)
    tw = np.exp(-2j * np.pi * np.outer(n1, n2) / FFT_N)
    fwd = f2[None, :, :] * tw[:, None, :]
    inv = np.conj(np.transpose(fwd, (0, 2, 1))) / FFT_N
    block = lambda m: np.concatenate([np.concatenate([m.real, -m.imag], axis=2),
                                      np.concatenate([m.imag, m.real], axis=2)], axis=1)
    g1 = np.conj(f1[:half, :])
    out_l = stack(g1)
    out_r = np.concatenate([-g1.imag, g1.real], axis=0)
    as32 = lambda a: jnp.asarray(a.astype(np.float32))
    return dict(sig_l=as32(sig_l), sig_r=as32(sig_r), fil_r=as32(fil_r), fwd=as32(block(fwd)),
                inv=as32(block(inv)), out_l=as32(out_l), out_r=as32(out_r))


_N2_GROUP = 8


def _dft_a_kernel(*refs, conv):
    if conv:
        u0_ref, u1_ref, cw_ref, cb_ref, ml_ref, mr_ref, o_ref, xs, ysc = refs
    else:
        u0_ref, u1_ref, ml_ref, mr_ref, o_ref, xs, ysc = refs
    half = FFT_N1 // 2
    for r, u_ref in enumerate((u0_ref, u1_ref)):
        for n1 in range(half):
            lo = FFT_N2 * n1
            rows = _dwconv_rows(u_ref, lo, FFT_N2, cw_ref, cb_ref, 1) if conv else u_ref[lo:lo + FFT_N2, :]
            xs[r, PITCH * n1:PITCH * n1 + FFT_N2, :] = rows
    ml = ml_ref[...]
    mr = mr_ref[...]

    def body(g, carry):
        n2 = g * _N2_GROUP
        x0 = jnp.concatenate([xs[0, pl.ds(n2 + i, half, stride=PITCH), :] for i in range(_N2_GROUP)], axis=1)
        x1 = jnp.concatenate([xs[1, pl.ds(n2 + i, half, stride=PITCH), :] for i in range(_N2_GROUP)], axis=1)
        y = _bdot(ml, x0) + _bdot(mr, x1)
        for i in range(_N2_GROUP):
            ysc[pl.ds(n2 + i, 2 * FFT_N1, stride=PITCH), :] = y[:, LANES * i:LANES * (i + 1)]
        return carry

    lax.fori_loop(0, FFT_N2 // _N2_GROUP, body, 0)
    for row in range(2 * FFT_N1):
        o_ref[FFT_N2 * row:FFT_N2 * (row + 1), :] = ysc[PITCH * row:PITCH * row + FFT_N2, :].astype(o_ref.dtype)


def _dft_a(ml, mr, srcs, groups, n_slabs, conv_args=None):
    conv = conv_args is not None
    slab = lambda arr_map: pl.BlockSpec((None, SEQ, LANES), arr_map)
    mspec = pl.BlockSpec((2 * FFT_N1, FFT_N1 // 2), lambda g, s: (0, 0))
    in_specs = [slab(srcs[0][1]), slab(srcs[1][1])]
    args = [srcs[0][0], srcs[1][0]]
    if conv:
        cw, cb, col0 = conv_args
        in_specs += [pl.BlockSpec((cw.shape[0], LANES), lambda g, s: (0, col0 + s)),
                     pl.BlockSpec((1, LANES), lambda g, s: (0, col0 + s))]
        args += [cw, cb]
    rows = 2 * FFT_N1 * FFT_N2
    return pl.pallas_call(
        functools.partial(_dft_a_kernel, conv=conv),
        grid=(groups, n_slabs),
        in_specs=in_specs + [mspec, mspec],
        out_specs=pl.BlockSpec((None, rows, LANES), lambda g, s: (g, 0, s)),
        out_shape=jax.ShapeDtypeStruct((groups, rows, n_slabs * LANES), BF16),
        scratch_shapes=[pltpu.VMEM((2, (FFT_N1 // 2) * PITCH, LANES), F32),
                        pltpu.VMEM((2 * FFT_N1 * PITCH, LANES), F32)],
        compiler_params=_cparams(("parallel", "parallel")),
        name="dft_a",
    )(*args, ml, mr)


def _dft_mid_kernel(f_ref, g_ref, ah_ref, ssq_ref, a_ref, o_ref):
    scale = lax.rsqrt(ssq_ref[0:1, :] + EPS)
    for kk in range(f_ref.shape[0]):
        f = f_ref[kk]
        g = g_ref[kk]
        h = _bdot(f, jnp.concatenate([ah_ref[0, kk], ah_ref[1, kk]], axis=0)) * scale
        hr = h[:FFT_N2]
        hi = h[FFT_N2:]
        for p in range(a_ref.shape[0]):
            y = _bdot(f, jnp.concatenate([a_ref[p, 0, kk], a_ref[p, 1, kk]], axis=0))
            yr = y[:FFT_N2]
            yi = y[FFT_N2:]
            z = jnp.concatenate([yr * hr - yi * hi, yr * hi + yi * hr], axis=0)
            w = _bdot(g, z)
            o_ref[p, 0, kk] = w[:FFT_N2].astype(o_ref.dtype)
            o_ref[p, 1, kk] = w[FFT_N2:].astype(o_ref.dtype)


def _dft_mid(fwd, inv, ah5, ssq, layer, order, a5):
    pairs = a5.shape[0]
    kb = 8
    blk = pl.BlockSpec((pairs, 2, kb, FFT_N2, HY_W), lambda k: (0, 0, k, 0, 0))
    mat = pl.BlockSpec((kb, 2 * FFT_N2, 2 * FFT_N2), lambda k: (k, 0, 0))
    return pl.pallas_call(
        _dft_mid_kernel,
        grid=(FFT_N1 // kb,),
        in_specs=[mat, mat,
                  pl.BlockSpec((None, 2, kb, FFT_N2, HY_W), lambda k: (layer, 0, k, 0, order)),
                  pl.BlockSpec((None, 8, HY_W), lambda k: (layer, 0, order)),
                  blk],
        out_specs=blk,
        out_shape=jax.ShapeDtypeStruct(a5.shape, BF16),
        compiler_params=_cparams(("parallel",)),
        name="dft_mid",
    )(fwd, inv, ah5, ssq, a5)


def _dft_c_kernel(*refs, u_conv):
    if u_conv:
        (b_ref, u0_ref, u1_ref, g0_ref, g1_ref, ucw_ref, ucb_ref, gcw_ref, gcb_ref, ml_ref, mr_ref, skip_ref,
         o_ref, bs, ys) = refs
    else:
        b_ref, u0_ref, u1_ref, g0_ref, g1_ref, gcw_ref, gcb_ref, ml_ref, mr_ref, skip_ref, o_ref, bs, ys = refs
    half = FFT_N1 // 2
    ml = ml_ref[...]
    mr = mr_ref[...]
    im0 = FFT_N1 * PITCH
    for row in range(2 * FFT_N1):
        bs[PITCH * row:PITCH * row + FFT_N2, :] = b_ref[FFT_N2 * row:FFT_N2 * (row + 1), :].astype(F32)

    def body(g, carry):
        n2 = g * _N2_GROUP
        br = jnp.concatenate([bs[pl.ds(n2 + i, FFT_N1, stride=PITCH), :] for i in range(_N2_GROUP)], axis=1)
        bi = jnp.concatenate([bs[pl.ds(im0 + n2 + i, FFT_N1, stride=PITCH), :] for i in range(_N2_GROUP)], axis=1)
        y = _bdot(ml, br) + _bdot(mr, bi)
        for i in range(_N2_GROUP):
            ys[0, pl.ds(n2 + i, half, stride=PITCH), :] = y[:half, LANES * i:LANES * (i + 1)]
            ys[1, pl.ds(n2 + i, half, stride=PITCH), :] = y[half:, LANES * i:LANES * (i + 1)]
        return carry

    lax.fori_loop(0, FFT_N2 // _N2_GROUP, body, 0)
    skip = skip_ref[...]
    for r, (u_ref, g_ref) in enumerate(((u0_ref, g0_ref), (u1_ref, g1_ref))):
        for n1 in range(half):
            lo = FFT_N2 * n1
            u = _dwconv_rows(u_ref, lo, FFT_N2, ucw_ref, ucb_ref, 1) if u_conv else u_ref[lo:lo + FFT_N2, :]
            gate = _dwconv_rows(g_ref, lo, FFT_N2, gcw_ref, gcb_ref, 1)
            conv = ys[r, PITCH * n1:PITCH * n1 + FFT_N2, :]
            o_ref[r, lo:lo + FFT_N2, :] = (gate * (conv + u * skip)).astype(o_ref.dtype)


def _dft_c(ml, mr, b3, u_src, gate_src, u_conv_args, gate_conv_args, skip, out_dtype):
    pairs = b3.shape[0]
    n_slabs = HY_W // LANES
    u_conv = u_conv_args is not None
    slab = lambda m: pl.BlockSpec((None, SEQ, LANES), m)
    wspecs = lambda cw, col0: [pl.BlockSpec((cw.shape[0], LANES), lambda p, s: (0, col0 + s)),
                               pl.BlockSpec((1, LANES), lambda p, s: (0, col0 + s))]
    in_specs = [pl.BlockSpec((None, 2 * FFT_N1 * FFT_N2, LANES), lambda p, s: (p, 0, s)),
                slab(u_src[1]), slab(u_src[2]), slab(gate_src[1]), slab(gate_src[2])]
    args = [b3, u_src[0], u_src[0], gate_src[0], gate_src[0]]
    if u_conv:
        in_specs += wspecs(u_conv_args[0], u_conv_args[2])
        args += [u_conv_args[0], u_conv_args[1]]
    in_specs += wspecs(gate_conv_args[0], gate_conv_args[2])
    args += [gate_conv_args[0], gate_conv_args[1]]
    mspec = pl.BlockSpec((FFT_N1, FFT_N1), lambda p, s: (0, 0))
    in_specs += [mspec, mspec, pl.BlockSpec((1, LANES), lambda p, s: (0, s))]
    args += [ml, mr, skip.reshape(1, HY_W)]
    return pl.pallas_call(
        functools.partial(_dft_c_kernel, u_conv=u_conv),
        grid=(pairs, n_slabs),
        in_specs=in_specs,
        out_specs=pl.BlockSpec((2, SEQ, LANES), lambda p, s: (p, 0, s)),
        out_shape=jax.ShapeDtypeStruct((2 * pairs, SEQ, HY_W), out_dtype),
        scratch_shapes=[pltpu.VMEM((2 * FFT_N1 * PITCH, LANES), F32),
                        pltpu.VMEM((2, (FFT_N1 // 2) * PITCH, LANES), F32)],
        compiler_params=_cparams(("parallel", "parallel")),
        name="dft_c",
    )(*args)


def _mixer_hyena(proj3, conv_w, conv_b, skip, ah5, ssq, layer, consts):
    bsz = proj3.shape[0]
    pairs = bsz // 2
    n_slabs = HY_W // LANES
    cb = conv_b.reshape(1, 3 * HY_W)
    col = lambda which: (COL_CU + which * HY_W) // LANES
    proj_map = lambda which, odd: (lambda p, s: (2 * p + odd, 0, col(which) + s))
    plain_map = lambda odd: (lambda p, s: (2 * p + odd, 0, s))
    conv_args = lambda which: (conv_w, cb, which * n_slabs)
    a5_shape = (pairs, 2, FFT_N1, FFT_N2, HY_W)

    a = _dft_a(consts['sig_l'], consts['sig_r'], [(proj3, proj_map(0, 0)), (proj3, proj_map(0, 1))],
               pairs, n_slabs, conv_args(0))
    b = _dft_mid(consts['fwd'], consts['inv'], ah5, ssq, layer, 0, a.reshape(a5_shape))
    z1 = _dft_c(consts['out_l'], consts['out_r'], b.reshape(a.shape),
                (proj3, proj_map(0, 0), proj_map(0, 1)), (proj3, proj_map(1, 0), proj_map(1, 1)),
                conv_args(0), conv_args(1), skip[0], F32)
    a = _dft_a(consts['sig_l'], consts['sig_r'], [(z1, plain_map(0)), (z1, plain_map(1))], pairs, n_slabs)
    b = _dft_mid(consts['fwd'], consts['inv'], ah5, ssq, layer, 1, a.reshape(a5_shape))
    return _dft_c(consts['out_l'], consts['out_r'], b.reshape(a.shape),
                  (z1, plain_map(0), plain_map(1)), (proj3, proj_map(2, 0), proj_map(2, 1)),
                  None, conv_args(2), skip[1], BF16)


def _hyena_filter_stage(consts, hy_w1, hy_b1, hy_w2, hy_b2, hy_w3, hy_sin_freq, hy_decay):
    hf, hb, ssq = _hy_filters(hy_w1, hy_b1, hy_w2, hy_b2, hy_w3, hy_sin_freq, hy_decay)
    cw = HY_ORDER * HY_W
    fmap = lambda l, s: (l, 0, s)
    ha = _dft_a(consts['sig_l'], consts['fil_r'], [(hf, fmap), (hb, fmap)], DEPTH, cw // LANES)
    return ha.reshape(DEPTH, 2, FFT_N1, FFT_N2, cw), ssq


def _mlstm_chunk(q_ref, k_ref, v_ref, gc_ref, c_scr, m_scr, reverse, i_off, f_off, state_off):
    ch = ML_CHUNK
    ri = lax.broadcasted_iota(jnp.int32, (ch, ch), 0)
    ci = lax.broadcasted_iota(jnp.int32, (ch, ch), 1)
    tri = (ci >= ri) if reverse else (ci <= ri)
    tri_f = tri.astype(F32)
    gc = gc_ref[...]
    gr = gc.T
    b_col = jnp.dot(tri_f, _log_sigmoid(gc), precision=HIGHEST, preferred_element_type=F32)
    b_row = lax.dot_general(_log_sigmoid(gr[:4 * ML_HEADS, :]), tri_f, (((1,), (1,)), ((), ())), precision=HIGHEST,
                            preferred_element_type=F32)
    last = 0 if reverse else ch - 1
    lane = lax.broadcasted_iota(jnp.int32, (ch, ML_HEAD_DIM), 1)
    ones_col = jnp.where(lane == 0, 1.0, 0.0).astype(BF16)

    outs = []
    for h in range(ML_HEADS):
        sl = slice(h * ML_HEAD_DIM, (h + 1) * ML_HEAD_DIM)
        st = state_off + h
        q = (q_ref[:, sl] * (ML_HEAD_DIM ** -0.5)).astype(BF16)
        k = k_ref[:, sl]
        v_aug = jnp.concatenate([v_ref[:, sl].astype(BF16), ones_col], axis=1)
        bc = b_col[:, f_off + h:f_off + h + 1]
        lic = gc[:, i_off + h:i_off + h + 1]
        br = b_row[f_off + h:f_off + h + 1, :]
        lir = gr[i_off + h:i_off + h + 1, :]
        b_tot = bc[last:last + 1, :]
        d = jnp.where(tri, bc - br + lir, -jnp.inf)
        w_end = b_tot - bc + lic
        m_loc = jnp.max(w_end, axis=0, keepdims=True)
        e_end = jnp.exp(w_end - m_loc)
        m_prev = m_scr[st:st + 1, 0:1]
        c_prev = c_scr[st]
        m_inter = bc + m_prev
        m_t = jnp.maximum(m_inter, jnp.max(d, axis=-1, keepdims=True))
        e_inter = jnp.exp(m_inter - m_t)
        qk = lax.dot_general(q, k.astype(BF16), (((1,), (1,)), ((), ())), preferred_element_type=F32)
        s = qk * jnp.exp(d - m_t)
        nd = _bdot(s, v_aug) + e_inter * _bdot(q, c_prev)
        num = nd[:, :ML_HEAD_DIM]
        den = nd[:, ML_HEAD_DIM:ML_HEAD_DIM + 1]
        outs.append(num / jnp.maximum(jnp.abs(den), jnp.exp(-m_t)))
        m_new = jnp.maximum(b_tot + m_prev, m_loc)
        decay = jnp.exp(b_tot + m_prev - m_new)
        gain = jnp.exp(m_loc - m_new)
        dc = lax.dot_general((k * e_end).astype(BF16), v_aug, (((0,), (0,)), ((), ())),
                             preferred_element_type=F32)
        c_scr[st] = decay * c_prev + gain * dc
        m_scr[st:st + 1, :] = jnp.broadcast_to(m_new, (1, LANES))
    return outs


ML_BATCH_ROWS = 1


def _mlstm_kernel(*refs):
    n_in = 8 * ML_BATCH_ROWS
    hf_ref, hb_ref, c_scr, m_scr = refs[n_in:]

    @pl.when(pl.program_id(1) == 0)
    def _():
        c_scr[...] = jnp.zeros_like(c_scr)
        m_scr[...] = jnp.zeros_like(m_scr)

    for r in range(ML_BATCH_ROWS):
        qf_ref, kf_ref, vf_ref, gf_ref, qb_ref, kb_ref, vb_ref, gb_ref = refs[8 * r:8 * r + 8]
        st = 2 * ML_HEADS * r
        outs_f = _mlstm_chunk(qf_ref, kf_ref, vf_ref, gf_ref, c_scr, m_scr, False, 0, ML_HEADS, st)
        outs_b = _mlstm_chunk(qb_ref, kb_ref, vb_ref, gb_ref, c_scr, m_scr, True, 2 * ML_HEADS, 3 * ML_HEADS,
                              st + ML_HEADS)
        hf_ref[r] = jnp.concatenate(outs_f, axis=1)
        hb_ref[r] = jnp.concatenate(outs_b, axis=1)


def _mixer_mlstm(proj2, bsz):
    nc = SEQ // ML_CHUNK
    rows = ML_BATCH_ROWS

    def specs(r, chunk_of):
        def at(col, width):
            def index(i, j):
                return pl.multiple_of((i * rows + r) * SEQ + chunk_of(j) * ML_CHUNK, ML_CHUNK), col
            return pl.BlockSpec((pl.Element(ML_CHUNK), pl.Element(width)), index)
        return [at(COL_DQ, GROUP_W), at(COL_DK, GROUP_W), at(COL_DV, GROUP_W), at(COL_GATES, LANES)]

    fwd_of = lambda j: j
    bwd_of = lambda j: nc - 1 - j
    in_specs = []
    for r in range(rows):
        in_specs += specs(r, fwd_of) + specs(r, bwd_of)
    out = lambda chunk_of: pl.BlockSpec((rows, ML_CHUNK, GROUP_W), lambda i, j: (i, chunk_of(j), 0))
    n_state = 2 * ML_HEADS * rows
    return pl.pallas_call(
        _mlstm_kernel,
        grid=(bsz // rows, nc),
        in_specs=in_specs,
        out_specs=[out(fwd_of), out(bwd_of)],
        out_shape=[jax.ShapeDtypeStruct((bsz, SEQ, GROUP_W), F32)] * 2,
        scratch_shapes=[pltpu.VMEM((n_state, ML_HEAD_DIM, 2 * ML_HEAD_DIM), F32), pltpu.VMEM((n_state, LANES), F32)],
        compiler_params=_cparams(("parallel", "arbitrary")),
        name="mlstm",
    )(*([proj2] * (8 * rows)))


def _outproj_kernel(lf_ref, lb_ref, ga_ref, yb_ref, yc_ref, mf_ref, mb_ref, o_ref, mg_ref, w_ref, x_ref, m_ref,
                    out_ref):
    ya = (jax.nn.gelu(ga_ref[...]) * (lf_ref[...] + lb_ref[...])).astype(BF16)
    normed = []
    for h in range(ML_HEADS):
        sl = slice(h * ML_HEAD_DIM, (h + 1) * ML_HEAD_DIM)
        hh = mf_ref[:, sl] + mb_ref[:, sl]
        ms = jnp.mean(hh * hh, axis=-1, keepdims=True)
        normed.append(hh * lax.rsqrt(ms + EPS) * mg_ref[:, sl])
    yd = (_sigmoid(o_ref[...]) * jnp.concatenate(normed, axis=1)).astype(BF16)
    acc = jnp.dot(ya, w_ref[0:GROUP_W, :], preferred_element_type=F32)
    acc += jnp.dot(yb_ref[...], w_ref[GROUP_W:2 * GROUP_W, :], preferred_element_type=F32)
    acc += jnp.dot(yc_ref[...], w_ref[2 * GROUP_W:3 * GROUP_W, :], preferred_element_type=F32)
    acc += jnp.dot(yd, w_ref[3 * GROUP_W:, :], preferred_element_type=F32)
    out_ref[...] = x_ref[...] + m_ref[2:3, :] * acc


def _outproj(lru_f, lru_b, proj2, y_b, y_c, ml_f, ml_b, ml_g, w_all, layer, x2, mod_l):
    n = x2.shape[0]
    tm = 512
    per_b = SEQ // tm
    grp = pl.BlockSpec((tm, GROUP_W), lambda i: (i, 0))
    return pl.pallas_call(
        _outproj_kernel,
        grid=(n // tm,),
        in_specs=[grp, grp, pl.BlockSpec((tm, GROUP_W), lambda i: (i, COL_AG // GROUP_W)), grp, grp, grp, grp,
                  pl.BlockSpec((pl.Element(tm), pl.Element(GROUP_W)), lambda i: (pl.multiple_of(i * tm, tm), COL_DO)),
                  pl.BlockSpec((1, GROUP_W), lambda i: (0, 0)),
                  pl.BlockSpec((None, D_MODEL, D_MODEL), lambda i: (layer, 0, 0)),
                  pl.BlockSpec((tm, D_MODEL), lambda i: (i, 0)),
                  pl.BlockSpec((None, 6, D_MODEL), lambda i: (i // per_b, 0, 0))],
        out_specs=pl.BlockSpec((tm, D_MODEL), lambda i: (i, 0)),
        out_shape=jax.ShapeDtypeStruct((n, D_MODEL), F32),
        compiler_params=_cparams(("parallel",)),
        name="out_proj",
    )(lru_f, lru_b, proj2, y_b, y_c, ml_f, ml_b, proj2, ml_g.reshape(1, GROUP_W), w_all, x2, mod_l)


def _ffn_kernel(x_ref, m_ref, g_ref, w1_ref, w3_ref, w2_ref, fg_ref, o_ref, h_scr, *, final):
    j = pl.program_id(1)

    @pl.when(j == 0)
    def _():
        h_scr[...] = _rms_mod(x_ref[...], g_ref[...], m_ref[4:5, :], m_ref[3:4, :]).astype(BF16)
        o_ref[...] = jnp.zeros_like(o_ref)

    h = h_scr[...]
    a = jnp.dot(h, w1_ref[...], preferred_element_type=F32)
    b = jnp.dot(h, w3_ref[...], preferred_element_type=F32)
    act = (a * _sigmoid(a)) * b
    o_ref[...] += jnp.dot(act.astype(BF16), w2_ref[...], preferred_element_type=F32)

    @pl.when(j == pl.num_programs(1) - 1)
    def _():
        y = x_ref[...] + m_ref[5:6, :] * o_ref[...]
        if final:
            ms = jnp.mean(y * y, axis=-1, keepdims=True)
            y = y * lax.rsqrt(ms + EPS) * fg_ref[...]
        o_ref[...] = y


def _ffn(x2, mod_l, g, w1_all, w3_all, w2_all, layer, final_g, final):
    n = x2.shape[0]
    tm, tf = 512, 512
    per_b = SEQ // tm
    row = lambda: pl.BlockSpec((1, D_MODEL), lambda i, j: (0, 0))
    return pl.pallas_call(
        functools.partial(_ffn_kernel, final=final),
        grid=(n // tm, D_FF // tf),
        in_specs=[pl.BlockSpec((tm, D_MODEL), lambda i, j: (i, 0)),
                  pl.BlockSpec((None, 6, D_MODEL), lambda i, j: (i // per_b, 0, 0)),
                  row(),
                  pl.BlockSpec((None, D_MODEL, tf), lambda i, j: (layer, 0, j)),
                  pl.BlockSpec((None, D_MODEL, tf), lambda i, j: (layer, 0, j)),
                  pl.BlockSpec((None, tf, D_MODEL), lambda i, j: (layer, j, 0)),
                  row()],
        out_specs=pl.BlockSpec((tm, D_MODEL), lambda i, j: (i, 0)),
        out_shape=jax.ShapeDtypeStruct((n, D_MODEL), F32),
        scratch_shapes=[pltpu.VMEM((tm, D_MODEL), BF16)],
        compiler_params=_cparams(("parallel", "arbitrary")),
        name="ffn",
    )(x2, mod_l, g.reshape(1, D_MODEL), w1_all, w3_all, w2_all, final_g.reshape(1, D_MODEL))


def kernel(x, c, w_in, b_in, w_out, norm_mix_g, norm_ffn_g, ada_w, ada_b, lru_conv_w, lru_conv_b, lru_wa, lru_ba, lru_wx, lru_bx, lru_lambda, att_q_norm_g, att_k_norm_g, hy_conv_w, hy_conv_b, hy_w1, hy_b1, hy_w2, hy_b2, hy_w3, hy_sin_freq, hy_decay, hy_skip, ml_norm_g, ffn_w1, ffn_w3, ffn_w2, final_g):
    bsz = x.shape[0]
    assert x.shape == (bsz, SEQ, D_MODEL) and bsz % 2 == 0
    n = bsz * SEQ
    mod = _ada_all(c, ada_w, ada_b)

    consts = {k: v.astype(BF16) for k, v in _dft_constants().items()}
    ah5, ssq = _hyena_filter_stage(consts, hy_w1, hy_b1, hy_w2, hy_b2, hy_w3, hy_sin_freq, hy_decay)

    pad = D_IN_PAD - D_IN
    w_in_b = jnp.pad(w_in.astype(BF16), ((0, 0), (0, 0), (0, pad)))
    b_in_p = jnp.pad(b_in, ((0, 0), (0, pad))).reshape(DEPTH, 1, D_IN_PAD)
    w_out_b = w_out.astype(BF16)
    w1_b, w3_b, w2_b = ffn_w1.astype(BF16), ffn_w3.astype(BF16), ffn_w2.astype(BF16)

    x2 = x.reshape(n, D_MODEL)
    for l in range(DEPTH):
        proj2 = _inproj(x2, mod[l], norm_mix_g[l], w_in_b, b_in_p, l)
        proj3 = proj2.reshape(bsz, SEQ, D_IN_PAD)
        lru_f, lru_b = _mixer_rglru(proj3, lru_conv_w[l], lru_conv_b[l], lru_wa[l], lru_ba[l], lru_wx[l], lru_bx[l],
                                    lru_lambda[l])
        y_b = _mixer_attention(proj3, att_q_norm_g[l], att_k_norm_g[l])
        y_c = _mixer_hyena(proj3, hy_conv_w[l], hy_conv_b[l], hy_skip[l], ah5, ssq, l, consts)
        ml_f, ml_b = _mixer_mlstm(proj2, bsz)
        flat = lambda a: a.reshape(n, GROUP_W)
        x2 = _outproj(flat(lru_f), flat(lru_b), proj2, flat(y_b), flat(y_c), flat(ml_f), flat(ml_b), ml_norm_g[l],
                      w_out_b, l, x2, mod[l])
        x2 = _ffn(x2, mod[l], norm_ffn_g[l], w1_b, w3_b, w2_b, l, final_g, final=(l == DEPTH - 1))
    return x2.reshape(bsz, SEQ, D_MODEL)
```
